```python
import jax, jax.numpy as jnp
from jax import lax
import numpy as np

D_MODEL = 1024
BATCH = 8
SEQ = 2048
DEPTH = 4

D_MIX = D_MODEL
POOL_WIDTH = D_MIX // 4
POOL_WINDOWS = (2, 4, 8, 16)
POOL_GROUPS = len(POOL_WINDOWS)
POOL_GROUP_DIM = POOL_WIDTH // POOL_GROUPS
HGRN_WIDTH = D_MIX // 4
HGRN_HEAD_DIM = 64
HGRN_HEADS = HGRN_WIDTH // HGRN_HEAD_DIM
HGRN_CHUNK = 64
LOG_F_FLOOR = 1e-30
ATTN_WIDTH = D_MIX - POOL_WIDTH - HGRN_WIDTH
HEAD_DIM = 64
N_HEADS = ATTN_WIDTH // HEAD_DIM
N_KV_HEADS = 2
KV_GROUP = N_HEADS // N_KV_HEADS
KV_WIDTH = N_KV_HEADS * HEAD_DIM
IDX_HEADS = 4
IDX_DIM = 64
TOPK_MAX = 256
Q_BLOCK = 128
MASK_VALUE = -1e30
ROPE_THETA = 500000.0
ROT_DIM = HEAD_DIM // 4
D_FF = 4 * D_MODEL
RMS_EPS = 1e-5
IN_SIZES = (POOL_WIDTH,
            HGRN_WIDTH, HGRN_WIDTH, HGRN_WIDTH, HGRN_WIDTH,
            ATTN_WIDTH, KV_WIDTH, KV_WIDTH,
            IDX_HEADS * IDX_DIM, IDX_DIM, IDX_HEADS)
D_IN = sum(IN_SIZES)

kernel_name = "hybrid_pool_hgrn2_dsa_trunk"


def rms_norm(x, g):
    xf = x.astype(jnp.float32)
    y = xf * lax.rsqrt(jnp.mean(xf * xf, axis=-1, keepdims=True) + RMS_EPS)
    return (y * g.astype(jnp.float32)).astype(x.dtype)


def split_cols(a, sizes):
    outs, start = [], 0
    for s in sizes:
        outs.append(a[..., start:start + s])
        start += s
    return outs


def rope_tables(positions):
    inv_freq = ROPE_THETA ** (-jnp.arange(0, ROT_DIM, 2, dtype=jnp.float32) / ROT_DIM)
    ang = positions.astype(jnp.float32)[..., None] * inv_freq
    return jnp.cos(ang), jnp.sin(ang)


def apply_partial_rope(x, cos, sin):
    half = ROT_DIM // 2
    shape = cos.shape[:2] + (1,) * (x.ndim - 3) + (half,)
    c, s = cos.reshape(shape), sin.reshape(shape)
    xf = x.astype(jnp.float32)
    x1, x2 = xf[..., :half], xf[..., half:ROT_DIM]
    out = jnp.concatenate([x1 * c - x2 * s, x2 * c + x1 * s, xf[..., ROT_DIM:]], axis=-1)
    return out.astype(x.dtype)


def multiscale_pool(u, w, scale):
    B, T, _ = u.shape
    uf = u.astype(jnp.float32).reshape(B, T, POOL_GROUPS, POOL_GROUP_DIM)
    cs = jnp.cumsum(uf, axis=1)
    t_idx = jnp.arange(T)
    means = []
    for g, win in enumerate(POOL_WINDOWS):
        c = cs[:, :, g]
        lag = jnp.pad(c, ((0, 0), (win, 0), (0, 0)))[:, :T]
        count = jnp.minimum(t_idx + 1, win).astype(jnp.float32)[None, :, None]
        means.append((c - lag) / count)
    pooled = jnp.stack(means, axis=2) - uf
    y = jnp.einsum('btgc,gcd->btgd', pooled, w.astype(jnp.float32))
    y = y.reshape(B, T, POOL_WIDTH) * scale.astype(jnp.float32)
    return y.astype(u.dtype)


def hgrn2(q_raw, f_raw, i_raw, g_raw, lb, norm_g):
    B, T, _ = q_raw.shape
    H, dh, C = HGRN_HEADS, HGRN_HEAD_DIM, HGRN_CHUNK
    N = T // C
    lb = lb.astype(jnp.float32)
    z = f_raw.astype(jnp.float32)
    q = jax.nn.silu(q_raw.astype(jnp.float32))
    sig = jax.nn.sigmoid(z)
    f = lb + (1.0 - lb) * sig
    log_f = jnp.log(jnp.maximum(f, LOG_F_FLOOR))
    k = (1.0 - lb) * (1.0 - sig)
    v = i_raw.astype(jnp.float32)

    def to_chunks(a):
        return a.reshape(B, N, C, H, dh).transpose(1, 0, 3, 2, 4)

    causal = jnp.tril(jnp.ones((C, C), dtype=bool))[:, :, None]

    def step(S, xs):
        qc, kc, lfc, vc = xs
        b = jnp.cumsum(lfc, axis=2)
        diff = b[:, :, :, None, :] - b[:, :, None, :, :]
        decay = jnp.where(causal, jnp.exp(jnp.minimum(diff, 0.0)), 0.0)
        scores = jnp.einsum('bhtk,bhtsk,bhsk->bhts', qc, decay, kc)
        o = jnp.einsum('bhts,bhsv->bhtv', scores, vc) \
            + jnp.einsum('bhtk,bhkv->bhtv', qc * jnp.exp(b), S)
        b_last = b[:, :, -1:, :]
        S = jnp.exp(b_last[:, :, 0, :])[..., None] * S \
            + jnp.einsum('bhsk,bhsv->bhkv', kc * jnp.exp(b_last - b), vc)
        return S, o

    S0 = jnp.zeros((B, H, dh, dh), jnp.float32)
    _, o = lax.scan(step, S0, (to_chunks(q), to_chunks(k), to_chunks(log_f), to_chunks(v)))
    o = o.transpose(1, 0, 3, 2, 4).reshape(B, T, H, dh)
    o = o * lax.rsqrt(jnp.mean(o * o, axis=-1, keepdims=True) + RMS_EPS) \
        * norm_g.astype(jnp.float32).reshape(H, dh)
    o = o.reshape(B, T, HGRN_WIDTH) * jax.nn.silu(g_raw.astype(jnp.float32))
    return o.astype(q_raw.dtype)


def dsa_sparse_attention(q, k, v, q_idx, k_idx, w_idx):
    B, T = q.shape[:2]
    nb = T // Q_BLOCK
    top_k = min(TOPK_MAX, T // 4)
    key_pos = jnp.arange(T)
    k_idx32 = k_idx.astype(jnp.float32)
    scale = HEAD_DIM ** -0.5

    def blocks(a):
        return jnp.moveaxis(a.reshape((B, nb, Q_BLOCK) + a.shape[2:]), 1, 0)

    def one_block(args):
        qb, qib, wb, start = args
        qpos = start + jnp.arange(Q_BLOCK)
        s_idx = jnp.einsum('bqhd,bsd->bqhs', qib.astype(jnp.float32), k_idx32)
        I = jnp.einsum('bqhs,bqh->bqs', jax.nn.relu(s_idx), wb.astype(jnp.float32))
        causal = key_pos[None, :] <= qpos[:, None]
        I = jnp.where(causal[None], I, MASK_VALUE)
        _, sel = lax.top_k(I, top_k)
        valid = sel <= qpos[None, :, None]
        kg = jax.vmap(lambda kk, ii: kk[ii])(k, sel)
        vg = jax.vmap(lambda vv, ii: vv[ii])(v, sel)
        sc = jnp.einsum('bqhgd,bqkhd->bqhgk', qb, kg).astype(jnp.float32) * scale
        sc = jnp.where(valid[:, :, None, None, :], sc, MASK_VALUE)
        p = jax.nn.softmax(sc, axis=-1).astype(v.dtype)
        return jnp.einsum('bqhgk,bqkhd->bqhgd', p, vg)

    starts = jnp.arange(nb) * Q_BLOCK
    out = lax.map(one_block, (blocks(q), blocks(q_idx), blocks(w_idx), starts))
    return jnp.moveaxis(out, 0, 1).reshape(B, T, ATTN_WIDTH)


def setup_inputs(seed: int = 0) -> dict:
    key = jax.random.key(seed)
    ks = jax.random.split(key, 14)
    f32 = jnp.float32
    x = jax.random.normal(ks[0], (BATCH, SEQ, D_MODEL), f32)
    offsets = jax.random.randint(ks[1], (BATCH, 1), 0, 4096, dtype=jnp.int32)
    positions = (offsets + jnp.arange(SEQ, dtype=jnp.int32)[None, :]).astype(jnp.int32)
    norm1_g = 1.0 + 0.02 * jax.random.normal(ks[2], (DEPTH, D_MODEL), f32)
    w_in = jax.random.normal(ks[3], (DEPTH, D_MODEL, D_IN), f32) * D_MODEL ** -0.5
    pool_w = jax.random.normal(ks[4], (DEPTH, POOL_GROUPS, POOL_GROUP_DIM, POOL_GROUP_DIM), f32) * POOL_GROUP_DIM ** -0.5
    pool_scale = 1.0 + 0.02 * jax.random.normal(ks[5], (DEPTH, POOL_WIDTH), f32)
    lb_logits = 0.5 * jax.random.normal(ks[6], (DEPTH, HGRN_WIDTH), f32)
    hgrn_norm_g = 1.0 + 0.02 * jax.random.normal(ks[7], (DEPTH, HGRN_WIDTH), f32)
    w_out = jax.random.normal(ks[8], (DEPTH, D_MIX, D_MODEL), f32) * D_MIX ** -0.5
    norm2_g = 1.0 + 0.02 * jax.random.normal(ks[9], (DEPTH, D_MODEL), f32)
    w_ff_in = jax.random.normal(ks[10], (DEPTH, D_MODEL, D_FF), f32) * D_MODEL ** -0.5
    w_ff_out = jax.random.normal(ks[11], (DEPTH, D_FF, D_MODEL), f32) * D_FF ** -0.5
    final_norm_g = 1.0 + 0.02 * jax.random.normal(ks[12], (D_MODEL,), f32)
    return {"x": x, "positions": positions, "norm1_g": norm1_g, "w_in": w_in,
            "pool_w": pool_w, "pool_scale": pool_scale, "lb_logits": lb_logits,
            "hgrn_norm_g": hgrn_norm_g, "w_out": w_out, "norm2_g": norm2_g,
            "w_ff_in": w_ff_in, "w_ff_out": w_ff_out, "final_norm_g": final_norm_g}


def reference(x, positions, norm1_g, w_in, pool_w, pool_scale, lb_logits, hgrn_norm_g,
              w_out, norm2_g, w_ff_in, w_ff_out, final_norm_g):
    B, T, _ = x.shape
    cos, sin = rope_tables(positions)
    p_lb = jax.nn.softmax(lb_logits.astype(jnp.float32), axis=0)
    lower_bounds = jnp.cumsum(p_lb, axis=0) - p_lb[0]
    idx_w_scale = (IDX_HEADS ** -0.5) * (IDX_DIM ** -0.5)
    for layer in range(DEPTH):
        h = rms_norm(x, norm1_g[layer])
        proj = h @ w_in[layer]
        (pool_in, hq, hf, hi, hg, q, k, v, qi, ki, wi) = split_cols(proj, IN_SIZES)
        y_pool = multiscale_pool(pool_in, pool_w[layer], pool_scale[layer])
        y_hgrn = hgrn2(hq, hf, hi, hg, lower_bounds[layer], hgrn_norm_g[layer])
        q = apply_partial_rope(q.reshape(B, T, N_KV_HEADS, KV_GROUP, HEAD_DIM), cos, sin)
        k = apply_partial_rope(k.reshape(B, T, N_KV_HEADS, HEAD_DIM), cos, sin)
        v = v.reshape(B, T, N_KV_HEADS, HEAD_DIM)
        qi = apply_partial_rope(qi.reshape(B, T, IDX_HEADS, IDX_DIM), cos, sin)
        ki = apply_partial_rope(ki, cos, sin)
        y_attn = dsa_sparse_attention(q, k, v, qi, ki, wi * idx_w_scale)
        mixed = jnp.concatenate([y_pool.astype(x.dtype), y_hgrn.astype(x.dtype),
                                 y_attn.astype(x.dtype)], axis=-1)
        x = x + mixed @ w_out[layer]
        h = rms_norm(x, norm2_g[layer])
        x = x + jnp.square(jax.nn.relu(h @ w_ff_in[layer])) @ w_ff_out[layer]
    return rms_norm(x, final_norm_g)
```

```python
import functools

import jax
import jax.numpy as jnp
from jax import lax
from jax.experimental import pallas as pl
from jax.experimental.pallas import tpu as pltpu

F32 = jnp.float32
BF16 = jnp.bfloat16

POOL_WINDOWS = (2, 4, 8, 16)
POOL_WIDTH = 256
POOL_GROUP_DIM = 64
HGRN_WIDTH = 256
HGRN_HEAD_DIM = 64
LOG_F_FLOOR = 1e-30
ATTN_WIDTH = 512
HEAD_DIM = 64
N_HEADS = 8
N_KV_HEADS = 2
KV_GROUP = N_HEADS // N_KV_HEADS
KV_WIDTH = N_KV_HEADS * HEAD_DIM
IDX_HEADS = 4
IDX_DIM = 64
TOPK_MAX = 256
MASK_VALUE = -1e30
ROPE_THETA = 500000.0
ROT_DIM = HEAD_DIM // 4
ROT_HALF = ROT_DIM // 2
RMS_EPS = 1e-5
IDX_W_SCALE = (IDX_HEADS ** -0.5) * (IDX_DIM ** -0.5)
ATTN_SCALE = HEAD_DIM ** -0.5

LANES = 128
SUBLANES = 8
VMEM_LIMIT_BYTES = 56 * 1024 * 1024

COL_POOL = 0
COL_HGRN = COL_POOL + POOL_WIDTH
COL_Q = COL_HGRN + 4 * HGRN_WIDTH
COL_K = COL_Q + ATTN_WIDTH
COL_V = COL_K + KV_WIDTH
COL_QI = COL_V + KV_WIDTH
COL_KI = COL_QI + IDX_HEADS * IDX_DIM
D_IN = COL_KI + IDX_DIM + IDX_HEADS
D_IN_PAD = COL_KI + LANES

PROJ_TILE = 512
HGRN_TILE = 256
HGRN_CHUNK = 16
Q_TILE = 128
KEY_TILE = 512
INT_MIN = -2 ** 31


def _nt_dot(a, b):
    return lax.dot_general(a, b, (((1,), (1,)), ((), ())), preferred_element_type=F32)


def _tn_dot(a, b):
    return lax.dot_general(a, b, (((0,), (0,)), ((), ())), preferred_element_type=F32)


def _dot(a, b):
    return jnp.dot(a, b, preferred_element_type=F32)


def _split_bf16(a):
    hi = a.astype(BF16)
    lo = (a - hi.astype(F32)).astype(BF16)
    return hi, lo


def _rms_norm(x, g):
    return x * lax.rsqrt(jnp.mean(x * x, axis=-1, keepdims=True) + RMS_EPS) * g


def _silu(x):
    return x * jax.nn.sigmoid(x)


def _rope(x, cos, sin_lo, sin_hi):
    w = x.shape[1]
    rep = w // LANES
    if rep > 1:
        cos = jnp.concatenate([cos] * rep, axis=1)
        sin_lo = jnp.concatenate([sin_lo] * rep, axis=1)
        sin_hi = jnp.concatenate([sin_hi] * rep, axis=1)
    from_below = pltpu.roll(x, ROT_HALF, 1)
    from_above = pltpu.roll(x, w - ROT_HALF, 1)
    return x * cos + from_below * sin_hi + from_above * sin_lo


def _in_proj_kernel(x_ref, g_ref, w_ref, cos_ref, slo_ref, shi_ref, pw_ref, pscale_ref,
                    ypool_ref, hg_ref, q_ref, k_ref, v_ref, qi_ref, kiw_ref,
                    halo_ref, *, tiles_per_seq):
    tm = x_ref.shape[0]
    halo = halo_ref.shape[0]
    i = pl.program_id(0)
    seq_tile = i % tiles_per_seq

    h = _rms_norm(x_ref[...], g_ref[...])
    proj = _dot(h.astype(BF16), w_ref[...])

    @pl.when(seq_tile == 0)
    def _():
        halo_ref[...] = jnp.zeros_like(halo_ref)

    u = proj[:, COL_POOL:COL_POOL + POOL_WIDTH]
    ext = jnp.concatenate([halo_ref[...], u], axis=0)
    halo_ref[...] = u[tm - halo:, :]
    sums = []
    acc = ext
    span = 1
    for win in POOL_WINDOWS:
        while span < win:
            acc = acc + pltpu.roll(acc, span, 0)
            span *= 2
        sums.append(acc[halo:, :])
    lane = lax.broadcasted_iota(jnp.int32, (tm, POOL_WIDTH), 1)
    group = lane // POOL_GROUP_DIM
    wsum = sums[-1]
    win_lane = jnp.full((tm, POOL_WIDTH), POOL_WINDOWS[-1], jnp.int32)
    for gi in range(len(POOL_WINDOWS) - 2, -1, -1):
        wsum = jnp.where(group == gi, sums[gi], wsum)
        win_lane = jnp.where(group == gi, POOL_WINDOWS[gi], win_lane)
    t_pos = seq_tile * tm + lax.broadcasted_iota(jnp.int32, (tm, POOL_WIDTH), 0)
    count = jnp.minimum(t_pos + 1, win_lane).astype(F32)
    pooled = wsum / count - u
    y_pool = _dot(pooled.astype(BF16), pw_ref[...]) * pscale_ref[...]
    ypool_ref[...] = y_pool.astype(ypool_ref.dtype)

    hg_ref[...] = proj[:, COL_HGRN:COL_HGRN + 4 * HGRN_WIDTH]

    cos, slo, shi = cos_ref[...], slo_ref[...], shi_ref[...]
    q = _rope(proj[:, COL_Q:COL_Q + ATTN_WIDTH], cos, slo, shi) * ATTN_SCALE
    q_ref[...] = q.astype(q_ref.dtype)
    k = _rope(proj[:, COL_K:COL_K + KV_WIDTH], cos, slo, shi)
    k_ref[...] = k.astype(k_ref.dtype)
    v_ref[...] = proj[:, COL_V:COL_V + KV_WIDTH].astype(v_ref.dtype)
    qi_ref[...] = _rope(proj[:, COL_QI:COL_QI + IDX_HEADS * IDX_DIM], cos, slo, shi)
    kiw_raw = proj[:, COL_KI:COL_KI + LANES]
    lane128 = lax.broadcasted_iota(jnp.int32, (tm, LANES), 1)
    kiw_ref[...] = jnp.where(lane128 < IDX_DIM, _rope(kiw_raw, cos, slo, shi),
                             kiw_raw * IDX_W_SCALE)


def _in_proj(x2d, norm_g, w_in_pad, cos, slo, shi, pool_w_bd, pool_scale, layer, seq_len):
    n_tok, d_model = x2d.shape
    tm = min(PROJ_TILE, seq_len)
    assert seq_len % tm == 0 and n_tok % tm == 0
    halo = 2 * SUBLANES
    assert halo >= POOL_WINDOWS[-1] and tm >= halo
    tok = lambda w: pl.BlockSpec((tm, w), lambda i: (i, 0))
    lay = lambda *s: pl.BlockSpec((None,) + s, lambda i: (layer,) + (0,) * len(s))
    out_shape = (
        jax.ShapeDtypeStruct((n_tok, POOL_WIDTH), BF16),
        jax.ShapeDtypeStruct((n_tok, 4 * HGRN_WIDTH), F32),
        jax.ShapeDtypeStruct((n_tok, ATTN_WIDTH), BF16),
        jax.ShapeDtypeStruct((n_tok, KV_WIDTH), BF16),
        jax.ShapeDtypeStruct((n_tok, KV_WIDTH), BF16),
        jax.ShapeDtypeStruct((n_tok, IDX_HEADS * IDX_DIM), F32),
        jax.ShapeDtypeStruct((n_tok, LANES), F32),
    )
    return pl.pallas_call(
        functools.partial(_in_proj_kernel, tiles_per_seq=seq_len // tm),
        grid=(n_tok // tm,),
        in_specs=[tok(d_model), lay(1, d_model), lay(d_model, D_IN_PAD),
                  tok(LANES), tok(LANES), tok(LANES),
                  lay(POOL_WIDTH, POOL_WIDTH), lay(1, POOL_WIDTH)],
        out_specs=tuple(tok(s.shape[1]) for s in out_shape),
        out_shape=out_shape,
        scratch_shapes=[pltpu.VMEM((halo, POOL_WIDTH), F32)],
        compiler_params=pltpu.CompilerParams(dimension_semantics=("arbitrary",),
                                             vmem_limit_bytes=VMEM_LIMIT_BYTES),
        name="in_proj",
    )(x2d, norm_g, w_in_pad, cos, slo, shi, pool_w_bd, pool_scale)


def _hgrn_kernel(hg_ref, lb_ref, ng_ref, out_ref, state_ref, o_ref):
    tb, width = out_ref.shape
    c = HGRN_CHUNK

    @pl.when(pl.program_id(1) == 0)
    def _():
        state_ref[...] = jnp.zeros_like(state_ref)

    lb = lb_ref[...]
    q = _silu(hg_ref[:, 0:width])
    sig = jax.nn.sigmoid(hg_ref[:, width:2 * width])
    f = lb + (1.0 - lb) * sig
    log_f = jnp.log(jnp.maximum(f, LOG_F_FLOOR))
    k = (1.0 - lb) * (1.0 - sig)
    v = hg_ref[:, 2 * width:3 * width]

    row_in_chunk = lax.broadcasted_iota(jnp.int32, (tb, width), 0) % c
    b = log_f
    step = 1
    while step < c:
        b = b + jnp.where(row_in_chunk >= step, pltpu.roll(b, step, 0), 0.0)
        step *= 2

    lane_r = lax.broadcasted_iota(jnp.int32, (width, width), 0) // HGRN_HEAD_DIM
    lane_c = lax.broadcasted_iota(jnp.int32, (width, width), 1) // HGRN_HEAD_DIM
    same_head = lane_r == lane_c
    head_sum = jnp.where(same_head, 1.0, 0.0).astype(BF16)

    o = None
    for d in range(c):
        if d == 0:
            p, vd = q * k, v
        else:
            kd, bd, vd = (pltpu.roll(a, d, 0) for a in (k, b, v))
            decay = jnp.exp(jnp.minimum(b - bd, 0.0))
            p = jnp.where(row_in_chunk >= d, q * kd * decay, 0.0)
        score = _dot(p.astype(BF16), head_sum)
        o = score * vd if o is None else o + score * vd
    o_ref[...] = o

    for ci in range(tb // c):
        rows = slice(ci * c, (ci + 1) * c)
        bc = b[rows]
        b_last = bc[c - 1:c]
        state = state_ref[...]
        q_dec = q[rows] * jnp.exp(bc)
        o_ref[rows, :] += _nt_dot(q_dec.astype(BF16), state.astype(BF16))
        k_dec = k[rows] * jnp.exp(b_last - bc)
        upd = _tn_dot(v[rows].astype(BF16), k_dec.astype(BF16))
        state_ref[...] = state * jnp.exp(b_last) + jnp.where(same_head, upd, 0.0)

    o = o_ref[...]
    sq_hi, sq_lo = _split_bf16(o * o)
    mean_sq = (_dot(sq_hi, head_sum) + _dot(sq_lo, head_sum)) * (1.0 / HGRN_HEAD_DIM)
    gate = _silu(hg_ref[:, 3 * width:4 * width])
    out_ref[...] = (o * lax.rsqrt(mean_sq + RMS_EPS) * ng_ref[...] * gate).astype(out_ref.dtype)


def _hgrn(hg, lower_bound, norm_g, layer, batch, seq_len):
    n_tok = hg.shape[0]
    tb = min(HGRN_TILE, seq_len)
    assert seq_len % tb == 0 and tb % HGRN_CHUNK == 0 and HGRN_CHUNK % SUBLANES == 0
    steps = seq_len // tb
    lay = pl.BlockSpec((None, 1, HGRN_WIDTH), lambda bi, ti: (layer, 0, 0))
    return pl.pallas_call(
        _hgrn_kernel,
        grid=(batch, steps),
        in_specs=[pl.BlockSpec((tb, 4 * HGRN_WIDTH), lambda bi, ti: (bi * steps + ti, 0)),
                  lay, lay],
        out_specs=pl.BlockSpec((tb, HGRN_WIDTH), lambda bi, ti: (bi * steps + ti, 0)),
        out_shape=jax.ShapeDtypeStruct((n_tok, HGRN_WIDTH), BF16),
        scratch_shapes=[pltpu.VMEM((HGRN_WIDTH, HGRN_WIDTH), F32),
                        pltpu.VMEM((tb, HGRN_WIDTH), F32)],
        compiler_params=pltpu.CompilerParams(dimension_semantics=("arbitrary", "arbitrary"),
                                             vmem_limit_bytes=VMEM_LIMIT_BYTES),
        name="hgrn2",
    )(hg, lower_bound, norm_g)


def _dsa_kernel(q_ref, k_ref, v_ref, qi_ref, kiw_ref, out_ref,
                key_ref, m_ref, l_ref, acc_ref, *, top_k, index_bits):
    qb = q_ref.shape[0]
    kt = key_ref.shape[2]
    sub = kt // LANES
    t0 = pl.program_id(1) * qb
    n_tiles = (t0 + qb + kt - 1) // kt

    row_t = t0 + lax.broadcasted_iota(jnp.int32, (qb, kt), 0)
    lane_kt = lax.broadcasted_iota(jnp.int32, (qb, kt), 1)
    lane_128 = lax.broadcasted_iota(jnp.int32, (qb, LANES), 1)
    ones_bf = jnp.ones((LANES, LANES), BF16)

    qi_hi, qi_lo = _split_bf16(qi_ref[...])
    kiw_q = kiw_ref[pl.ds(pl.multiple_of(t0, qb), qb), :]
    w_heads = [kiw_q[:, IDX_DIM + h:IDX_DIM + h + 1] for h in range(IDX_HEADS)]

    def index_tile(ti, carry):
        base = pl.multiple_of(ti * kt, kt)
        ki_hi, ki_lo = _split_bf16(kiw_ref[pl.ds(base, kt), :][:, 0:IDX_DIM])
        score = jnp.zeros((qb, kt), F32)
        for h in range(IDX_HEADS):
            cols = slice(h * IDX_DIM, (h + 1) * IDX_DIM)
            s = (_nt_dot(qi_hi[:, cols], ki_hi) + _nt_dot(qi_hi[:, cols], ki_lo)
                 + _nt_dot(qi_lo[:, cols], ki_hi))
            score = score + jnp.maximum(s, 0.0) * w_heads[h]
        score = jnp.where(score == 0.0, 0.0, score)
        score = jnp.where(base + lane_kt <= row_t, score, MASK_VALUE)
        bits = lax.bitcast_convert_type(score, jnp.int32)
        key_ref[ti] = bits ^ ((bits >> 31) & 0x7FFFFFFF)
        return carry

    lax.fori_loop(0, n_tiles, index_tile, 0)

    def count(pred):
        def body(ti, acc):
            keys = key_ref[ti]
            for s in range(sub):
                col = ti * kt + s * LANES + lane_128
                acc = acc + jnp.where(pred(keys[:, s * LANES:(s + 1) * LANES], col), 1.0, 0.0)
            return acc
        partial = lax.fori_loop(0, n_tiles, body, jnp.zeros((qb, LANES), F32))
        return _dot(partial.astype(BF16), ones_bf)

    def value_bit(it, thr):
        cand = thr + lax.shift_left(jnp.int32(1), 31 - it)
        n_ge = count(lambda key, col: key >= cand)
        return jnp.where(n_ge >= top_k, cand, thr)

    thr = lax.fori_loop(0, 32, value_bit, jnp.full((qb, LANES), INT_MIN, jnp.int32))
    n_gt = count(lambda key, col: key > thr)
    need = top_k - n_gt

    def index_bit(it, cut):
        cand = cut + lax.shift_left(jnp.int32(1), index_bits - 1 - it)
        n_before = count(lambda key, col: (key == thr) & (col < cand))
        return jnp.where(n_before < need, cand, cut)

    cut = lax.fori_loop(0, index_bits, index_bit, jnp.zeros((qb, LANES), jnp.int32))

    thr_kt = jnp.concatenate([thr] * sub, axis=1)
    cut_kt = jnp.concatenate([cut] * sub, axis=1)

    m_ref[...] = jnp.full_like(m_ref, MASK_VALUE)
    l_ref[...] = jnp.zeros_like(l_ref)
    acc_ref[...] = jnp.zeros_like(acc_ref)

    def attend_tile(ti, carry):
        base = pl.multiple_of(ti * kt, kt)
        keys = key_ref[ti]
        col = base + lane_kt
        chosen = ((keys > thr_kt) | ((keys == thr_kt) & (col <= cut_kt))) & (col <= row_t)
        k_tile = k_ref[pl.ds(base, kt), :]
        v_tile = v_ref[pl.ds(base, kt), :]
        for h in range(N_HEADS):
            g = h // KV_GROUP
            kv_cols = slice(g * HEAD_DIM, (g + 1) * HEAD_DIM)
            s = _nt_dot(q_ref[:, h * HEAD_DIM:(h + 1) * HEAD_DIM], k_tile[:, kv_cols])
            s = jnp.where(chosen, s, MASK_VALUE)
            m_old = m_ref[h]
            m_new = jnp.maximum(m_old, jnp.max(s, axis=-1, keepdims=True))
            alpha = jnp.exp(m_old - m_new)
            p = jnp.exp(s - m_new)
            l_ref[h] = alpha * l_ref[h] + jnp.sum(p, axis=-1, keepdims=True)
            acc_ref[h] = alpha * acc_ref[h] + _dot(p.astype(BF16), v_tile[:, kv_cols])
            m_ref[h] = m_new
        return carry

    lax.fori_loop(0, n_tiles, attend_tile, 0)
    for h in range(N_HEADS):
        out_ref[:, h * HEAD_DIM:(h + 1) * HEAD_DIM] = (acc_ref[h] / l_ref[h]).astype(out_ref.dtype)


def _dsa(q, k, v, qi, kiw, batch, seq_len):
    n_tok = q.shape[0]
    qb = min(Q_TILE, seq_len)
    kt = min(KEY_TILE, seq_len)
    assert seq_len % qb == 0 and seq_len % kt == 0 and kt % qb == 0 and kt % LANES == 0
    steps = seq_len // qb
    top_k = min(TOPK_MAX, seq_len // 4)
    qtile = lambda w: pl.BlockSpec((qb, w), lambda bi, qi_: (bi * steps + qi_, 0))
    seq = lambda w: pl.BlockSpec((seq_len, w), lambda bi, qi_: (bi, 0))
    return pl.pallas_call(
        functools.partial(_dsa_kernel, top_k=top_k, index_bits=(seq_len - 1).bit_length()),
        grid=(batch, steps),
        in_specs=[qtile(ATTN_WIDTH), seq(KV_WIDTH), seq(KV_WIDTH),
                  qtile(IDX_HEADS * IDX_DIM), seq(LANES)],
        out_specs=qtile(ATTN_WIDTH),
        out_shape=jax.ShapeDtypeStruct((n_tok, ATTN_WIDTH), BF16),
        scratch_shapes=[pltpu.VMEM((seq_len // kt, qb, kt), jnp.int32),
                        pltpu.VMEM((N_HEADS, qb, 1), F32),
                        pltpu.VMEM((N_HEADS, qb, 1), F32),
                        pltpu.VMEM((N_HEADS, qb, HEAD_DIM), F32)],
        compiler_params=pltpu.CompilerParams(dimension_semantics=("arbitrary", "arbitrary"),
                                             vmem_limit_bytes=VMEM_LIMIT_BYTES),
        name="dsa",
    )(q, k, v, qi, kiw)


def _out_ffn_kernel(x_ref, yp_ref, yh_ref, ya_ref, wo_hbm, g2_ref, wfi_hbm, wfo_hbm, gf_ref,
                    out_ref, wo_ref, wfi_ref, wfo_ref, *, layer, final_norm):
    @pl.when(pl.program_id(0) == 0)
    def _():
        pltpu.sync_copy(wo_hbm.at[layer], wo_ref)
        pltpu.sync_copy(wfi_hbm.at[layer], wfi_ref)
        pltpu.sync_copy(wfo_hbm.at[layer], wfo_ref)

    x = x_ref[...]
    x = x + _dot(yp_ref[...], wo_ref[0:POOL_WIDTH, :])
    x = x + _dot(yh_ref[...], wo_ref[POOL_WIDTH:POOL_WIDTH + HGRN_WIDTH, :])
    x = x + _dot(ya_ref[...], wo_ref[POOL_WIDTH + HGRN_WIDTH:, :])
    h = _rms_norm(x, g2_ref[...]).astype(BF16)
    a = jnp.maximum(_dot(h, wfi_ref[...]), 0.0)
    x = x + _dot((a * a).astype(BF16), wfo_ref[...])
    if final_norm:
        x = _rms_norm(x, gf_ref[...])
    out_ref[...] = x


def _out_ffn(x2d, y_pool, y_hgrn, y_attn, w_out, norm2_g, w_ff_in, w_ff_out, final_g,
             layer, final_norm, seq_len):
    n_tok, d_model = x2d.shape
    d_ff = w_ff_in.shape[2]
    tm = min(PROJ_TILE, seq_len)
    tok = lambda w: pl.BlockSpec((tm, w), lambda i: (i, 0))
    hbm = pl.BlockSpec(memory_space=pl.ANY)
    return pl.pallas_call(
        functools.partial(_out_ffn_kernel, layer=layer, final_norm=final_norm),
        grid=(n_tok // tm,),
        in_specs=[tok(d_model), tok(POOL_WIDTH), tok(HGRN_WIDTH), tok(ATTN_WIDTH),
                  hbm, pl.BlockSpec((None, 1, d_model), lambda i: (layer, 0, 0)), hbm, hbm,
                  pl.BlockSpec((1, d_model), lambda i: (0, 0))],
        out_specs=tok(d_model),
        out_shape=jax.ShapeDtypeStruct((n_tok, d_model), F32),
        scratch_shapes=[pltpu.VMEM((d_model, d_model), BF16),
                        pltpu.VMEM((d_model, d_ff), BF16),
                        pltpu.VMEM((d_ff, d_model), BF16)],
        compiler_params=pltpu.CompilerParams(dimension_semantics=("arbitrary",),
                                             vmem_limit_bytes=VMEM_LIMIT_BYTES),
        name="out_ffn",
    )(x2d, y_pool, y_hgrn, y_attn, w_out, norm2_g, w_ff_in, w_ff_out, final_g)


def _rope_patterns(positions):
    inv_freq = ROPE_THETA ** (-jnp.arange(0, ROT_DIM, 2, dtype=F32) / ROT_DIM)
    ang = positions.astype(F32).reshape(-1, 1) * inv_freq
    cos, sin = jnp.cos(ang), jnp.sin(ang)
    n = ang.shape[0]
    rest = HEAD_DIM - ROT_DIM
    cos_h = jnp.concatenate([cos, cos, jnp.ones((n, rest), F32)], axis=1)
    slo_h = jnp.concatenate([-sin, jnp.zeros((n, ROT_HALF + rest), F32)], axis=1)
    shi_h = jnp.concatenate([jnp.zeros((n, ROT_HALF), F32), sin, jnp.zeros((n, rest), F32)], axis=1)
    two = lambda a: jnp.concatenate([a, a], axis=1)
    return two(cos_h), two(slo_h), two(shi_h)


def kernel(x, positions, norm1_g, w_in, pool_w, pool_scale, lb_logits, hgrn_norm_g, w_out,
           norm2_g, w_ff_in, w_ff_out, final_norm_g):
    batch, seq_len, d_model = x.shape
    depth = w_in.shape[0]
    assert w_in.shape[2] == D_IN

    p_lb = jax.nn.softmax(lb_logits.astype(F32), axis=0)
    lower_bounds = (jnp.cumsum(p_lb, axis=0) - p_lb[0])[:, None, :]
    w_in_pad = jnp.pad(w_in, ((0, 0), (0, 0), (0, D_IN_PAD - D_IN))).astype(BF16)
    groups = len(POOL_WINDOWS)
    eye = jnp.eye(groups, dtype=F32)
    pool_w_bd = (pool_w[:, :, :, None, :] * eye[None, :, None, :, None]).reshape(
        depth, POOL_WIDTH, POOL_WIDTH).astype(BF16)
    w_out_b, w_ff_in_b, w_ff_out_b = (w.astype(BF16) for w in (w_out, w_ff_in, w_ff_out))
    row = lambda a: a.reshape(depth, 1, -1).astype(F32)
    norm1, norm2, pscale, hnorm = row(norm1_g), row(norm2_g), row(pool_scale), row(hgrn_norm_g)
    final_g = final_norm_g.reshape(1, d_model).astype(F32)
    cos, slo, shi = _rope_patterns(positions)

    x2d = x.reshape(batch * seq_len, d_model)
    for layer in range(depth):
        y_pool, hg, q, k, v, qi, kiw = _in_proj(x2d, norm1, w_in_pad, cos, slo, shi, pool_w_bd,
                                                pscale, layer, seq_len)
        y_hgrn = _hgrn(hg, lower_bounds, hnorm, layer, batch, seq_len)
        y_attn = _dsa(q, k, v, qi, kiw, batch, seq_len)
        x2d = _out_ffn(x2d, y_pool, y_hgrn, y_attn, w_out_b, norm2, w_ff_in_b, w_ff_out_b,
                       final_g, layer, layer == depth - 1, seq_len)
    return x2d.reshape(batch, seq_len, d_model)
```

```python
import functools

import jax
import jax.numpy as jnp
from jax import lax
from jax.experimental import pallas as pl
from jax.experimental.pallas import tpu as pltpu

F32 = jnp.float32
BF16 = jnp.bfloat16

POOL_WINDOWS = (2, 4, 8, 16)
POOL_WIDTH = 256
POOL_GROUP_DIM = 64
HGRN_WIDTH = 256
HGRN_HEAD_DIM = 64
LOG_F_FLOOR = 1e-30
ATTN_WIDTH = 512
HEAD_DIM = 64
N_HEADS = 8
N_KV_HEADS = 2
KV_GROUP = N_HEADS // N_KV_HEADS
KV_WIDTH = N_KV_HEADS * HEAD_DIM
IDX_HEADS = 4
IDX_DIM = 64
TOPK_MAX = 256
MASK_VALUE = -1e30
ROPE_THETA = 500000.0
ROT_DIM = HEAD_DIM // 4
ROT_HALF = ROT_DIM // 2
RMS_EPS = 1e-5
IDX_W_SCALE = (IDX_HEADS ** -0.5) * (IDX_DIM ** -0.5)
ATTN_SCALE = HEAD_DIM ** -0.5
LOG2_E = 1.4426950408889634

LANES = 128
SUBLANES = 8
VMEM_LIMIT_BYTES = 56 * 1024 * 1024

COL_POOL = 0
COL_HGRN = COL_POOL + POOL_WIDTH
COL_Q = COL_HGRN + 4 * HGRN_WIDTH
COL_K = COL_Q + ATTN_WIDTH
COL_V = COL_K + KV_WIDTH
COL_QI = COL_V + KV_WIDTH
COL_KI = COL_QI + IDX_HEADS * IDX_DIM
D_IN = COL_KI + IDX_DIM + IDX_HEADS
D_IN_PAD = COL_KI + LANES
IDX_CAT = 4 * IDX_DIM

PROJ_TILE = 512
HGRN_TILE = 256
HGRN_CHUNK = 16
Q_TILE = 128
KEY_TILE = 256
INT_MIN = -2 ** 31


def _nt_dot(a, b):
    return lax.dot_general(a, b, (((1,), (1,)), ((), ())), preferred_element_type=F32)


def _tn_dot(a, b):
    return lax.dot_general(a, b, (((0,), (0,)), ((), ())), preferred_element_type=F32)


def _dot(a, b):
    return jnp.dot(a, b, preferred_element_type=F32)


def _split_bf16(a):
    hi = a.astype(BF16)
    lo = (a - hi.astype(F32)).astype(BF16)
    return hi, lo


def _rms_norm(x, g):
    return x * lax.rsqrt(jnp.mean(x * x, axis=-1, keepdims=True) + RMS_EPS) * g


def _silu(x):
    return x * jax.nn.sigmoid(x)


def _rope(x, cos, sin_lo, sin_hi):
    w = x.shape[1]
    rep = w // LANES
    if rep > 1:
        cos = jnp.concatenate([cos] * rep, axis=1)
        sin_lo = jnp.concatenate([sin_lo] * rep, axis=1)
        sin_hi = jnp.concatenate([sin_hi] * rep, axis=1)
    from_below = pltpu.roll(x, ROT_HALF, 1)
    from_above = pltpu.roll(x, w - ROT_HALF, 1)
    return x * cos + from_below * sin_hi + from_above * sin_lo


def _in_proj_kernel(x_ref, g_ref, w_ref, cos_ref, slo_ref, shi_ref, pw_ref, pscale_ref,
                    ypool_ref, hg_ref, q_ref, k_ref, v_ref, qi_ref, ki_ref, wi_ref,
                    halo_ref, *, tiles_per_seq):
    tm = x_ref.shape[0]
    halo = halo_ref.shape[0]
    i = pl.program_id(0)
    seq_tile = i % tiles_per_seq

    h = _rms_norm(x_ref[...], g_ref[...])
    proj = _dot(h.astype(BF16), w_ref[...])

    @pl.when(seq_tile == 0)
    def _():
        halo_ref[...] = jnp.zeros_like(halo_ref)

    u = proj[:, COL_POOL:COL_POOL + POOL_WIDTH]
    ext = jnp.concatenate([halo_ref[...], u], axis=0)
    halo_ref[...] = u[tm - halo:, :]
    sums = []
    acc = ext
    span = 1
    for win in POOL_WINDOWS:
        while span < win:
            acc = acc + pltpu.roll(acc, span, 0)
            span *= 2
        sums.append(acc[halo:, :])
    lane = lax.broadcasted_iota(jnp.int32, (tm, POOL_WIDTH), 1)
    group = lane // POOL_GROUP_DIM
    wsum = sums[-1]
    win_lane = jnp.full((tm, POOL_WIDTH), POOL_WINDOWS[-1], jnp.int32)
    for gi in range(len(POOL_WINDOWS) - 2, -1, -1):
        wsum = jnp.where(group == gi, sums[gi], wsum)
        win_lane = jnp.where(group == gi, POOL_WINDOWS[gi], win_lane)
    t_pos = seq_tile * tm + lax.broadcasted_iota(jnp.int32, (tm, POOL_WIDTH), 0)
    count = jnp.minimum(t_pos + 1, win_lane).astype(F32)
    pooled = wsum / count - u
    y_pool = _dot(pooled.astype(BF16), pw_ref[...]) * pscale_ref[...]
    ypool_ref[...] = y_pool.astype(ypool_ref.dtype)

    hg_ref[...] = proj[:, COL_HGRN:COL_HGRN + 4 * HGRN_WIDTH]

    cos, slo, shi = cos_ref[...], slo_ref[...], shi_ref[...]
    q = _rope(proj[:, COL_Q:COL_Q + ATTN_WIDTH], cos, slo, shi) * (ATTN_SCALE * LOG2_E)
    q_ref[...] = q.astype(q_ref.dtype)
    k = _rope(proj[:, COL_K:COL_K + KV_WIDTH], cos, slo, shi)
    k_ref[...] = k.astype(k_ref.dtype)
    v_ref[...] = proj[:, COL_V:COL_V + KV_WIDTH].astype(v_ref.dtype)
    qi = _rope(proj[:, COL_QI:COL_QI + IDX_HEADS * IDX_DIM], cos, slo, shi)
    qi_hi = qi.astype(BF16).astype(F32)
    qi_lo = qi - qi_hi
    zeros = jnp.zeros((tm, IDX_DIM), F32)
    pieces = []
    for hd in range(IDX_HEADS):
        cols = slice(hd * IDX_DIM, (hd + 1) * IDX_DIM)
        pieces += [qi_hi[:, cols], qi_hi[:, cols], qi_lo[:, cols], zeros]
    qi_ref[...] = jnp.concatenate(pieces, axis=1).astype(qi_ref.dtype)
    kiw_raw = proj[:, COL_KI:COL_KI + LANES]
    ki = _rope(kiw_raw, cos, slo, shi)[:, 0:IDX_DIM]
    ki_hi = ki.astype(BF16).astype(F32)
    ki_ref[...] = jnp.concatenate([ki_hi, ki - ki_hi, ki_hi, zeros], axis=1).astype(ki_ref.dtype)
    wi_ref[...] = kiw_raw * IDX_W_SCALE


def _in_proj(x2d, norm_g, w_in_pad, cos, slo, shi, pool_w_bd, pool_scale, layer, seq_len):
    n_tok, d_model = x2d.shape
    tm = min(PROJ_TILE, seq_len)
    assert seq_len % tm == 0 and n_tok % tm == 0
    halo = 2 * SUBLANES
    assert halo >= POOL_WINDOWS[-1] and tm >= halo
    tok = lambda w: pl.BlockSpec((tm, w), lambda i: (i, 0))
    lay = lambda *s: pl.BlockSpec((None,) + s, lambda i: (layer,) + (0,) * len(s))
    out_shape = (
        jax.ShapeDtypeStruct((n_tok, POOL_WIDTH), BF16),
        jax.ShapeDtypeStruct((n_tok, 4 * HGRN_WIDTH), F32),
        jax.ShapeDtypeStruct((n_tok, ATTN_WIDTH), BF16),
        jax.ShapeDtypeStruct((n_tok, KV_WIDTH), BF16),
        jax.ShapeDtypeStruct((n_tok, KV_WIDTH), BF16),
        jax.ShapeDtypeStruct((n_tok, IDX_HEADS * IDX_CAT), BF16),
        jax.ShapeDtypeStruct((n_tok, IDX_CAT), BF16),
        jax.ShapeDtypeStruct((n_tok, LANES), F32),
    )
    return pl.pallas_call(
        functools.partial(_in_proj_kernel, tiles_per_seq=seq_len // tm),
        grid=(n_tok // tm,),
        in_specs=[tok(d_model), lay(1, d_model), lay(d_model, D_IN_PAD),
                  tok(LANES), tok(LANES), tok(LANES),
                  lay(POOL_WIDTH, POOL_WIDTH), lay(1, POOL_WIDTH)],
        out_specs=tuple(tok(s.shape[1]) for s in out_shape),
        out_shape=out_shape,
        scratch_shapes=[pltpu.VMEM((halo, POOL_WIDTH), F32)],
        compiler_params=pltpu.CompilerParams(dimension_semantics=("arbitrary",),
                                             vmem_limit_bytes=VMEM_LIMIT_BYTES),
        name="in_proj",
    )(x2d, norm_g, w_in_pad, cos, slo, shi, pool_w_bd, pool_scale)


def _hgrn_kernel(hg_ref, lb_ref, ng_ref, out_ref, state_ref, o_ref):
    tb, width = out_ref.shape
    c = HGRN_CHUNK

    @pl.when(pl.program_id(1) == 0)
    def _():
        state_ref[...] = jnp.zeros_like(state_ref)

    lb = lb_ref[...]
    q = _silu(hg_ref[:, 0:width])
    sig = jax.nn.sigmoid(hg_ref[:, width:2 * width])
    f = lb + (1.0 - lb) * sig
    log_f = jnp.log(jnp.maximum(f, LOG_F_FLOOR))
    k = (1.0 - lb) * (1.0 - sig)
    v = hg_ref[:, 2 * width:3 * width]

    row_in_chunk = lax.broadcasted_iota(jnp.int32, (tb, width), 0) % c
    b = log_f
    step = 1
    while step < c:
        b = b + jnp.where(row_in_chunk >= step, pltpu.roll(b, step, 0), 0.0)
        step *= 2

    lane_r = lax.broadcasted_iota(jnp.int32, (width, width), 0) // HGRN_HEAD_DIM
    lane_c = lax.broadcasted_iota(jnp.int32, (width, width), 1) // HGRN_HEAD_DIM
    same_head = lane_r == lane_c
    head_sum = jnp.where(same_head, 1.0, 0.0).astype(BF16)

    o = None
    for d in range(c):
        if d == 0:
            p, vd = q * k, v
        else:
            kd, bd, vd = (pltpu.roll(a, d, 0) for a in (k, b, v))
            decay = jnp.exp(jnp.minimum(b - bd, 0.0))
            p = jnp.where(row_in_chunk >= d, q * kd * decay, 0.0)
        score = _dot(p.astype(BF16), head_sum)
        o = score * vd if o is None else o + score * vd
    o_ref[...] = o

    for ci in range(tb // c):
        rows = slice(ci * c, (ci + 1) * c)
        bc = b[rows]
        b_last = bc[c - 1:c]
        state = state_ref[...]
        q_dec = q[rows] * jnp.exp(bc)
        o_ref[rows, :] += _nt_dot(q_dec.astype(BF16), state.astype(BF16))
        k_dec = k[rows] * jnp.exp(b_last - bc)
        upd = _tn_dot(v[rows].astype(BF16), k_dec.astype(BF16))
        state_ref[...] = state * jnp.exp(b_last) + jnp.where(same_head, upd, 0.0)

    o = o_ref[...]
    sq_hi, sq_lo = _split_bf16(o * o)
    mean_sq = (_dot(sq_hi, head_sum) + _dot(sq_lo, head_sum)) * (1.0 / HGRN_HEAD_DIM)
    gate = _silu(hg_ref[:, 3 * width:4 * width])
    out_ref[...] = (o * lax.rsqrt(mean_sq + RMS_EPS) * ng_ref[...] * gate).astype(out_ref.dtype)


def _hgrn(hg, lower_bound, norm_g, layer, batch, seq_len):
    n_tok = hg.shape[0]
    tb = min(HGRN_TILE, seq_len)
    assert seq_len % tb == 0 and tb % HGRN_CHUNK == 0 and HGRN_CHUNK % SUBLANES == 0
    steps = seq_len // tb
    lay = pl.BlockSpec((None, 1, HGRN_WIDTH), lambda bi, ti: (layer, 0, 0))
    return pl.pallas_call(
        _hgrn_kernel,
        grid=(batch, steps),
        in_specs=[pl.BlockSpec((tb, 4 * HGRN_WIDTH), lambda bi, ti: (bi * steps + ti, 0)),
                  lay, lay],
        out_specs=pl.BlockSpec((tb, HGRN_WIDTH), lambda bi, ti: (bi * steps + ti, 0)),
        out_shape=jax.ShapeDtypeStruct((n_tok, HGRN_WIDTH), BF16),
        scratch_shapes=[pltpu.VMEM((HGRN_WIDTH, HGRN_WIDTH), F32),
                        pltpu.VMEM((tb, HGRN_WIDTH), F32)],
        compiler_params=pltpu.CompilerParams(dimension_semantics=("arbitrary", "arbitrary"),
                                             vmem_limit_bytes=VMEM_LIMIT_BYTES),
        name="hgrn2",
    )(hg, lower_bound, norm_g)


def _dsa_kernel(q_ref, k_ref, v_ref, qi_ref, ki_ref, wi_ref, out_ref,
                key_ref, hi_ref, lo_ref, thr_ref, need_ref, m_ref, acc_ref, *, top_k):
    qb = q_ref.shape[0]
    kt = key_ref.shape[1]
    i16 = jnp.int16
    rows16 = 2 * SUBLANES
    span = 4 * rows16
    tall = lambda a: jnp.concatenate([a] * (span // rows16), axis=0)
    t0 = pl.program_id(1) * qb
    n_tiles = (t0 + qb + kt - 1) // kt

    key_pos = lax.broadcasted_iota(jnp.int32, (kt, qb), 0)
    query_pos = t0 + lax.broadcasted_iota(jnp.int32, (kt, qb), 1)

    wi_t = wi_ref[...].T
    w_heads = [wi_t[IDX_DIM + h:IDX_DIM + h + 1, :] for h in range(IDX_HEADS)]

    def index_tile(ti, carry):
        base = pl.multiple_of(ti * kt, kt)
        ki_tile = ki_ref[pl.ds(base, kt), :]
        score = jnp.zeros((kt, qb), F32)
        for h in range(IDX_HEADS):
            s = _nt_dot(ki_tile, qi_ref[:, h * IDX_CAT:(h + 1) * IDX_CAT])
            score = score + jnp.maximum(s, 0.0) * w_heads[h]
        score = jnp.where(score == 0.0, 0.0, score)
        score = jnp.where(base + key_pos <= query_pos, score, MASK_VALUE)
        bits = lax.bitcast_convert_type(score, jnp.int32)
        key = bits ^ ((bits >> 31) & 0x7FFFFFFF)
        key_ref[ti] = key
        hi_ref[ti] = (key >> 16).astype(i16)
        lo_ref[ti] = ((key & 0xFFFF) - 2 ** 15).astype(i16)
        return carry

    lax.fori_loop(0, n_tiles, index_tile, 0)

    def count(ref, pred):
        def body(ti, acc):
            keys = ref[ti]
            for r in range(kt // span):
                acc = acc + jnp.where(pred(keys[r * span:(r + 1) * span, :]), i16(1), i16(0))
            return acc
        partial = lax.fori_loop(0, n_tiles, body, jnp.zeros((span, qb), i16))
        return jnp.sum(partial.astype(F32), axis=0, keepdims=True)

    def kth_largest(ref, target):
        def bit(it, thr):
            cand = thr + lax.shift_left(jnp.int32(1), 15 - it)
            cand16 = tall(cand.astype(i16))
            n_ge = count(ref, lambda keys: keys >= cand16)
            return jnp.where(n_ge >= target, cand, thr)
        return lax.fori_loop(0, 16, bit, jnp.full((rows16, qb), -2 ** 15, jnp.int32))

    thr_ref[...] = jnp.full_like(thr_ref, INT_MIN)
    need_ref[...] = jnp.zeros_like(need_ref)

    @pl.when(t0 + qb > top_k)
    def _():
        upper = kth_largest(hi_ref, float(top_k))
        upper16 = tall(upper.astype(i16))
        n_upper_gt = count(hi_ref, lambda keys: keys > upper16)
        upper_kt = jnp.concatenate([upper16] * (kt // span), axis=0)

        def keep_bucket(ti, carry):
            lo_ref[ti] = jnp.where(hi_ref[ti] == upper_kt, lo_ref[ti], i16(-2 ** 15))
            return carry

        lax.fori_loop(0, n_tiles, keep_bucket, 0)
        lower = kth_largest(lo_ref, top_k - n_upper_gt)
        lower16 = tall(lower.astype(i16))
        n_gt = n_upper_gt + count(lo_ref, lambda keys: keys > lower16)
        thr_ref[...] = (lax.shift_left(upper, 16) + (lower + 2 ** 15))[0:SUBLANES]
        need_ref[...] = jnp.broadcast_to(top_k - n_gt, need_ref.shape)

    thr = thr_ref[0:1, :]
    need = need_ref[0:1, :]

    m_ref[...] = jnp.full_like(m_ref, MASK_VALUE)
    acc_ref[...] = jnp.zeros_like(acc_ref)
    earlier = (lax.broadcasted_iota(jnp.int32, (kt, kt), 1)
               < lax.broadcasted_iota(jnp.int32, (kt, kt), 0))
    earlier_bf = jnp.where(earlier, 1.0, 0.0).astype(BF16)
    ones_rows = jnp.ones((SUBLANES, kt), BF16)
    dim_v = lax.broadcasted_iota(jnp.int32, (KV_WIDTH, kt), 0)
    q_stack = [jnp.concatenate([q_ref[:, h * HEAD_DIM:(h + 1) * HEAD_DIM]
                                for h in range(g * KV_GROUP, (g + 1) * KV_GROUP)], axis=0)
               for g in range(N_KV_HEADS)]

    def attend_tile(ti, ties_before):
        base = pl.multiple_of(ti * kt, kt)
        keys = key_ref[ti]
        tie = keys == thr
        tie_bf = jnp.where(tie, 1.0, 0.0).astype(BF16)
        rank = _dot(earlier_bf, tie_bf) + ties_before
        chosen = ((keys > thr) | (tie & (rank < need))) & (base + key_pos <= query_pos)
        k_tile = k_ref[pl.ds(base, kt), :]
        v_t = v_ref[pl.ds(base, kt), :].astype(F32).T
        for g in range(N_KV_HEADS):
            v_aug = jnp.where(dim_v // HEAD_DIM == g, v_t, 1.0).astype(BF16)
            s_all = _nt_dot(k_tile[:, g * HEAD_DIM:(g + 1) * HEAD_DIM], q_stack[g])
            for hh in range(KV_GROUP):
                h = g * KV_GROUP + hh
                s = jnp.where(chosen, s_all[:, hh * qb:(hh + 1) * qb], MASK_VALUE)
                m_old = m_ref[h]
                m_new = jnp.maximum(m_old, jnp.max(s, axis=0, keepdims=True))
                p = jnp.exp2(s - m_new)
                acc_ref[h] = jnp.exp2(m_old - m_new) * acc_ref[h] + _dot(v_aug, p.astype(BF16))
                m_ref[h] = m_new
        return ties_before + _dot(ones_rows, tie_bf)[0:1]

    lax.fori_loop(0, n_tiles, attend_tile, jnp.zeros((1, qb), F32))
    for pair in range(N_HEADS // 2):
        outs = []
        for h in (2 * pair, 2 * pair + 1):
            g = h // KV_GROUP
            acc = acc_ref[h]
            other = (1 - g) * HEAD_DIM
            outs.append(acc[g * HEAD_DIM:(g + 1) * HEAD_DIM, :] / acc[other:other + 1, :])
        out_ref[:, pair * LANES:(pair + 1) * LANES] = (
            jnp.concatenate(outs, axis=0).T.astype(out_ref.dtype))


def _dsa(q, k, v, qi, ki, wi, batch, seq_len):
    n_tok = q.shape[0]
    qb = min(Q_TILE, seq_len)
    kt = min(KEY_TILE, seq_len)
    assert seq_len % qb == 0 and seq_len % kt == 0 and kt % qb == 0 and qb % LANES == 0
    assert N_KV_HEADS == 2 and KV_WIDTH == LANES and 2 * HEAD_DIM == LANES
    steps = seq_len // qb
    n_kt = seq_len // kt
    top_k = min(TOPK_MAX, seq_len // 4)
    qtile = lambda w: pl.BlockSpec((qb, w), lambda bi, qi_: (bi * steps + qi_, 0))
    seq = lambda w: pl.BlockSpec((seq_len, w), lambda bi, qi_: (bi, 0))
    return pl.pallas_call(
        functools.partial(_dsa_kernel, top_k=top_k),
        grid=(batch, steps),
        in_specs=[qtile(ATTN_WIDTH), seq(KV_WIDTH), seq(KV_WIDTH),
                  qtile(IDX_HEADS * IDX_CAT), seq(IDX_CAT), qtile(LANES)],
        out_specs=qtile(ATTN_WIDTH),
        out_shape=jax.ShapeDtypeStruct((n_tok, ATTN_WIDTH), BF16),
        scratch_shapes=[pltpu.VMEM((n_kt, kt, qb), jnp.int32),
                        pltpu.VMEM((n_kt, kt, qb), jnp.int16),
                        pltpu.VMEM((n_kt, kt, qb), jnp.int16),
                        pltpu.VMEM((SUBLANES, qb), jnp.int32),
                        pltpu.VMEM((SUBLANES, qb), F32),
                        pltpu.VMEM((N_HEADS, 1, qb), F32),
                        pltpu.VMEM((N_HEADS, KV_WIDTH, qb), F32)],
        compiler_params=pltpu.CompilerParams(dimension_semantics=("arbitrary", "arbitrary"),
                                             vmem_limit_bytes=VMEM_LIMIT_BYTES),
        name="dsa",
    )(q, k, v, qi, ki, wi)


def _out_ffn_kernel(x_ref, yp_ref, yh_ref, ya_ref, wo_hbm, g2_ref, wfi_hbm, wfo_hbm, gf_ref,
                    out_ref, wo_ref, wfi_ref, wfo_ref, *, layer, final_norm):
    @pl.when(pl.program_id(0) == 0)
    def _():
        pltpu.sync_copy(wo_hbm.at[layer], wo_ref)
        pltpu.sync_copy(wfi_hbm.at[layer], wfi_ref)
        pltpu.sync_copy(wfo_hbm.at[layer], wfo_ref)

    x = x_ref[...]
    x = x + _dot(yp_ref[...], wo_ref[0:POOL_WIDTH, :])
    x = x + _dot(yh_ref[...], wo_ref[POOL_WIDTH:POOL_WIDTH + HGRN_WIDTH, :])
    x = x + _dot(ya_ref[...], wo_ref[POOL_WIDTH + HGRN_WIDTH:, :])
    h = _rms_norm(x, g2_ref[...]).astype(BF16)
    a = jnp.maximum(_dot(h, wfi_ref[...]), 0.0)
    x = x + _dot((a * a).astype(BF16), wfo_ref[...])
    if final_norm:
        x = _rms_norm(x, gf_ref[...])
    out_ref[...] = x


def _out_ffn(x2d, y_pool, y_hgrn, y_attn, w_out, norm2_g, w_ff_in, w_ff_out, final_g,
             layer, final_norm, seq_len):
    n_tok, d_model = x2d.shape
    d_ff = w_ff_in.shape[2]
    tm = min(PROJ_TILE, seq_len)
    tok = lambda w: pl.BlockSpec((tm, w), lambda i: (i, 0))
    hbm = pl.BlockSpec(memory_space=pl.ANY)
    return pl.pallas_call(
        functools.partial(_out_ffn_kernel, layer=layer, final_norm=final_norm),
        grid=(n_tok // tm,),
        in_specs=[tok(d_model), tok(POOL_WIDTH), tok(HGRN_WIDTH), tok(ATTN_WIDTH),
                  hbm, pl.BlockSpec((None, 1, d_model), lambda i: (layer, 0, 0)), hbm, hbm,
                  pl.BlockSpec((1, d_model), lambda i: (0, 0))],
        out_specs=tok(d_model),
        out_shape=jax.ShapeDtypeStruct((n_tok, d_model), F32),
        scratch_shapes=[pltpu.VMEM((d_model, d_model), BF16),
                        pltpu.VMEM((d_model, d_ff), BF16),
                        pltpu.VMEM((d_ff, d_model), BF16)],
        compiler_params=pltpu.CompilerParams(dimension_semantics=("arbitrary",),
                                             vmem_limit_bytes=VMEM_LIMIT_BYTES),
        name="out_ffn",
    )(x2d, y_pool, y_hgrn, y_attn, w_out, norm2_g, w_ff_in, w_ff_out, final_g)


def _rope_patterns(positions):
    inv_freq = ROPE_THETA ** (-jnp.arange(0, ROT_DIM, 2, dtype=F32) / ROT_DIM)
    ang = positions.astype(F32).reshape(-1, 1) * inv_freq
    cos, sin = jnp.cos(ang), jnp.sin(ang)
    n = ang.shape[0]
    rest = HEAD_DIM - ROT_DIM
    cos_h = jnp.concatenate([cos, cos, jnp.ones((n, rest), F32)], axis=1)
    slo_h = jnp.concatenate([-sin, jnp.zeros((n, ROT_HALF + rest), F32)], axis=1)
    shi_h = jnp.concatenate([jnp.zeros((n, ROT_HALF), F32), sin, jnp.zeros((n, rest), F32)], axis=1)
    two = lambda a: jnp.concatenate([a, a], axis=1)
    return two(cos_h), two(slo_h), two(shi_h)


def kernel(x, positions, norm1_g, w_in, pool_w, pool_scale, lb_logits, hgrn_norm_g, w_out,
           norm2_g, w_ff_in, w_ff_out, final_norm_g):
    batch, seq_len, d_model = x.shape
    depth = w_in.shape[0]
    assert w_in.shape[2] == D_IN

    p_lb = jax.nn.softmax(lb_logits.astype(F32), axis=0)
    lower_bounds = (jnp.cumsum(p_lb, axis=0) - p_lb[0])[:, None, :]
    w_in_pad = jnp.pad(w_in, ((0, 0), (0, 0), (0, D_IN_PAD - D_IN))).astype(BF16)
    groups = len(POOL_WINDOWS)
    eye = jnp.eye(groups, dtype=F32)
    pool_w_bd = (pool_w[:, :, :, None, :] * eye[None, :, None, :, None]).reshape(
        depth, POOL_WIDTH, POOL_WIDTH).astype(BF16)
    w_out_b, w_ff_in_b, w_ff_out_b = (w.astype(BF16) for w in (w_out, w_ff_in, w_ff_out))
    row = lambda a: a.reshape(depth, 1, -1).astype(F32)
    norm1, norm2, pscale, hnorm = row(norm1_g), row(norm2_g), row(pool_scale), row(hgrn_norm_g)
    final_g = final_norm_g.reshape(1, d_model).astype(F32)
    cos, slo, shi = _rope_patterns(positions)

    x2d = x.reshape(batch * seq_len, d_model)
    for layer in range(depth):
        y_pool, hg, q, k, v, qi, ki, wi = _in_proj(x2d, norm1, w_in_pad, cos, slo, shi, pool_w_bd,
                                                pscale, layer, seq_len)
        y_hgrn = _hgrn(hg, lower_bounds, hnorm, layer, batch, seq_len)
        y_attn = _dsa(q, k, v, qi, ki, wi, batch, seq_len)
        x2d = _out_ffn(x2d, y_pool, y_hgrn, y_attn, w_out_b, norm2, w_ff_in_b, w_ff_out_b,
                       final_g, layer, layer == depth - 1, seq_len)
    return x2d.reshape(batch, seq_len, d_model)
```

```python
import functools

import jax
import jax.numpy as jnp
from jax import lax
from jax.experimental import pallas as pl
from jax.experimental.pallas import tpu as pltpu

F32 = jnp.float32
BF16 = jnp.bfloat16

POOL_WINDOWS = (2, 4, 8, 16)
POOL_WIDTH = 256
POOL_GROUP_DIM = 64
HGRN_WIDTH = 256
HGRN_HEAD_DIM = 64
LOG_F_FLOOR = 1e-30
ATTN_WIDTH = 512
HEAD_DIM = 64
N_HEADS = 8
N_KV_HEADS = 2
KV_GROUP = N_HEADS // N_KV_HEADS
KV_WIDTH = N_KV_HEADS * HEAD_DIM
IDX_HEADS = 4
IDX_DIM = 64
TOPK_MAX = 256
MASK_VALUE = -1e30
ROPE_THETA = 500000.0
ROT_DIM = HEAD_DIM // 4
ROT_HALF = ROT_DIM // 2
RMS_EPS = 1e-5
IDX_W_SCALE = (IDX_HEADS ** -0.5) * (IDX_DIM ** -0.5)
ATTN_SCALE = HEAD_DIM ** -0.5
LOG2_E = 1.4426950408889634

LANES = 128
SUBLANES = 8
VMEM_LIMIT_BYTES = 56 * 1024 * 1024

COL_POOL = 0
COL_HGRN = COL_POOL + POOL_WIDTH
COL_Q = COL_HGRN + 4 * HGRN_WIDTH
COL_K = COL_Q + ATTN_WIDTH
COL_V = COL_K + KV_WIDTH
COL_QI = COL_V + KV_WIDTH
COL_KI = COL_QI + IDX_HEADS * IDX_DIM
D_IN = COL_KI + IDX_DIM + IDX_HEADS
D_IN_PAD = COL_KI + LANES
IDX_CAT = 4 * IDX_DIM

PROJ_TILE = 512
HGRN_TILE = 256
HGRN_CHUNK = 16
Q_TILE = 256
KEY_TILE = 256
INT_MIN = -2 ** 31


def _nt_dot(a, b):
    return lax.dot_general(a, b, (((1,), (1,)), ((), ())), preferred_element_type=F32)


def _tn_dot(a, b):
    return lax.dot_general(a, b, (((0,), (0,)), ((), ())), preferred_element_type=F32)


def _dot(a, b):
    return jnp.dot(a, b, preferred_element_type=F32)


def _split_bf16(a):
    hi = a.astype(BF16)
    lo = (a - hi.astype(F32)).astype(BF16)
    return hi, lo


def _rms_norm(x, g):
    return x * lax.rsqrt(jnp.mean(x * x, axis=-1, keepdims=True) + RMS_EPS) * g


def _silu(x):
    return x * jax.nn.sigmoid(x)


def _rope(x, cos, sin_lo, sin_hi):
    w = x.shape[1]
    rep = w // LANES
    if rep > 1:
        cos = jnp.concatenate([cos] * rep, axis=1)
        sin_lo = jnp.concatenate([sin_lo] * rep, axis=1)
        sin_hi = jnp.concatenate([sin_hi] * rep, axis=1)
    from_below = pltpu.roll(x, ROT_HALF, 1)
    from_above = pltpu.roll(x, w - ROT_HALF, 1)
    return x * cos + from_below * sin_hi + from_above * sin_lo


def _in_proj_kernel(x_ref, g_ref, w_ref, cos_ref, slo_ref, shi_ref, pw_ref, pscale_ref,
                    ypool_ref, hg_ref, q_ref, k_ref, v_ref, qi_ref, ki_ref, wi_ref,
                    halo_ref, *, tiles_per_seq):
    tm = x_ref.shape[0]
    halo = halo_ref.shape[0]
    i = pl.program_id(0)
    seq_tile = i % tiles_per_seq

    h = _rms_norm(x_ref[...], g_ref[...])
    proj = _dot(h.astype(BF16), w_ref[...])

    @pl.when(seq_tile == 0)
    def _():
        halo_ref[...] = jnp.zeros_like(halo_ref)

    u = proj[:, COL_POOL:COL_POOL + POOL_WIDTH]
    ext = jnp.concatenate([halo_ref[...], u], axis=0)
    halo_ref[...] = u[tm - halo:, :]
    sums = []
    acc = ext
    span = 1
    for win in POOL_WINDOWS:
        while span < win:
            acc = acc + pltpu.roll(acc, span, 0)
            span *= 2
        sums.append(acc[halo:, :])
    lane = lax.broadcasted_iota(jnp.int32, (tm, POOL_WIDTH), 1)
    group = lane // POOL_GROUP_DIM
    wsum = sums[-1]
    win_lane = jnp.full((tm, POOL_WIDTH), POOL_WINDOWS[-1], jnp.int32)
    for gi in range(len(POOL_WINDOWS) - 2, -1, -1):
        wsum = jnp.where(group == gi, sums[gi], wsum)
        win_lane = jnp.where(group == gi, POOL_WINDOWS[gi], win_lane)
    t_pos = seq_tile * tm + lax.broadcasted_iota(jnp.int32, (tm, POOL_WIDTH), 0)
    count = jnp.minimum(t_pos + 1, win_lane).astype(F32)
    pooled = wsum / count - u
    y_pool = _dot(pooled.astype(BF16), pw_ref[...]) * pscale_ref[...]
    ypool_ref[...] = y_pool.astype(ypool_ref.dtype)

    hg_ref[...] = proj[:, COL_HGRN:COL_HGRN + 4 * HGRN_WIDTH]

    cos, slo, shi = cos_ref[...], slo_ref[...], shi_ref[...]
    q = _rope(proj[:, COL_Q:COL_Q + ATTN_WIDTH], cos, slo, shi) * (ATTN_SCALE * LOG2_E)
    q_ref[...] = q.astype(q_ref.dtype)
    k = _rope(proj[:, COL_K:COL_K + KV_WIDTH], cos, slo, shi)
    k_ref[...] = k.astype(k_ref.dtype)
    v_ref[...] = proj[:, COL_V:COL_V + KV_WIDTH].astype(v_ref.dtype)
    qi = _rope(proj[:, COL_QI:COL_QI + IDX_HEADS * IDX_DIM], cos, slo, shi)
    qi_hi = qi.astype(BF16).astype(F32)
    qi_lo = qi - qi_hi
    zeros = jnp.zeros((tm, IDX_DIM), F32)
    pieces = []
    for hd in range(IDX_HEADS):
        cols = slice(hd * IDX_DIM, (hd + 1) * IDX_DIM)
        pieces += [qi_hi[:, cols], qi_hi[:, cols], qi_lo[:, cols], zeros]
    qi_ref[...] = jnp.concatenate(pieces, axis=1).astype(qi_ref.dtype)
    kiw_raw = proj[:, COL_KI:COL_KI + LANES]
    ki = _rope(kiw_raw, cos, slo, shi)[:, 0:IDX_DIM]
    ki_hi = ki.astype(BF16).astype(F32)
    ki_ref[...] = jnp.concatenate([ki_hi, ki - ki_hi, ki_hi, zeros], axis=1).astype(ki_ref.dtype)
    wi_ref[...] = kiw_raw * IDX_W_SCALE


def _in_proj(x2d, norm_g, w_in_pad, cos, slo, shi, pool_w_bd, pool_scale, layer, seq_len):
    n_tok, d_model = x2d.shape
    tm = min(PROJ_TILE, seq_len)
    assert seq_len % tm == 0 and n_tok % tm == 0
    halo = 2 * SUBLANES
    assert halo >= POOL_WINDOWS[-1] and tm >= halo
    tok = lambda w: pl.BlockSpec((tm, w), lambda i: (i, 0))
    lay = lambda *s: pl.BlockSpec((None,) + s, lambda i: (layer,) + (0,) * len(s))
    out_shape = (
        jax.ShapeDtypeStruct((n_tok, POOL_WIDTH), BF16),
        jax.ShapeDtypeStruct((n_tok, 4 * HGRN_WIDTH), F32),
        jax.ShapeDtypeStruct((n_tok, ATTN_WIDTH), BF16),
        jax.ShapeDtypeStruct((n_tok, KV_WIDTH), BF16),
        jax.ShapeDtypeStruct((n_tok, KV_WIDTH), BF16),
        jax.ShapeDtypeStruct((n_tok, IDX_HEADS * IDX_CAT), BF16),
        jax.ShapeDtypeStruct((n_tok, IDX_CAT), BF16),
        jax.ShapeDtypeStruct((n_tok, LANES), F32),
    )
    return pl.pallas_call(
        functools.partial(_in_proj_kernel, tiles_per_seq=seq_len // tm),
        grid=(n_tok // tm,),
        in_specs=[tok(d_model), lay(1, d_model), lay(d_model, D_IN_PAD),
                  tok(LANES), tok(LANES), tok(LANES),
                  lay(POOL_WIDTH, POOL_WIDTH), lay(1, POOL_WIDTH)],
        out_specs=tuple(tok(s.shape[1]) for s in out_shape),
        out_shape=out_shape,
        scratch_shapes=[pltpu.VMEM((halo, POOL_WIDTH), F32)],
        compiler_params=pltpu.CompilerParams(dimension_semantics=("arbitrary",),
                                             vmem_limit_bytes=VMEM_LIMIT_BYTES),
        name="in_proj",
    )(x2d, norm_g, w_in_pad, cos, slo, shi, pool_w_bd, pool_scale)


def _hgrn_kernel(hg_ref, lb_ref, ng_ref, out_ref, state_ref, o_ref):
    tb, width = out_ref.shape
    c = HGRN_CHUNK

    @pl.when(pl.program_id(1) == 0)
    def _():
        state_ref[...] = jnp.zeros_like(state_ref)

    lb = lb_ref[...]
    q = _silu(hg_ref[:, 0:width])
    sig = jax.nn.sigmoid(hg_ref[:, width:2 * width])
    f = lb + (1.0 - lb) * sig
    log_f = jnp.log(jnp.maximum(f, LOG_F_FLOOR))
    k = (1.0 - lb) * (1.0 - sig)
    v = hg_ref[:, 2 * width:3 * width]

    row_in_chunk = lax.broadcasted_iota(jnp.int32, (tb, width), 0) % c
    b = log_f
    step = 1
    while step < c:
        b = b + jnp.where(row_in_chunk >= step, pltpu.roll(b, step, 0), 0.0)
        step *= 2

    lane_r = lax.broadcasted_iota(jnp.int32, (width, width), 0) // HGRN_HEAD_DIM
    lane_c = lax.broadcasted_iota(jnp.int32, (width, width), 1) // HGRN_HEAD_DIM
    same_head = lane_r == lane_c
    head_sum = jnp.where(same_head, 1.0, 0.0).astype(BF16)

    o = None
    for d in range(c):
        if d == 0:
            p, vd = q * k, v
        else:
            kd, bd, vd = (pltpu.roll(a, d, 0) for a in (k, b, v))
            decay = jnp.exp(jnp.minimum(b - bd, 0.0))
            p = jnp.where(row_in_chunk >= d, q * kd * decay, 0.0)
        score = _dot(p.astype(BF16), head_sum)
        o = score * vd if o is None else o + score * vd
    o_ref[...] = o

    for ci in range(tb // c):
        rows = slice(ci * c, (ci + 1) * c)
        bc = b[rows]
        b_last = bc[c - 1:c]
        state = state_ref[...]
        q_dec = q[rows] * jnp.exp(bc)
        o_ref[rows, :] += _nt_dot(q_dec.astype(BF16), state.astype(BF16))
        k_dec = k[rows] * jnp.exp(b_last - bc)
        upd = _tn_dot(v[rows].astype(BF16), k_dec.astype(BF16))
        state_ref[...] = state * jnp.exp(b_last) + jnp.where(same_head, upd, 0.0)

    o = o_ref[...]
    sq_hi, sq_lo = _split_bf16(o * o)
    mean_sq = (_dot(sq_hi, head_sum) + _dot(sq_lo, head_sum)) * (1.0 / HGRN_HEAD_DIM)
    gate = _silu(hg_ref[:, 3 * width:4 * width])
    out_ref[...] = (o * lax.rsqrt(mean_sq + RMS_EPS) * ng_ref[...] * gate).astype(out_ref.dtype)


def _hgrn(hg, lower_bound, norm_g, layer, batch, seq_len):
    n_tok = hg.shape[0]
    tb = min(HGRN_TILE, seq_len)
    assert seq_len % tb == 0 and tb % HGRN_CHUNK == 0 and HGRN_CHUNK % SUBLANES == 0
    steps = seq_len // tb
    lay = pl.BlockSpec((None, 1, HGRN_WIDTH), lambda bi, ti: (layer, 0, 0))
    return pl.pallas_call(
        _hgrn_kernel,
        grid=(batch, steps),
        in_specs=[pl.BlockSpec((tb, 4 * HGRN_WIDTH), lambda bi, ti: (bi * steps + ti, 0)),
                  lay, lay],
        out_specs=pl.BlockSpec((tb, HGRN_WIDTH), lambda bi, ti: (bi * steps + ti, 0)),
        out_shape=jax.ShapeDtypeStruct((n_tok, HGRN_WIDTH), BF16),
        scratch_shapes=[pltpu.VMEM((HGRN_WIDTH, HGRN_WIDTH), F32),
                        pltpu.VMEM((tb, HGRN_WIDTH), F32)],
        compiler_params=pltpu.CompilerParams(dimension_semantics=("arbitrary", "arbitrary"),
                                             vmem_limit_bytes=VMEM_LIMIT_BYTES),
        name="hgrn2",
    )(hg, lower_bound, norm_g)


def _dsa_kernel(q_ref, k_ref, v_ref, qi_ref, ki_ref, wi_ref, out_ref,
                key_ref, hi_ref, lo_ref, thr_ref, need_ref, m_ref, acc_ref, *, top_k):
    qb = q_ref.shape[0]
    kt = key_ref.shape[1]
    i16 = jnp.int16
    rows16 = 2 * SUBLANES
    span = 4 * rows16
    tall = lambda a: jnp.concatenate([a] * (span // rows16), axis=0)
    t0 = pl.program_id(1) * qb
    n_tiles = (t0 + qb + kt - 1) // kt

    key_pos = lax.broadcasted_iota(jnp.int32, (kt, qb), 0)
    query_pos = t0 + lax.broadcasted_iota(jnp.int32, (kt, qb), 1)

    wi_t = wi_ref[...].T
    w_heads = [wi_t[IDX_DIM + h:IDX_DIM + h + 1, :] for h in range(IDX_HEADS)]

    def index_tile(ti, carry):
        base = pl.multiple_of(ti * kt, kt)
        ki_tile = ki_ref[pl.ds(base, kt), :]
        score = jnp.zeros((kt, qb), F32)
        for h in range(IDX_HEADS):
            s = _nt_dot(ki_tile, qi_ref[:, h * IDX_CAT:(h + 1) * IDX_CAT])
            score = score + jnp.maximum(s, 0.0) * w_heads[h]
        score = jnp.where(score == 0.0, 0.0, score)
        score = jnp.where(base + key_pos <= query_pos, score, MASK_VALUE)
        bits = lax.bitcast_convert_type(score, jnp.int32)
        key = bits ^ ((bits >> 31) & 0x7FFFFFFF)
        key_ref[ti] = key
        hi_ref[ti] = (key >> 16).astype(i16)
        lo_ref[ti] = ((key & 0xFFFF) - 2 ** 15).astype(i16)
        return carry

    lax.fori_loop(0, n_tiles, index_tile, 0)

    def count(ref, pred):
        def body(ti, acc):
            keys = ref[ti]
            for r in range(kt // span):
                acc = acc + jnp.where(pred(keys[r * span:(r + 1) * span, :]), i16(1), i16(0))
            return acc
        partial = lax.fori_loop(0, n_tiles, body, jnp.zeros((span, qb), i16))
        return jnp.sum(partial.astype(F32), axis=0, keepdims=True)

    def kth_largest(ref, target):
        def bit(it, thr):
            cand = thr + lax.shift_left(jnp.int32(1), 15 - it)
            cand16 = tall(cand.astype(i16))
            n_ge = count(ref, lambda keys: keys >= cand16)
            return jnp.where(n_ge >= target, cand, thr)
        return lax.fori_loop(0, 16, bit, jnp.full((rows16, qb), -2 ** 15, jnp.int32))

    thr_ref[...] = jnp.full_like(thr_ref, INT_MIN)
    need_ref[...] = jnp.zeros_like(need_ref)

    @pl.when(t0 + qb > top_k)
    def _():
        upper = kth_largest(hi_ref, float(top_k))
        upper16 = tall(upper.astype(i16))
        n_upper_gt = count(hi_ref, lambda keys: keys > upper16)
        upper_kt = jnp.concatenate([upper16] * (kt // span), axis=0)

        def keep_bucket(ti, carry):
            lo_ref[ti] = jnp.where(hi_ref[ti] == upper_kt, lo_ref[ti], i16(-2 ** 15))
            return carry

        lax.fori_loop(0, n_tiles, keep_bucket, 0)
        lower = kth_largest(lo_ref, top_k - n_upper_gt)
        lower16 = tall(lower.astype(i16))
        n_gt = n_upper_gt + count(lo_ref, lambda keys: keys > lower16)
        thr_ref[...] = (lax.shift_left(upper, 16) + (lower + 2 ** 15))[0:SUBLANES]
        need_ref[...] = jnp.broadcast_to(top_k - n_gt, need_ref.shape)

    thr = thr_ref[0:1, :]
    need = need_ref[0:1, :]

    m_ref[...] = jnp.full_like(m_ref, MASK_VALUE)
    acc_ref[...] = jnp.zeros_like(acc_ref)
    earlier = (lax.broadcasted_iota(jnp.int32, (kt, kt), 1)
               < lax.broadcasted_iota(jnp.int32, (kt, kt), 0))
    earlier_bf = jnp.where(earlier, 1.0, 0.0).astype(BF16)
    ones_rows = jnp.ones((SUBLANES, kt), BF16)
    dim_v = lax.broadcasted_iota(jnp.int32, (KV_WIDTH, kt), 0)
    q_stack = [jnp.concatenate([q_ref[:, h * HEAD_DIM:(h + 1) * HEAD_DIM]
                                for h in range(g * KV_GROUP, (g + 1) * KV_GROUP)], axis=0)
               for g in range(N_KV_HEADS)]

    def attend_tile(ti, ties_before):
        base = pl.multiple_of(ti * kt, kt)
        k_tile = k_ref[pl.ds(base, kt), :]
        s_all = [_nt_dot(k_tile[:, g * HEAD_DIM:(g + 1) * HEAD_DIM], q_stack[g])
                 for g in range(N_KV_HEADS)]
        keys = key_ref[ti]
        tie = keys == thr
        tie_bf = jnp.where(tie, 1.0, 0.0).astype(BF16)
        rank = _dot(earlier_bf, tie_bf) + ties_before
        chosen = ((keys > thr) | (tie & (rank < need))) & (base + key_pos <= query_pos)
        bias = jnp.where(chosen, 0.0, MASK_VALUE)
        v_t = v_ref[pl.ds(base, kt), :].astype(F32).T
        for g in range(N_KV_HEADS):
            v_aug = jnp.where(dim_v // HEAD_DIM == g, v_t, 1.0).astype(BF16)
            for hh in range(KV_GROUP):
                h = g * KV_GROUP + hh
                s = s_all[g][:, hh * qb:(hh + 1) * qb] + bias
                m_old = m_ref[h]
                m_new = jnp.maximum(m_old, jnp.max(s, axis=0, keepdims=True))
                p = jnp.exp2(s - m_new)
                acc_ref[h] = jnp.exp2(m_old - m_new) * acc_ref[h] + _dot(v_aug, p.astype(BF16))
                m_ref[h] = m_new
        return ties_before + _dot(ones_rows, tie_bf)[0:1]

    lax.fori_loop(0, n_tiles, attend_tile, jnp.zeros((1, qb), F32))
    for pair in range(N_HEADS // 2):
        outs = []
        for h in (2 * pair, 2 * pair + 1):
            g = h // KV_GROUP
            acc = acc_ref[h]
            other = (1 - g) * HEAD_DIM
            outs.append(acc[g * HEAD_DIM:(g + 1) * HEAD_DIM, :] / acc[other:other + 1, :])
        out_ref[:, pair * LANES:(pair + 1) * LANES] = (
            jnp.concatenate(outs, axis=0).T.astype(out_ref.dtype))


def _dsa(q, k, v, qi, ki, wi, batch, seq_len):
    n_tok = q.shape[0]
    qb = min(Q_TILE, seq_len)
    kt = min(KEY_TILE, seq_len)
    assert seq_len % qb == 0 and seq_len % kt == 0 and kt % qb == 0 and qb % LANES == 0
    assert N_KV_HEADS == 2 and KV_WIDTH == LANES and 2 * HEAD_DIM == LANES
    steps = seq_len // qb
    n_kt = seq_len // kt
    top_k = min(TOPK_MAX, seq_len // 4)
    qtile = lambda w: pl.BlockSpec((qb, w), lambda bi, qi_: (bi * steps + qi_, 0))
    seq = lambda w: pl.BlockSpec((seq_len, w), lambda bi, qi_: (bi, 0))
    return pl.pallas_call(
        functools.partial(_dsa_kernel, top_k=top_k),
        grid=(batch, steps),
        in_specs=[qtile(ATTN_WIDTH), seq(KV_WIDTH), seq(KV_WIDTH),
                  qtile(IDX_HEADS * IDX_CAT), seq(IDX_CAT), qtile(LANES)],
        out_specs=qtile(ATTN_WIDTH),
        out_shape=jax.ShapeDtypeStruct((n_tok, ATTN_WIDTH), BF16),
        scratch_shapes=[pltpu.VMEM((n_kt, kt, qb), jnp.int32),
                        pltpu.VMEM((n_kt, kt, qb), jnp.int16),
                        pltpu.VMEM((n_kt, kt, qb), jnp.int16),
                        pltpu.VMEM((SUBLANES, qb), jnp.int32),
                        pltpu.VMEM((SUBLANES, qb), F32),
                        pltpu.VMEM((N_HEADS, 1, qb), F32),
                        pltpu.VMEM((N_HEADS, KV_WIDTH, qb), F32)],
        compiler_params=pltpu.CompilerParams(dimension_semantics=("arbitrary", "arbitrary"),
                                             vmem_limit_bytes=VMEM_LIMIT_BYTES),
        name="dsa",
    )(q, k, v, qi, ki, wi)


def _out_ffn_kernel(x_ref, yp_ref, yh_ref, ya_ref, wo_hbm, g2_ref, wfi_hbm, wfo_hbm, gf_ref,
                    out_ref, wo_ref, wfi_ref, wfo_ref, *, layer, final_norm):
    @pl.when(pl.program_id(0) == 0)
    def _():
        pltpu.sync_copy(wo_hbm.at[layer], wo_ref)
        pltpu.sync_copy(wfi_hbm.at[layer], wfi_ref)
        pltpu.sync_copy(wfo_hbm.at[layer], wfo_ref)

    x = x_ref[...]
    x = x + _dot(yp_ref[...], wo_ref[0:POOL_WIDTH, :])
    x = x + _dot(yh_ref[...], wo_ref[POOL_WIDTH:POOL_WIDTH + HGRN_WIDTH, :])
    x = x + _dot(ya_ref[...], wo_ref[POOL_WIDTH + HGRN_WIDTH:, :])
    h = _rms_norm(x, g2_ref[...]).astype(BF16)
    a = jnp.maximum(_dot(h, wfi_ref[...]), 0.0)
    x = x + _dot((a * a).astype(BF16), wfo_ref[...])
    if final_norm:
        x = _rms_norm(x, gf_ref[...])
    out_ref[...] = x


def _out_ffn(x2d, y_pool, y_hgrn, y_attn, w_out, norm2_g, w_ff_in, w_ff_out, final_g,
             layer, final_norm, seq_len):
    n_tok, d_model = x2d.shape
    d_ff = w_ff_in.shape[2]
    tm = min(PROJ_TILE, seq_len)
    tok = lambda w: pl.BlockSpec((tm, w), lambda i: (i, 0))
    hbm = pl.BlockSpec(memory_space=pl.ANY)
    return pl.pallas_call(
        functools.partial(_out_ffn_kernel, layer=layer, final_norm=final_norm),
        grid=(n_tok // tm,),
        in_specs=[tok(d_model), tok(POOL_WIDTH), tok(HGRN_WIDTH), tok(ATTN_WIDTH),
                  hbm, pl.BlockSpec((None, 1, d_model), lambda i: (layer, 0, 0)), hbm, hbm,
                  pl.BlockSpec((1, d_model), lambda i: (0, 0))],
        out_specs=tok(d_model),
        out_shape=jax.ShapeDtypeStruct((n_tok, d_model), F32),
        scratch_shapes=[pltpu.VMEM((d_model, d_model), BF16),
                        pltpu.VMEM((d_model, d_ff), BF16),
                        pltpu.VMEM((d_ff, d_model), BF16)],
        compiler_params=pltpu.CompilerParams(dimension_semantics=("arbitrary",),
                                             vmem_limit_bytes=VMEM_LIMIT_BYTES),
        name="out_ffn",
    )(x2d, y_pool, y_hgrn, y_attn, w_out, norm2_g, w_ff_in, w_ff_out, final_g)


def _rope_patterns(positions):
    inv_freq = ROPE_THETA ** (-jnp.arange(0, ROT_DIM, 2, dtype=F32) / ROT_DIM)
    ang = positions.astype(F32).reshape(-1, 1) * inv_freq
    cos, sin = jnp.cos(ang), jnp.sin(ang)
    n = ang.shape[0]
    rest = HEAD_DIM - ROT_DIM
    cos_h = jnp.concatenate([cos, cos, jnp.ones((n, rest), F32)], axis=1)
    slo_h = jnp.concatenate([-sin, jnp.zeros((n, ROT_HALF + rest), F32)], axis=1)
    shi_h = jnp.concatenate([jnp.zeros((n, ROT_HALF), F32), sin, jnp.zeros((n, rest), F32)], axis=1)
    two = lambda a: jnp.concatenate([a, a], axis=1)
    return two(cos_h), two(slo_h), two(shi_h)


def kernel(x, positions, norm1_g, w_in, pool_w, pool_scale, lb_logits, hgrn_norm_g, w_out,
           norm2_g, w_ff_in, w_ff_out, final_norm_g):
    batch, seq_len, d_model = x.shape
    depth = w_in.shape[0]
    assert w_in.shape[2] == D_IN

    p_lb = jax.nn.softmax(lb_logits.astype(F32), axis=0)
    lower_bounds = (jnp.cumsum(p_lb, axis=0) - p_lb[0])[:, None, :]
    w_in_pad = jnp.pad(w_in, ((0, 0), (0, 0), (0, D_IN_PAD - D_IN))).astype(BF16)
    groups = len(POOL_WINDOWS)
    eye = jnp.eye(groups, dtype=F32)
    pool_w_bd = (pool_w[:, :, :, None, :] * eye[None, :, None, :, None]).reshape(
        depth, POOL_WIDTH, POOL_WIDTH).astype(BF16)
    w_out_b, w_ff_in_b, w_ff_out_b = (w.astype(BF16) for w in (w_out, w_ff_in, w_ff_out))
    row = lambda a: a.reshape(depth, 1, -1).astype(F32)
    norm1, norm2, pscale, hnorm = row(norm1_g), row(norm2_g), row(pool_scale), row(hgrn_norm_g)
    final_g = final_norm_g.reshape(1, d_model).astype(F32)
    cos, slo, shi = _rope_patterns(positions)

    x2d = x.reshape(batch * seq_len, d_model)
    for layer in range(depth):
        y_pool, hg, q, k, v, qi, ki, wi = _in_proj(x2d, norm1, w_in_pad, cos, slo, shi, pool_w_bd,
                                                pscale, layer, seq_len)
        y_hgrn = _hgrn(hg, lower_bounds, hnorm, layer, batch, seq_len)
        y_attn = _dsa(q, k, v, qi, ki, wi, batch, seq_len)
        x2d = _out_ffn(x2d, y_pool, y_hgrn, y_attn, w_out_b, norm2, w_ff_in_b, w_ff_out_b,
                       final_g, layer, layer == depth - 1, seq_len)
    return x2d.reshape(batch, seq_len, d_model)
```

```python
import functools

import jax
import jax.numpy as jnp
from jax import lax
from jax.experimental import pallas as pl
from jax.experimental.pallas import tpu as pltpu

F32 = jnp.float32
BF16 = jnp.bfloat16

POOL_WINDOWS = (2, 4, 8, 16)
POOL_WIDTH = 256
POOL_GROUP_DIM = 64
HGRN_WIDTH = 256
HGRN_HEAD_DIM = 64
LOG_F_FLOOR = 1e-30
ATTN_WIDTH = 512
HEAD_DIM = 64
N_HEADS = 8
N_KV_HEADS = 2
KV_GROUP = N_HEADS // N_KV_HEADS
KV_WIDTH = N_KV_HEADS * HEAD_DIM
IDX_HEADS = 4
IDX_DIM = 64
TOPK_MAX = 256
MASK_VALUE = -1e30
ROPE_THETA = 500000.0
ROT_DIM = HEAD_DIM // 4
ROT_HALF = ROT_DIM // 2
RMS_EPS = 1e-5
IDX_W_SCALE = (IDX_HEADS ** -0.5) * (IDX_DIM ** -0.5)
ATTN_SCALE = HEAD_DIM ** -0.5
LOG2_E = 1.4426950408889634

LANES = 128
SUBLANES = 8
VMEM_LIMIT_BYTES = 56 * 1024 * 1024

COL_POOL = 0
COL_HGRN = COL_POOL + POOL_WIDTH
COL_Q = COL_HGRN + 4 * HGRN_WIDTH
COL_K = COL_Q + ATTN_WIDTH
COL_V = COL_K + KV_WIDTH
COL_QI = COL_V + KV_WIDTH
COL_KI = COL_QI + IDX_HEADS * IDX_DIM
D_IN = COL_KI + IDX_DIM + IDX_HEADS
D_IN_PAD = COL_KI + LANES
IDX_CAT = 4 * IDX_DIM

PROJ_TILE = 512
PROJ_SUB = 128
HGRN_TILE = 256
HGRN_CHUNK = 16
Q_TILE = 256
KEY_TILE = 256
INT_MIN = -2 ** 31


def _nt_dot(a, b):
    return lax.dot_general(a, b, (((1,), (1,)), ((), ())), preferred_element_type=F32)


def _tn_dot(a, b):
    return lax.dot_general(a, b, (((0,), (0,)), ((), ())), preferred_element_type=F32)


def _dot(a, b):
    return jnp.dot(a, b, preferred_element_type=F32)


def _split_bf16(a):
    hi = a.astype(BF16)
    lo = (a - hi.astype(F32)).astype(BF16)
    return hi, lo


def _rms_norm(x, g):
    return x * lax.rsqrt(jnp.mean(x * x, axis=-1, keepdims=True) + RMS_EPS) * g


def _silu(x):
    return x * jax.nn.sigmoid(x)


def _rope(x, cos, sin_lo, sin_hi):
    w = x.shape[1]
    rep = w // LANES
    if rep > 1:
        cos = jnp.concatenate([cos] * rep, axis=1)
        sin_lo = jnp.concatenate([sin_lo] * rep, axis=1)
        sin_hi = jnp.concatenate([sin_hi] * rep, axis=1)
    from_below = pltpu.roll(x, ROT_HALF, 1)
    from_above = pltpu.roll(x, w - ROT_HALF, 1)
    return x * cos + from_below * sin_hi + from_above * sin_lo


def _in_proj_kernel(x_ref, g_ref, w_ref, cos_ref, slo_ref, shi_ref, pw_ref, pscale_ref,
                    ypool_ref, hg_ref, q_ref, k_ref, v_ref, qi_ref, ki_ref, wi_ref,
                    halo_ref, *, tiles_per_seq):
    tm = x_ref.shape[0]
    sub = min(PROJ_SUB, tm)
    seq_tile = pl.program_id(0) % tiles_per_seq

    @pl.when(seq_tile == 0)
    def _():
        halo_ref[...] = jnp.zeros_like(halo_ref)

    prev = halo_ref[...]
    for r in range(tm // sub):
        prev = _in_proj_rows(slice(r * sub, (r + 1) * sub), seq_tile * tm + r * sub, prev,
                             x_ref, g_ref, w_ref, cos_ref, slo_ref, shi_ref, pw_ref, pscale_ref,
                             ypool_ref, hg_ref, q_ref, k_ref, v_ref, qi_ref, ki_ref, wi_ref)
    halo_ref[...] = prev


def _in_proj_rows(rows, t_start, prev, x_ref, g_ref, w_ref, cos_ref, slo_ref, shi_ref, pw_ref,
                  pscale_ref, ypool_ref, hg_ref, q_ref, k_ref, v_ref, qi_ref, ki_ref, wi_ref):
    tm = rows.stop - rows.start
    halo = prev.shape[0]
    h = _rms_norm(x_ref[rows, :], g_ref[...])
    proj = _dot(h.astype(BF16), w_ref[...])

    u = proj[:, COL_POOL:COL_POOL + POOL_WIDTH]
    ext = jnp.concatenate([prev, u], axis=0)
    sums = []
    acc = ext
    span = 1
    for win in POOL_WINDOWS:
        while span < win:
            acc = acc + pltpu.roll(acc, span, 0)
            span *= 2
        sums.append(acc[halo:, :])
    lane = lax.broadcasted_iota(jnp.int32, (tm, POOL_WIDTH), 1)
    group = lane // POOL_GROUP_DIM
    wsum = sums[-1]
    win_lane = jnp.full((tm, POOL_WIDTH), POOL_WINDOWS[-1], jnp.int32)
    for gi in range(len(POOL_WINDOWS) - 2, -1, -1):
        wsum = jnp.where(group == gi, sums[gi], wsum)
        win_lane = jnp.where(group == gi, POOL_WINDOWS[gi], win_lane)
    t_pos = t_start + lax.broadcasted_iota(jnp.int32, (tm, POOL_WIDTH), 0)
    count = jnp.minimum(t_pos + 1, win_lane).astype(F32)
    pooled = wsum / count - u
    y_pool = _dot(pooled.astype(BF16), pw_ref[...]) * pscale_ref[...]
    ypool_ref[rows, :] = y_pool.astype(ypool_ref.dtype)

    hg_ref[rows, :] = proj[:, COL_HGRN:COL_HGRN + 4 * HGRN_WIDTH]

    cos, slo, shi = cos_ref[rows, :], slo_ref[rows, :], shi_ref[rows, :]
    q = _rope(proj[:, COL_Q:COL_Q + ATTN_WIDTH], cos, slo, shi) * (ATTN_SCALE * LOG2_E)
    q_ref[rows, :] = q.astype(q_ref.dtype)
    k = _rope(proj[:, COL_K:COL_K + KV_WIDTH], cos, slo, shi)
    k_ref[rows, :] = k.astype(k_ref.dtype)
    v_ref[rows, :] = proj[:, COL_V:COL_V + KV_WIDTH].astype(v_ref.dtype)
    qi = _rope(proj[:, COL_QI:COL_QI + IDX_HEADS * IDX_DIM], cos, slo, shi)
    qi_hi = qi.astype(BF16).astype(F32)
    qi_lo = qi - qi_hi
    zeros = jnp.zeros((tm, IDX_DIM), F32)
    pieces = []
    for hd in range(IDX_HEADS):
        cols = slice(hd * IDX_DIM, (hd + 1) * IDX_DIM)
        pieces += [qi_hi[:, cols], qi_hi[:, cols], qi_lo[:, cols], zeros]
    qi_ref[rows, :] = jnp.concatenate(pieces, axis=1).astype(qi_ref.dtype)
    kiw_raw = proj[:, COL_KI:COL_KI + LANES]
    ki = _rope(kiw_raw, cos, slo, shi)[:, 0:IDX_DIM]
    ki_hi = ki.astype(BF16).astype(F32)
    ki_ref[rows, :] = jnp.concatenate([ki_hi, ki - ki_hi, ki_hi, zeros], axis=1).astype(ki_ref.dtype)
    wi_ref[rows, :] = kiw_raw * IDX_W_SCALE
    return u[tm - halo:, :]


def _in_proj(x2d, norm_g, w_in_pad, cos, slo, shi, pool_w_bd, pool_scale, layer, seq_len):
    n_tok, d_model = x2d.shape
    tm = min(PROJ_TILE, seq_len)
    assert seq_len % tm == 0 and n_tok % tm == 0
    halo = 2 * SUBLANES
    assert halo >= POOL_WINDOWS[-1] and tm >= halo
    tok = lambda w: pl.BlockSpec((tm, w), lambda i: (i, 0))
    lay = lambda *s: pl.BlockSpec((None,) + s, lambda i: (layer,) + (0,) * len(s))
    out_shape = (
        jax.ShapeDtypeStruct((n_tok, POOL_WIDTH), BF16),
        jax.ShapeDtypeStruct((n_tok, 4 * HGRN_WIDTH), F32),
        jax.ShapeDtypeStruct((n_tok, ATTN_WIDTH), BF16),
        jax.ShapeDtypeStruct((n_tok, KV_WIDTH), BF16),
        jax.ShapeDtypeStruct((n_tok, KV_WIDTH), BF16),
        jax.ShapeDtypeStruct((n_tok, IDX_HEADS * IDX_CAT), BF16),
        jax.ShapeDtypeStruct((n_tok, IDX_CAT), BF16),
        jax.ShapeDtypeStruct((n_tok, LANES), F32),
    )
    return pl.pallas_call(
        functools.partial(_in_proj_kernel, tiles_per_seq=seq_len // tm),
        grid=(n_tok // tm,),
        in_specs=[tok(d_model), lay(1, d_model), lay(d_model, D_IN_PAD),
                  tok(LANES), tok(LANES), tok(LANES),
                  lay(POOL_WIDTH, POOL_WIDTH), lay(1, POOL_WIDTH)],
        out_specs=tuple(tok(s.shape[1]) for s in out_shape),
        out_shape=out_shape,
        scratch_shapes=[pltpu.VMEM((halo, POOL_WIDTH), F32)],
        compiler_params=pltpu.CompilerParams(dimension_semantics=("arbitrary",),
                                             vmem_limit_bytes=VMEM_LIMIT_BYTES),
        name="in_proj",
    )(x2d, norm_g, w_in_pad, cos, slo, shi, pool_w_bd, pool_scale)


def _hgrn_kernel(hg_ref, lb_ref, ng_ref, out_ref, state_ref, o_ref):
    tb, width = out_ref.shape
    c = HGRN_CHUNK

    @pl.when(pl.program_id(1) == 0)
    def _():
        state_ref[...] = jnp.zeros_like(state_ref)

    lb = lb_ref[...]
    q = _silu(hg_ref[:, 0:width])
    sig = jax.nn.sigmoid(hg_ref[:, width:2 * width])
    f = lb + (1.0 - lb) * sig
    log_f = jnp.log(jnp.maximum(f, LOG_F_FLOOR))
    k = (1.0 - lb) * (1.0 - sig)
    v = hg_ref[:, 2 * width:3 * width]

    row_in_chunk = lax.broadcasted_iota(jnp.int32, (tb, width), 0) % c
    b = log_f
    step = 1
    while step < c:
        b = b + jnp.where(row_in_chunk >= step, pltpu.roll(b, step, 0), 0.0)
        step *= 2

    lane_r = lax.broadcasted_iota(jnp.int32, (width, width), 0) // HGRN_HEAD_DIM
    lane_c = lax.broadcasted_iota(jnp.int32, (width, width), 1) // HGRN_HEAD_DIM
    same_head = lane_r == lane_c
    head_sum = jnp.where(same_head, 1.0, 0.0).astype(BF16)

    o = None
    for d in range(c):
        if d == 0:
            p, vd = q * k, v
        else:
            kd, bd, vd = (pltpu.roll(a, d, 0) for a in (k, b, v))
            decay = jnp.exp(jnp.minimum(b - bd, 0.0))
            p = jnp.where(row_in_chunk >= d, q * kd * decay, 0.0)
        score = _dot(p.astype(BF16), head_sum)
        o = score * vd if o is None else o + score * vd
    o_ref[...] = o

    for ci in range(tb // c):
        rows = slice(ci * c, (ci + 1) * c)
        bc = b[rows]
        b_last = bc[c - 1:c]
        state = state_ref[...]
        q_dec = q[rows] * jnp.exp(bc)
        o_ref[rows, :] += _nt_dot(q_dec.astype(BF16), state.astype(BF16))
        k_dec = k[rows] * jnp.exp(b_last - bc)
        upd = _tn_dot(v[rows].astype(BF16), k_dec.astype(BF16))
        state_ref[...] = state * jnp.exp(b_last) + jnp.where(same_head, upd, 0.0)

    o = o_ref[...]
    sq_hi, sq_lo = _split_bf16(o * o)
    mean_sq = (_dot(sq_hi, head_sum) + _dot(sq_lo, head_sum)) * (1.0 / HGRN_HEAD_DIM)
    gate = _silu(hg_ref[:, 3 * width:4 * width])
    out_ref[...] = (o * lax.rsqrt(mean_sq + RMS_EPS) * ng_ref[...] * gate).astype(out_ref.dtype)


def _hgrn(hg, lower_bound, norm_g, layer, batch, seq_len):
    n_tok = hg.shape[0]
    tb = min(HGRN_TILE, seq_len)
    assert seq_len % tb == 0 and tb % HGRN_CHUNK == 0 and HGRN_CHUNK % SUBLANES == 0
    steps = seq_len // tb
    lay = pl.BlockSpec((None, 1, HGRN_WIDTH), lambda bi, ti: (layer, 0, 0))
    return pl.pallas_call(
        _hgrn_kernel,
        grid=(batch, steps),
        in_specs=[pl.BlockSpec((tb, 4 * HGRN_WIDTH), lambda bi, ti: (bi * steps + ti, 0)),
                  lay, lay],
        out_specs=pl.BlockSpec((tb, HGRN_WIDTH), lambda bi, ti: (bi * steps + ti, 0)),
        out_shape=jax.ShapeDtypeStruct((n_tok, HGRN_WIDTH), BF16),
        scratch_shapes=[pltpu.VMEM((HGRN_WIDTH, HGRN_WIDTH), F32),
                        pltpu.VMEM((tb, HGRN_WIDTH), F32)],
        compiler_params=pltpu.CompilerParams(dimension_semantics=("arbitrary", "arbitrary"),
                                             vmem_limit_bytes=VMEM_LIMIT_BYTES),
        name="hgrn2",
    )(hg, lower_bound, norm_g)


def _dsa_kernel(q_ref, k_ref, v_ref, qi_ref, ki_ref, wi_ref, out_ref,
                key_ref, hi_ref, lo_ref, thr_ref, need_ref, m_ref, acc_ref, *, top_k):
    qb = q_ref.shape[0]
    kt = key_ref.shape[1]
    i16 = jnp.int16
    rows16 = 2 * SUBLANES
    span = 4 * rows16
    tall = lambda a: jnp.concatenate([a] * (span // rows16), axis=0)
    t0 = pl.program_id(1) * qb
    n_tiles = (t0 + qb + kt - 1) // kt

    key_pos = lax.broadcasted_iota(jnp.int32, (kt, qb), 0)
    query_pos = t0 + lax.broadcasted_iota(jnp.int32, (kt, qb), 1)

    wi_t = wi_ref[...].T
    w_heads = [wi_t[IDX_DIM + h:IDX_DIM + h + 1, :] for h in range(IDX_HEADS)]

    def index_tile(ti, carry):
        base = pl.multiple_of(ti * kt, kt)
        ki_tile = ki_ref[pl.ds(base, kt), :]
        score = jnp.zeros((kt, qb), F32)
        for h in range(IDX_HEADS):
            s = _nt_dot(ki_tile, qi_ref[:, h * IDX_CAT:(h + 1) * IDX_CAT])
            score = score + jnp.maximum(s, 0.0) * w_heads[h]
        score = jnp.where(score == 0.0, 0.0, score)
        score = jnp.where(base + key_pos <= query_pos, score, MASK_VALUE)
        bits = lax.bitcast_convert_type(score, jnp.int32)
        key = bits ^ ((bits >> 31) & 0x7FFFFFFF)
        key_ref[ti] = key
        hi_ref[ti] = (key >> 16).astype(i16)
        lo_ref[ti] = ((key & 0xFFFF) - 2 ** 15).astype(i16)
        return carry

    lax.fori_loop(0, n_tiles, index_tile, 0)

    def count(ref, pred):
        def body(ti, acc):
            keys = ref[ti]
            for r in range(kt // span):
                acc = acc + jnp.where(pred(keys[r * span:(r + 1) * span, :]), i16(1), i16(0))
            return acc
        partial = lax.fori_loop(0, n_tiles, body, jnp.zeros((span, qb), i16))
        return jnp.sum(partial.astype(F32), axis=0, keepdims=True)

    def kth_largest(ref, target):
        def bit(it, thr):
            cand = thr + lax.shift_left(jnp.int32(1), 15 - it)
            cand16 = tall(cand.astype(i16))
            n_ge = count(ref, lambda keys: keys >= cand16)
            return jnp.where(n_ge >= target, cand, thr)
        return lax.fori_loop(0, 16, bit, jnp.full((rows16, qb), -2 ** 15, jnp.int32))

    thr_ref[...] = jnp.full_like(thr_ref, INT_MIN)
    need_ref[...] = jnp.zeros_like(need_ref)

    @pl.when(t0 + qb > top_k)
    def _():
        upper = kth_largest(hi_ref, float(top_k))
        upper16 = tall(upper.astype(i16))
        n_upper_gt = count(hi_ref, lambda keys: keys > upper16)
        upper_kt = jnp.concatenate([upper16] * (kt // span), axis=0)

        def keep_bucket(ti, carry):
            lo_ref[ti] = jnp.where(hi_ref[ti] == upper_kt, lo_ref[ti], i16(-2 ** 15))
            return carry

        lax.fori_loop(0, n_tiles, keep_bucket, 0)
        lower = kth_largest(lo_ref, top_k - n_upper_gt)
        lower16 = tall(lower.astype(i16))
        n_gt = n_upper_gt + count(lo_ref, lambda keys: keys > lower16)
        thr_ref[...] = (lax.shift_left(upper, 16) + (lower + 2 ** 15))[0:SUBLANES]
        need_ref[...] = jnp.broadcast_to(top_k - n_gt, need_ref.shape)

    thr = thr_ref[0:1, :]
    need = need_ref[0:1, :]

    m_ref[...] = jnp.full_like(m_ref, MASK_VALUE)
    acc_ref[...] = jnp.zeros_like(acc_ref)
    earlier = (lax.broadcasted_iota(jnp.int32, (kt, kt), 1)
               < lax.broadcasted_iota(jnp.int32, (kt, kt), 0))
    earlier_bf = jnp.where(earlier, 1.0, 0.0).astype(BF16)
    ones_rows = jnp.ones((SUBLANES, kt), BF16)
    dim_v = lax.broadcasted_iota(jnp.int32, (KV_WIDTH, kt), 0)
    q_stack = [jnp.concatenate([q_ref[:, h * HEAD_DIM:(h + 1) * HEAD_DIM]
                                for h in range(g * KV_GROUP, (g + 1) * KV_GROUP)], axis=0)
               for g in range(N_KV_HEADS)]

    def attend_tile(ti, ties_before):
        base = pl.multiple_of(ti * kt, kt)
        k_tile = k_ref[pl.ds(base, kt), :]
        s_all = [_nt_dot(k_tile[:, g * HEAD_DIM:(g + 1) * HEAD_DIM], q_stack[g])
                 for g in range(N_KV_HEADS)]
        keys = key_ref[ti]
        tie = keys == thr
        tie_bf = jnp.where(tie, 1.0, 0.0).astype(BF16)
        rank = _dot(earlier_bf, tie_bf) + ties_before
        chosen = ((keys > thr) | (tie & (rank < need))) & (base + key_pos <= query_pos)
        bias = jnp.where(chosen, 0.0, MASK_VALUE)
        v_t = v_ref[pl.ds(base, kt), :].astype(F32).T
        for g in range(N_KV_HEADS):
            v_aug = jnp.where(dim_v // HEAD_DIM == g, v_t, 1.0).astype(BF16)
            for hh in range(KV_GROUP):
                h = g * KV_GROUP + hh
                s = s_all[g][:, hh * qb:(hh + 1) * qb] + bias
                m_old = m_ref[h]
                m_new = jnp.maximum(m_old, jnp.max(s, axis=0, keepdims=True))
                p = jnp.exp2(s - m_new)
                acc_ref[h] = jnp.exp2(m_old - m_new) * acc_ref[h] + _dot(v_aug, p.astype(BF16))
                m_ref[h] = m_new
        return ties_before + _dot(ones_rows, tie_bf)[0:1]

    lax.fori_loop(0, n_tiles, attend_tile, jnp.zeros((1, qb), F32))
    for pair in range(N_HEADS // 2):
        outs = []
        for h in (2 * pair, 2 * pair + 1):
            g = h // KV_GROUP
            acc = acc_ref[h]
            other = (1 - g) * HEAD_DIM
            outs.append(acc[g * HEAD_DIM:(g + 1) * HEAD_DIM, :] / acc[other:other + 1, :])
        out_ref[:, pair * LANES:(pair + 1) * LANES] = (
            jnp.concatenate(outs, axis=0).T.astype(out_ref.dtype))


def _dsa(q, k, v, qi, ki, wi, batch, seq_len):
    n_tok = q.shape[0]
    qb = min(Q_TILE, seq_len)
    kt = min(KEY_TILE, seq_len)
    assert seq_len % qb == 0 and seq_len % kt == 0 and kt % qb == 0 and qb % LANES == 0
    assert N_KV_HEADS == 2 and KV_WIDTH == LANES and 2 * HEAD_DIM == LANES
    steps = seq_len // qb
    n_kt = seq_len // kt
    top_k = min(TOPK_MAX, seq_len // 4)
    qtile = lambda w: pl.BlockSpec((qb, w), lambda bi, qi_: (bi * steps + qi_, 0))
    seq = lambda w: pl.BlockSpec((seq_len, w), lambda bi, qi_: (bi, 0))
    return pl.pallas_call(
        functools.partial(_dsa_kernel, top_k=top_k),
        grid=(batch, steps),
        in_specs=[qtile(ATTN_WIDTH), seq(KV_WIDTH), seq(KV_WIDTH),
                  qtile(IDX_HEADS * IDX_CAT), seq(IDX_CAT), qtile(LANES)],
        out_specs=qtile(ATTN_WIDTH),
        out_shape=jax.ShapeDtypeStruct((n_tok, ATTN_WIDTH), BF16),
        scratch_shapes=[pltpu.VMEM((n_kt, kt, qb), jnp.int32),
                        pltpu.VMEM((n_kt, kt, qb), jnp.int16),
                        pltpu.VMEM((n_kt, kt, qb), jnp.int16),
                        pltpu.VMEM((SUBLANES, qb), jnp.int32),
                        pltpu.VMEM((SUBLANES, qb), F32),
                        pltpu.VMEM((N_HEADS, 1, qb), F32),
                        pltpu.VMEM((N_HEADS, KV_WIDTH, qb), F32)],
        compiler_params=pltpu.CompilerParams(dimension_semantics=("arbitrary", "arbitrary"),
                                             vmem_limit_bytes=VMEM_LIMIT_BYTES),
        name="dsa",
    )(q, k, v, qi, ki, wi)


def _out_ffn_kernel(x_ref, yp_ref, yh_ref, ya_ref, wo_hbm, g2_ref, wfi_hbm, wfo_hbm, gf_ref,
                    out_ref, wo_ref, wfi_ref, wfo_ref, *, layer, final_norm):
    @pl.when(pl.program_id(0) == 0)
    def _():
        pltpu.sync_copy(wo_hbm.at[layer], wo_ref)
        pltpu.sync_copy(wfi_hbm.at[layer], wfi_ref)
        pltpu.sync_copy(wfo_hbm.at[layer], wfo_ref)

    x = x_ref[...]
    x = x + _dot(yp_ref[...], wo_ref[0:POOL_WIDTH, :])
    x = x + _dot(yh_ref[...], wo_ref[POOL_WIDTH:POOL_WIDTH + HGRN_WIDTH, :])
    x = x + _dot(ya_ref[...], wo_ref[POOL_WIDTH + HGRN_WIDTH:, :])
    h = _rms_norm(x, g2_ref[...]).astype(BF16)
    a = jnp.maximum(_dot(h, wfi_ref[...]), 0.0)
    x = x + _dot((a * a).astype(BF16), wfo_ref[...])
    if final_norm:
        x = _rms_norm(x, gf_ref[...])
    out_ref[...] = x


def _out_ffn(x2d, y_pool, y_hgrn, y_attn, w_out, norm2_g, w_ff_in, w_ff_out, final_g,
             layer, final_norm, seq_len):
    n_tok, d_model = x2d.shape
    d_ff = w_ff_in.shape[2]
    tm = min(PROJ_TILE, seq_len)
    tok = lambda w: pl.BlockSpec((tm, w), lambda i: (i, 0))
    hbm = pl.BlockSpec(memory_space=pl.ANY)
    return pl.pallas_call(
        functools.partial(_out_ffn_kernel, layer=layer, final_norm=final_norm),
        grid=(n_tok // tm,),
        in_specs=[tok(d_model), tok(POOL_WIDTH), tok(HGRN_WIDTH), tok(ATTN_WIDTH),
                  hbm, pl.BlockSpec((None, 1, d_model), lambda i: (layer, 0, 0)), hbm, hbm,
                  pl.BlockSpec((1, d_model), lambda i: (0, 0))],
        out_specs=tok(d_model),
        out_shape=jax.ShapeDtypeStruct((n_tok, d_model), F32),
        scratch_shapes=[pltpu.VMEM((d_model, d_model), BF16),
                        pltpu.VMEM((d_model, d_ff), BF16),
                        pltpu.VMEM((d_ff, d_model), BF16)],
        compiler_params=pltpu.CompilerParams(dimension_semantics=("arbitrary",),
                                             vmem_limit_bytes=VMEM_LIMIT_BYTES),
        name="out_ffn",
    )(x2d, y_pool, y_hgrn, y_attn, w_out, norm2_g, w_ff_in, w_ff_out, final_g)


def _rope_patterns(positions):
    inv_freq = ROPE_THETA ** (-jnp.arange(0, ROT_DIM, 2, dtype=F32) / ROT_DIM)
    rest = jnp.zeros((HEAD_DIM - ROT_DIM,), F32)
    freq_head = jnp.concatenate([inv_freq, inv_freq, rest])
    lo_head = jnp.concatenate([-jnp.ones((ROT_HALF,), F32), jnp.zeros((ROT_HALF,), F32), rest])
    hi_head = jnp.concatenate([jnp.zeros((ROT_HALF,), F32), jnp.ones((ROT_HALF,), F32), rest])
    two = lambda a: jnp.concatenate([a, a])[None, :]
    ang = positions.astype(F32).reshape(-1, 1) * two(freq_head)
    sin = jnp.sin(ang)
    return jnp.cos(ang), sin * two(lo_head), sin * two(hi_head)


def kernel(x, positions, norm1_g, w_in, pool_w, pool_scale, lb_logits, hgrn_norm_g, w_out,
           norm2_g, w_ff_in, w_ff_out, final_norm_g):
    batch, seq_len, d_model = x.shape
    depth = w_in.shape[0]
    assert w_in.shape[2] == D_IN

    p_lb = jax.nn.softmax(lb_logits.astype(F32), axis=0)
    lower_bounds = (jnp.cumsum(p_lb, axis=0) - p_lb[0])[:, None, :]
    w_in_pad = jnp.pad(w_in, ((0, 0), (0, 0), (0, D_IN_PAD - D_IN))).astype(BF16)
    groups = len(POOL_WINDOWS)
    eye = jnp.eye(groups, dtype=F32)
    pool_w_bd = (pool_w[:, :, :, None, :] * eye[None, :, None, :, None]).reshape(
        depth, POOL_WIDTH, POOL_WIDTH).astype(BF16)
    w_out_b, w_ff_in_b, w_ff_out_b = (w.astype(BF16) for w in (w_out, w_ff_in, w_ff_out))
    row = lambda a: a.reshape(depth, 1, -1).astype(F32)
    norm1, norm2, pscale, hnorm = row(norm1_g), row(norm2_g), row(pool_scale), row(hgrn_norm_g)
    final_g = final_norm_g.reshape(1, d_model).astype(F32)
    cos, slo, shi = _rope_patterns(positions)

    x2d = x.reshape(batch * seq_len, d_model)
    for layer in range(depth):
        y_pool, hg, q, k, v, qi, ki, wi = _in_proj(x2d, norm1, w_in_pad, cos, slo, shi, pool_w_bd,
                                                pscale, layer, seq_len)
        y_hgrn = _hgrn(hg, lower_bounds, hnorm, layer, batch, seq_len)
        y_attn = _dsa(q, k, v, qi, ki, wi, batch, seq_len)
        x2d = _out_ffn(x2d, y_pool, y_hgrn, y_attn, w_out_b, norm2, w_ff_in_b, w_ff_out_b,
                       final_g, layer, layer == depth - 1, seq_len)
    return x2d.reshape(batch, seq_len, d_model)
```

```python
import functools

import jax
import jax.numpy as jnp
from jax import lax
from jax.experimental import pallas as pl
from jax.experimental.pallas import tpu as pltpu

F32 = jnp.float32
BF16 = jnp.bfloat16

POOL_WINDOWS = (2, 4, 8, 16)
POOL_WIDTH = 256
POOL_GROUP_DIM = 64
HGRN_WIDTH = 256
HGRN_HEAD_DIM = 64
LOG_F_FLOOR = 1e-30
ATTN_WIDTH = 512
HEAD_DIM = 64
N_HEADS = 8
N_KV_HEADS = 2
KV_GROUP = N_HEADS // N_KV_HEADS
KV_WIDTH = N_KV_HEADS * HEAD_DIM
IDX_HEADS = 4
IDX_DIM = 64
TOPK_MAX = 256
MASK_VALUE = -1e30
ROPE_THETA = 500000.0
ROT_DIM = HEAD_DIM // 4
ROT_HALF = ROT_DIM // 2
RMS_EPS = 1e-5
IDX_W_SCALE = (IDX_HEADS ** -0.5) * (IDX_DIM ** -0.5)
ATTN_SCALE = HEAD_DIM ** -0.5
LOG2_E = 1.4426950408889634

LANES = 128
SUBLANES = 8
VMEM_LIMIT_BYTES = 56 * 1024 * 1024

COL_POOL = 0
COL_HGRN = COL_POOL + POOL_WIDTH
COL_Q = COL_HGRN + 4 * HGRN_WIDTH
COL_K = COL_Q + ATTN_WIDTH
COL_V = COL_K + KV_WIDTH
COL_QI = COL_V + KV_WIDTH
COL_KI = COL_QI + IDX_HEADS * IDX_DIM
D_IN = COL_KI + IDX_DIM + IDX_HEADS
D_IN_PAD = COL_KI + LANES
IDX_CAT = 4 * IDX_DIM

PROJ_TILE = 512
PROJ_SUB = 128
HGRN_TILE = 256
HGRN_CHUNK = 16
Q_TILE = 256
KEY_TILE = 256
INT_MIN = -2 ** 31


def _nt_dot(a, b):
    return lax.dot_general(a, b, (((1,), (1,)), ((), ())), preferred_element_type=F32)


def _tn_dot(a, b):
    return lax.dot_general(a, b, (((0,), (0,)), ((), ())), preferred_element_type=F32)


def _dot(a, b):
    return jnp.dot(a, b, preferred_element_type=F32)


def _split_bf16(a):
    hi = a.astype(BF16)
    lo = (a - hi.astype(F32)).astype(BF16)
    return hi, lo


def _rms_norm(x, g):
    return x * lax.rsqrt(jnp.mean(x * x, axis=-1, keepdims=True) + RMS_EPS) * g


def _silu(x):
    return x * jax.nn.sigmoid(x)


def _rope(x, cos, sin_lo, sin_hi):
    w = x.shape[1]
    rep = w // LANES
    if rep > 1:
        cos = jnp.concatenate([cos] * rep, axis=1)
        sin_lo = jnp.concatenate([sin_lo] * rep, axis=1)
        sin_hi = jnp.concatenate([sin_hi] * rep, axis=1)
    from_below = pltpu.roll(x, ROT_HALF, 1)
    from_above = pltpu.roll(x, w - ROT_HALF, 1)
    return x * cos + from_below * sin_hi + from_above * sin_lo


def _in_proj_kernel(x_ref, g_ref, w_ref, cos_ref, slo_ref, shi_ref, pw_ref, pscale_ref,
                    ypool_ref, hg_ref, q_ref, k_ref, v_ref, qi_ref, ki_ref, wi_ref,
                    halo_ref, *, tiles_per_seq):
    tm = x_ref.shape[0]
    sub = min(PROJ_SUB, tm)
    seq_tile = pl.program_id(0) % tiles_per_seq

    @pl.when(seq_tile == 0)
    def _():
        halo_ref[...] = jnp.zeros_like(halo_ref)

    prev = halo_ref[...]
    for r in range(tm // sub):
        prev = _in_proj_rows(slice(r * sub, (r + 1) * sub), seq_tile * tm + r * sub, prev,
                             x_ref, g_ref, w_ref, cos_ref, slo_ref, shi_ref, pw_ref, pscale_ref,
                             ypool_ref, hg_ref, q_ref, k_ref, v_ref, qi_ref, ki_ref, wi_ref)
    halo_ref[...] = prev


def _in_proj_rows(rows, t_start, prev, x_ref, g_ref, w_ref, cos_ref, slo_ref, shi_ref, pw_ref,
                  pscale_ref, ypool_ref, hg_ref, q_ref, k_ref, v_ref, qi_ref, ki_ref, wi_ref):
    tm = rows.stop - rows.start
    halo = prev.shape[0]
    h = _rms_norm(x_ref[rows, :], g_ref[...])
    proj = _dot(h.astype(BF16), w_ref[...])

    u = proj[:, COL_POOL:COL_POOL + POOL_WIDTH]
    ext = jnp.concatenate([prev, u], axis=0)
    sums = []
    acc = ext
    span = 1
    for win in POOL_WINDOWS:
        while span < win:
            acc = acc + pltpu.roll(acc, span, 0)
            span *= 2
        sums.append(acc[halo:, :])
    lane = lax.broadcasted_iota(jnp.int32, (tm, POOL_WIDTH), 1)
    group = lane // POOL_GROUP_DIM
    wsum = sums[-1]
    win_lane = jnp.full((tm, POOL_WIDTH), POOL_WINDOWS[-1], jnp.int32)
    for gi in range(len(POOL_WINDOWS) - 2, -1, -1):
        wsum = jnp.where(group == gi, sums[gi], wsum)
        win_lane = jnp.where(group == gi, POOL_WINDOWS[gi], win_lane)
    t_pos = t_start + lax.broadcasted_iota(jnp.int32, (tm, POOL_WIDTH), 0)
    count = jnp.minimum(t_pos + 1, win_lane).astype(F32)
    pooled = wsum / count - u
    y_pool = _dot(pooled.astype(BF16), pw_ref[...]) * pscale_ref[...]
    ypool_ref[rows, :] = y_pool.astype(ypool_ref.dtype)

    hg_ref[rows, :] = proj[:, COL_HGRN:COL_HGRN + 4 * HGRN_WIDTH]

    cos, slo, shi = cos_ref[rows, :], slo_ref[rows, :], shi_ref[rows, :]
    q = _rope(proj[:, COL_Q:COL_Q + ATTN_WIDTH], cos, slo, shi) * (ATTN_SCALE * LOG2_E)
    q_ref[rows, :] = q.astype(q_ref.dtype)
    k = _rope(proj[:, COL_K:COL_K + KV_WIDTH], cos, slo, shi)
    k_ref[rows, :] = k.astype(k_ref.dtype)
    v_ref[rows, :] = proj[:, COL_V:COL_V + KV_WIDTH].astype(v_ref.dtype)
    qi = _rope(proj[:, COL_QI:COL_QI + IDX_HEADS * IDX_DIM], cos, slo, shi)
    qi_hi = qi.astype(BF16).astype(F32)
    qi_lo = qi - qi_hi
    zeros = jnp.zeros((tm, IDX_DIM), F32)
    pieces = []
    for hd in range(IDX_HEADS):
        cols = slice(hd * IDX_DIM, (hd + 1) * IDX_DIM)
        pieces += [qi_hi[:, cols], qi_hi[:, cols], qi_lo[:, cols], zeros]
    qi_ref[rows, :] = jnp.concatenate(pieces, axis=1).astype(qi_ref.dtype)
    kiw_raw = proj[:, COL_KI:COL_KI + LANES]
    ki = _rope(kiw_raw, cos, slo, shi)[:, 0:IDX_DIM]
    ki_hi = ki.astype(BF16).astype(F32)
    ki_ref[rows, :] = jnp.concatenate([ki_hi, ki - ki_hi, ki_hi, zeros], axis=1).astype(ki_ref.dtype)
    wi_ref[rows, :] = kiw_raw * IDX_W_SCALE
    return u[tm - halo:, :]


def _in_proj(x2d, norm_g, w_in_pad, cos, slo, shi, pool_w_bd, pool_scale, layer, seq_len):
    n_tok, d_model = x2d.shape
    tm = min(PROJ_TILE, seq_len)
    assert seq_len % tm == 0 and n_tok % tm == 0
    halo = 2 * SUBLANES
    assert halo >= POOL_WINDOWS[-1] and tm >= halo
    tok = lambda w: pl.BlockSpec((tm, w), lambda i: (i, 0))
    lay = lambda *s: pl.BlockSpec((None,) + s, lambda i: (layer,) + (0,) * len(s))
    out_shape = (
        jax.ShapeDtypeStruct((n_tok, POOL_WIDTH), BF16),
        jax.ShapeDtypeStruct((n_tok, 4 * HGRN_WIDTH), F32),
        jax.ShapeDtypeStruct((n_tok, ATTN_WIDTH), BF16),
        jax.ShapeDtypeStruct((n_tok, KV_WIDTH), BF16),
        jax.ShapeDtypeStruct((n_tok, KV_WIDTH), BF16),
        jax.ShapeDtypeStruct((n_tok, IDX_HEADS * IDX_CAT), BF16),
        jax.ShapeDtypeStruct((n_tok, IDX_CAT), BF16),
        jax.ShapeDtypeStruct((n_tok, LANES), F32),
    )
    return pl.pallas_call(
        functools.partial(_in_proj_kernel, tiles_per_seq=seq_len // tm),
        grid=(n_tok // tm,),
        in_specs=[tok(d_model), lay(1, d_model), lay(d_model, D_IN_PAD),
                  tok(LANES), tok(LANES), tok(LANES),
                  lay(POOL_WIDTH, POOL_WIDTH), lay(1, POOL_WIDTH)],
        out_specs=tuple(tok(s.shape[1]) for s in out_shape),
        out_shape=out_shape,
        scratch_shapes=[pltpu.VMEM((halo, POOL_WIDTH), F32)],
        compiler_params=pltpu.CompilerParams(dimension_semantics=("arbitrary",),
                                             vmem_limit_bytes=VMEM_LIMIT_BYTES),
        name="in_proj",
    )(x2d, norm_g, w_in_pad, cos, slo, shi, pool_w_bd, pool_scale)


def _hgrn_kernel(hg_ref, lb_ref, ng_ref, out_ref, state_ref, o_ref):
    tb, width = out_ref.shape
    c = HGRN_CHUNK
    half = c // 2
    n_chunks = tb // c
    heads = width // HGRN_HEAD_DIM

    @pl.when(pl.program_id(1) == 0)
    def _():
        state_ref[...] = jnp.zeros_like(state_ref)

    lb = lb_ref[...]
    q = _silu(hg_ref[:, 0:width])
    sig = jax.nn.sigmoid(hg_ref[:, width:2 * width])
    f = lb + (1.0 - lb) * sig
    log2_f = jnp.log(jnp.maximum(f, LOG_F_FLOOR)) * LOG2_E
    k = (1.0 - lb) * (1.0 - sig)
    v = hg_ref[:, 2 * width:3 * width]

    row_in_chunk = lax.broadcasted_iota(jnp.int32, (tb, width), 0) % c
    b = log2_f
    step = 1
    while step < c:
        b = b + jnp.where(row_in_chunk >= step, pltpu.roll(b, step, 0), 0.0)
        step *= 2

    lane_r = lax.broadcasted_iota(jnp.int32, (width, width), 0) // HGRN_HEAD_DIM
    lane_c = lax.broadcasted_iota(jnp.int32, (width, width), 1) // HGRN_HEAD_DIM
    head_sum = jnp.where(lane_r == lane_c, 1.0, 0.0).astype(BF16)

    def band(q_, k_, b_dst, b_src, v_, row, offsets):
        o = None
        for s in offsets:
            if s == 0:
                p, vs = q_ * k_ * jnp.exp2(b_dst - b_src), v_
            else:
                ks, bs, vs = (pltpu.roll(a, s, 0) for a in (k_, b_src, v_))
                p = jnp.where(row >= s, q_ * ks * jnp.exp2(b_dst - bs), 0.0)
            score = _dot(p.astype(BF16), head_sum)
            o = score * vs if o is None else o + score * vs
        return o

    o_ref[...] = band(q, k, b, b, v, row_in_chunk, range(half))
    first = lambda a: jnp.concatenate([a[ci * c:ci * c + half] for ci in range(n_chunks)], axis=0)
    second = lambda a: jnp.concatenate([a[ci * c + half:(ci + 1) * c] for ci in range(n_chunks)], axis=0)
    row_in_half = lax.broadcasted_iota(jnp.int32, (tb // 2, width), 0) % half
    o_far = band(second(q), first(k), second(b), first(b), first(v), row_in_half, range(half))
    for ci in range(n_chunks):
        o_ref[ci * c + half:(ci + 1) * c, :] += o_far[ci * half:(ci + 1) * half]

    lane_head = lax.broadcasted_iota(jnp.int32, (c, width), 1) // HGRN_HEAD_DIM
    for ci in range(n_chunks):
        rows = slice(ci * c, (ci + 1) * c)
        bc = b[rows]
        b_last = bc[c - 1:c]
        state = state_ref[...]
        q_dec = q[rows] * jnp.exp2(bc)
        q_heads = jnp.concatenate([jnp.where(lane_head == hd, q_dec, 0.0) for hd in range(heads)], axis=0)
        o_heads = _nt_dot(q_heads.astype(BF16), state.astype(BF16))
        o_inter = jnp.where(lane_head == 0, o_heads[0:c], 0.0)
        for hd in range(1, heads):
            o_inter = jnp.where(lane_head == hd, o_heads[hd * c:(hd + 1) * c], o_inter)
        o_ref[rows, :] += o_inter
        k_dec = k[rows] * jnp.exp2(b_last - bc)
        upd = _tn_dot(v[rows].astype(BF16), k_dec.astype(BF16))
        state_ref[...] = state * jnp.exp2(b_last) + upd

    o = o_ref[...]
    sq_hi, sq_lo = _split_bf16(o * o)
    mean_sq = (_dot(sq_hi, head_sum) + _dot(sq_lo, head_sum)) * (1.0 / HGRN_HEAD_DIM)
    gate = _silu(hg_ref[:, 3 * width:4 * width])
    out_ref[...] = (o * lax.rsqrt(mean_sq + RMS_EPS) * ng_ref[...] * gate).astype(out_ref.dtype)


def _hgrn(hg, lower_bound, norm_g, layer, batch, seq_len):
    n_tok = hg.shape[0]
    tb = min(HGRN_TILE, seq_len)
    assert seq_len % tb == 0 and tb % HGRN_CHUNK == 0 and HGRN_CHUNK == 2 * SUBLANES
    steps = seq_len // tb
    lay = pl.BlockSpec((None, 1, HGRN_WIDTH), lambda bi, ti: (layer, 0, 0))
    return pl.pallas_call(
        _hgrn_kernel,
        grid=(batch, steps),
        in_specs=[pl.BlockSpec((tb, 4 * HGRN_WIDTH), lambda bi, ti: (bi * steps + ti, 0)),
                  lay, lay],
        out_specs=pl.BlockSpec((tb, HGRN_WIDTH), lambda bi, ti: (bi * steps + ti, 0)),
        out_shape=jax.ShapeDtypeStruct((n_tok, HGRN_WIDTH), BF16),
        scratch_shapes=[pltpu.VMEM((HGRN_WIDTH, HGRN_WIDTH), F32),
                        pltpu.VMEM((tb, HGRN_WIDTH), F32)],
        compiler_params=pltpu.CompilerParams(dimension_semantics=("arbitrary", "arbitrary"),
                                             vmem_limit_bytes=VMEM_LIMIT_BYTES),
        name="hgrn2",
    )(hg, lower_bound, norm_g)


def _dsa_kernel(q_ref, k_ref, v_ref, qi_ref, ki_ref, wi_ref, out_ref,
                key_ref, hi_ref, lo_ref, thr_ref, need_ref, m_ref, acc_ref, *, top_k):
    qb = q_ref.shape[0]
    kt = key_ref.shape[1]
    i16 = jnp.int16
    rows16 = 2 * SUBLANES
    span = 4 * rows16
    tall = lambda a: jnp.concatenate([a] * (span // rows16), axis=0)
    t0 = pl.program_id(1) * qb
    n_tiles = (t0 + qb + kt - 1) // kt

    key_pos = lax.broadcasted_iota(jnp.int32, (kt, qb), 0)
    query_pos = t0 + lax.broadcasted_iota(jnp.int32, (kt, qb), 1)

    wi_t = wi_ref[...].T
    w_heads = [wi_t[IDX_DIM + h:IDX_DIM + h + 1, :] for h in range(IDX_HEADS)]

    def index_tile(ti, carry):
        base = pl.multiple_of(ti * kt, kt)
        ki_tile = ki_ref[pl.ds(base, kt), :]
        score = jnp.zeros((kt, qb), F32)
        for h in range(IDX_HEADS):
            s = _nt_dot(ki_tile, qi_ref[:, h * IDX_CAT:(h + 1) * IDX_CAT])
            score = score + jnp.maximum(s, 0.0) * w_heads[h]
        score = jnp.where(score == 0.0, 0.0, score)
        score = jnp.where(base + key_pos <= query_pos, score, MASK_VALUE)
        bits = lax.bitcast_convert_type(score, jnp.int32)
        key = bits ^ ((bits >> 31) & 0x7FFFFFFF)
        key_ref[ti] = key
        hi_ref[ti] = (key >> 16).astype(i16)
        lo_ref[ti] = ((key & 0xFFFF) - 2 ** 15).astype(i16)
        return carry

    lax.fori_loop(0, n_tiles, index_tile, 0)

    def count(ref, pred):
        def body(ti, acc):
            keys = ref[ti]
            for r in range(kt // span):
                acc = acc + jnp.where(pred(keys[r * span:(r + 1) * span, :]), i16(1), i16(0))
            return acc
        partial = lax.fori_loop(0, n_tiles, body, jnp.zeros((span, qb), i16))
        return jnp.sum(partial.astype(F32), axis=0, keepdims=True)

    def kth_largest(ref, target):
        def bit(it, thr):
            cand = thr + lax.shift_left(jnp.int32(1), 15 - it)
            cand16 = tall(cand.astype(i16))
            n_ge = count(ref, lambda keys: keys >= cand16)
            return jnp.where(n_ge >= target, cand, thr)
        return lax.fori_loop(0, 16, bit, jnp.full((rows16, qb), -2 ** 15, jnp.int32))

    thr_ref[...] = jnp.full_like(thr_ref, INT_MIN)
    need_ref[...] = jnp.zeros_like(need_ref)

    @pl.when(t0 + qb > top_k)
    def _():
        upper = kth_largest(hi_ref, float(top_k))
        upper16 = tall(upper.astype(i16))
        n_upper_gt = count(hi_ref, lambda keys: keys > upper16)
        upper_kt = jnp.concatenate([upper16] * (kt // span), axis=0)

        def keep_bucket(ti, carry):
            lo_ref[ti] = jnp.where(hi_ref[ti] == upper_kt, lo_ref[ti], i16(-2 ** 15))
            return carry

        lax.fori_loop(0, n_tiles, keep_bucket, 0)
        lower = kth_largest(lo_ref, top_k - n_upper_gt)
        lower16 = tall(lower.astype(i16))
        n_gt = n_upper_gt + count(lo_ref, lambda keys: keys > lower16)
        thr_ref[...] = (lax.shift_left(upper, 16) + (lower + 2 ** 15))[0:SUBLANES]
        need_ref[...] = jnp.broadcast_to(top_k - n_gt, need_ref.shape)

    thr = thr_ref[0:1, :]
    need = need_ref[0:1, :]

    m_ref[...] = jnp.full_like(m_ref, MASK_VALUE)
    acc_ref[...] = jnp.zeros_like(acc_ref)
    earlier = (lax.broadcasted_iota(jnp.int32, (kt, kt), 1)
               < lax.broadcasted_iota(jnp.int32, (kt, kt), 0))
    earlier_bf = jnp.where(earlier, 1.0, 0.0).astype(BF16)
    ones_rows = jnp.ones((SUBLANES, kt), BF16)
    dim_v = lax.broadcasted_iota(jnp.int32, (KV_WIDTH, kt), 0)
    q_stack = [jnp.concatenate([q_ref[:, h * HEAD_DIM:(h + 1) * HEAD_DIM]
                                for h in range(g * KV_GROUP, (g + 1) * KV_GROUP)], axis=0)
               for g in range(N_KV_HEADS)]

    def attend_tile(ti, ties_before):
        base = pl.multiple_of(ti * kt, kt)
        k_tile = k_ref[pl.ds(base, kt), :]
        s_all = [_nt_dot(k_tile[:, g * HEAD_DIM:(g + 1) * HEAD_DIM], q_stack[g])
                 for g in range(N_KV_HEADS)]
        keys = key_ref[ti]
        tie = keys == thr
        tie_bf = jnp.where(tie, 1.0, 0.0).astype(BF16)
        rank = _dot(earlier_bf, tie_bf) + ties_before
        chosen = ((keys > thr) | (tie & (rank < need))) & (base + key_pos <= query_pos)
        bias = jnp.where(chosen, 0.0, MASK_VALUE)
        v_t = v_ref[pl.ds(base, kt), :].astype(F32).T
        for g in range(N_KV_HEADS):
            v_aug = jnp.where(dim_v // HEAD_DIM == g, v_t, 1.0).astype(BF16)
            for hh in range(KV_GROUP):
                h = g * KV_GROUP + hh
                s = s_all[g][:, hh * qb:(hh + 1) * qb] + bias
                m_old = m_ref[h]
                m_new = jnp.maximum(m_old, jnp.max(s, axis=0, keepdims=True))
                p = jnp.exp2(s - m_new)
                acc_ref[h] = jnp.exp2(m_old - m_new) * acc_ref[h] + _dot(v_aug, p.astype(BF16))
                m_ref[h] = m_new
        return ties_before + _dot(ones_rows, tie_bf)[0:1]

    lax.fori_loop(0, n_tiles, attend_tile, jnp.zeros((1, qb), F32))
    for pair in range(N_HEADS // 2):
        outs = []
        for h in (2 * pair, 2 * pair + 1):
            g = h // KV_GROUP
            acc = acc_ref[h]
            other = (1 - g) * HEAD_DIM
            outs.append(acc[g * HEAD_DIM:(g + 1) * HEAD_DIM, :] / acc[other:other + 1, :])
        out_ref[:, pair * LANES:(pair + 1) * LANES] = (
            jnp.concatenate(outs, axis=0).T.astype(out_ref.dtype))


def _dsa(q, k, v, qi, ki, wi, batch, seq_len):
    n_tok = q.shape[0]
    qb = min(Q_TILE, seq_len)
    kt = min(KEY_TILE, seq_len)
    assert seq_len % qb == 0 and seq_len % kt == 0 and kt % qb == 0 and qb % LANES == 0
    assert N_KV_HEADS == 2 and KV_WIDTH == LANES and 2 * HEAD_DIM == LANES
    steps = seq_len // qb
    n_kt = seq_len // kt
    top_k = min(TOPK_MAX, seq_len // 4)
    qtile = lambda w: pl.BlockSpec((qb, w), lambda bi, qi_: (bi * steps + qi_, 0))
    seq = lambda w: pl.BlockSpec((seq_len, w), lambda bi, qi_: (bi, 0))
    return pl.pallas_call(
        functools.partial(_dsa_kernel, top_k=top_k),
        grid=(batch, steps),
        in_specs=[qtile(ATTN_WIDTH), seq(KV_WIDTH), seq(KV_WIDTH),
                  qtile(IDX_HEADS * IDX_CAT), seq(IDX_CAT), qtile(LANES)],
        out_specs=qtile(ATTN_WIDTH),
        out_shape=jax.ShapeDtypeStruct((n_tok, ATTN_WIDTH), BF16),
        scratch_shapes=[pltpu.VMEM((n_kt, kt, qb), jnp.int32),
                        pltpu.VMEM((n_kt, kt, qb), jnp.int16),
                        pltpu.VMEM((n_kt, kt, qb), jnp.int16),
                        pltpu.VMEM((SUBLANES, qb), jnp.int32),
                        pltpu.VMEM((SUBLANES, qb), F32),
                        pltpu.VMEM((N_HEADS, 1, qb), F32),
                        pltpu.VMEM((N_HEADS, KV_WIDTH, qb), F32)],
        compiler_params=pltpu.CompilerParams(dimension_semantics=("arbitrary", "arbitrary"),
                                             vmem_limit_bytes=VMEM_LIMIT_BYTES),
        name="dsa",
    )(q, k, v, qi, ki, wi)


def _out_ffn_kernel(x_ref, yp_ref, yh_ref, ya_ref, wo_hbm, g2_ref, wfi_hbm, wfo_hbm, gf_ref,
                    out_ref, wo_ref, wfi_ref, wfo_ref, *, layer, final_norm):
    @pl.when(pl.program_id(0) == 0)
    def _():
        pltpu.sync_copy(wo_hbm.at[layer], wo_ref)
        pltpu.sync_copy(wfi_hbm.at[layer], wfi_ref)
        pltpu.sync_copy(wfo_hbm.at[layer], wfo_ref)

    x = x_ref[...]
    x = x + _dot(yp_ref[...], wo_ref[0:POOL_WIDTH, :])
    x = x + _dot(yh_ref[...], wo_ref[POOL_WIDTH:POOL_WIDTH + HGRN_WIDTH, :])
    x = x + _dot(ya_ref[...], wo_ref[POOL_WIDTH + HGRN_WIDTH:, :])
    h = _rms_norm(x, g2_ref[...]).astype(BF16)
    a = jnp.maximum(_dot(h, wfi_ref[...]), 0.0)
    x = x + _dot((a * a).astype(BF16), wfo_ref[...])
    if final_norm:
        x = _rms_norm(x, gf_ref[...])
    out_ref[...] = x


def _out_ffn(x2d, y_pool, y_hgrn, y_attn, w_out, norm2_g, w_ff_in, w_ff_out, final_g,
             layer, final_norm, seq_len):
    n_tok, d_model = x2d.shape
    d_ff = w_ff_in.shape[2]
    tm = min(PROJ_TILE, seq_len)
    tok = lambda w: pl.BlockSpec((tm, w), lambda i: (i, 0))
    hbm = pl.BlockSpec(memory_space=pl.ANY)
    return pl.pallas_call(
        functools.partial(_out_ffn_kernel, layer=layer, final_norm=final_norm),
        grid=(n_tok // tm,),
        in_specs=[tok(d_model), tok(POOL_WIDTH), tok(HGRN_WIDTH), tok(ATTN_WIDTH),
                  hbm, pl.BlockSpec((None, 1, d_model), lambda i: (layer, 0, 0)), hbm, hbm,
                  pl.BlockSpec((1, d_model), lambda i: (0, 0))],
        out_specs=tok(d_model),
        out_shape=jax.ShapeDtypeStruct((n_tok, d_model), F32),
        scratch_shapes=[pltpu.VMEM((d_model, d_model), BF16),
                        pltpu.VMEM((d_model, d_ff), BF16),
                        pltpu.VMEM((d_ff, d_model), BF16)],
        compiler_params=pltpu.CompilerParams(dimension_semantics=("arbitrary",),
                                             vmem_limit_bytes=VMEM_LIMIT_BYTES),
        name="out_ffn",
    )(x2d, y_pool, y_hgrn, y_attn, w_out, norm2_g, w_ff_in, w_ff_out, final_g)


def _rope_patterns(positions):
    inv_freq = ROPE_THETA ** (-jnp.arange(0, ROT_DIM, 2, dtype=F32) / ROT_DIM)
    rest = jnp.zeros((HEAD_DIM - ROT_DIM,), F32)
    freq_head = jnp.concatenate([inv_freq, inv_freq, rest])
    lo_head = jnp.concatenate([-jnp.ones((ROT_HALF,), F32), jnp.zeros((ROT_HALF,), F32), rest])
    hi_head = jnp.concatenate([jnp.zeros((ROT_HALF,), F32), jnp.ones((ROT_HALF,), F32), rest])
    two = lambda a: jnp.concatenate([a, a])[None, :]
    ang = positions.astype(F32).reshape(-1, 1) * two(freq_head)
    sin = jnp.sin(ang)
    return jnp.cos(ang), sin * two(lo_head), sin * two(hi_head)


def kernel(x, positions, norm1_g, w_in, pool_w, pool_scale, lb_logits, hgrn_norm_g, w_out,
           norm2_g, w_ff_in, w_ff_out, final_norm_g):
    batch, seq_len, d_model = x.shape
    depth = w_in.shape[0]
    assert w_in.shape[2] == D_IN

    p_lb = jax.nn.softmax(lb_logits.astype(F32), axis=0)
    lower_bounds = (jnp.cumsum(p_lb, axis=0) - p_lb[0])[:, None, :]
    w_in_pad = jnp.pad(w_in, ((0, 0), (0, 0), (0, D_IN_PAD - D_IN))).astype(BF16)
    groups = len(POOL_WINDOWS)
    eye = jnp.eye(groups, dtype=F32)
    pool_w_bd = (pool_w[:, :, :, None, :] * eye[None, :, None, :, None]).reshape(
        depth, POOL_WIDTH, POOL_WIDTH).astype(BF16)
    w_out_b, w_ff_in_b, w_ff_out_b = (w.astype(BF16) for w in (w_out, w_ff_in, w_ff_out))
    row = lambda a: a.reshape(depth, 1, -1).astype(F32)
    norm1, norm2, pscale, hnorm = row(norm1_g), row(norm2_g), row(pool_scale), row(hgrn_norm_g)
    final_g = final_norm_g.reshape(1, d_model).astype(F32)
    cos, slo, shi = _rope_patterns(positions)

    x2d = x.reshape(batch * seq_len, d_model)
    for layer in range(depth):
        y_pool, hg, q, k, v, qi, ki, wi = _in_proj(x2d, norm1, w_in_pad, cos, slo, shi, pool_w_bd,
                                                pscale, layer, seq_len)
        y_hgrn = _hgrn(hg, lower_bounds, hnorm, layer, batch, seq_len)
        y_attn = _dsa(q, k, v, qi, ki, wi, batch, seq_len)
        x2d = _out_ffn(x2d, y_pool, y_hgrn, y_attn, w_out_b, norm2, w_ff_in_b, w_ff_out_b,
                       final_g, layer, layer == depth - 1, seq_len)
    return x2d.reshape(batch, seq_len, d_model)
```

```python
import functools

import jax
import jax.numpy as jnp
from jax import lax
from jax.experimental import pallas as pl
from jax.experimental.pallas import tpu as pltpu

F32 = jnp.float32
BF16 = jnp.bfloat16

POOL_WINDOWS = (2, 4, 8, 16)
POOL_WIDTH = 256
POOL_GROUP_DIM = 64
HGRN_WIDTH = 256
HGRN_HEAD_DIM = 64
LOG_F_FLOOR = 1e-30
ATTN_WIDTH = 512
HEAD_DIM = 64
N_HEADS = 8
N_KV_HEADS = 2
KV_GROUP = N_HEADS // N_KV_HEADS
KV_WIDTH = N_KV_HEADS * HEAD_DIM
IDX_HEADS = 4
IDX_DIM = 64
TOPK_MAX = 256
MASK_VALUE = -1e30
ROPE_THETA = 500000.0
ROT_DIM = HEAD_DIM // 4
ROT_HALF = ROT_DIM // 2
RMS_EPS = 1e-5
IDX_W_SCALE = (IDX_HEADS ** -0.5) * (IDX_DIM ** -0.5)
ATTN_SCALE = HEAD_DIM ** -0.5
LOG2_E = 1.4426950408889634

LANES = 128
SUBLANES = 8
VMEM_LIMIT_BYTES = 56 * 1024 * 1024

COL_POOL = 0
COL_HGRN = COL_POOL + POOL_WIDTH
COL_Q = COL_HGRN + 4 * HGRN_WIDTH
COL_K = COL_Q + ATTN_WIDTH
COL_V = COL_K + KV_WIDTH
COL_QI = COL_V + KV_WIDTH
COL_KI = COL_QI + IDX_HEADS * IDX_DIM
D_IN = COL_KI + IDX_DIM + IDX_HEADS
D_IN_PAD = COL_KI + LANES
IDX_CAT = 4 * IDX_DIM

PROJ_TILE = 512
PROJ_SUB = 128
HGRN_TILE = 256
HGRN_CHUNK = 16
Q_TILE = 256
KEY_TILE = 256
INT_MIN = -2 ** 31


def _nt_dot(a, b):
    return lax.dot_general(a, b, (((1,), (1,)), ((), ())), preferred_element_type=F32)


def _tn_dot(a, b):
    return lax.dot_general(a, b, (((0,), (0,)), ((), ())), preferred_element_type=F32)


def _dot(a, b):
    return jnp.dot(a, b, preferred_element_type=F32)


def _split_bf16(a):
    hi = a.astype(BF16)
    lo = (a - hi.astype(F32)).astype(BF16)
    return hi, lo


def _rms_norm(x, g):
    return x * lax.rsqrt(jnp.mean(x * x, axis=-1, keepdims=True) + RMS_EPS) * g


def _silu(x):
    return x * jax.nn.sigmoid(x)


def _rope(x, cos, sin_lo, sin_hi):
    w = x.shape[1]
    rep = w // LANES
    if rep > 1:
        cos = jnp.concatenate([cos] * rep, axis=1)
        sin_lo = jnp.concatenate([sin_lo] * rep, axis=1)
        sin_hi = jnp.concatenate([sin_hi] * rep, axis=1)
    from_below = pltpu.roll(x, ROT_HALF, 1)
    from_above = pltpu.roll(x, w - ROT_HALF, 1)
    return x * cos + from_below * sin_hi + from_above * sin_lo


def _in_proj_kernel(x_ref, g_ref, w_ref, cos_ref, slo_ref, shi_ref, pw_ref, pscale_ref,
                    ypool_ref, hg_ref, q_ref, k_ref, v_ref, qi_ref, ki_ref, wi_ref,
                    halo_ref, *, tiles_per_seq):
    tm = x_ref.shape[0]
    sub = min(PROJ_SUB, tm)
    seq_tile = pl.program_id(0) % tiles_per_seq

    @pl.when(seq_tile == 0)
    def _():
        halo_ref[...] = jnp.zeros_like(halo_ref)

    prev = halo_ref[...]
    for r in range(tm // sub):
        prev = _in_proj_rows(slice(r * sub, (r + 1) * sub), seq_tile * tm + r * sub, prev,
                             x_ref, g_ref, w_ref, cos_ref, slo_ref, shi_ref, pw_ref, pscale_ref,
                             ypool_ref, hg_ref, q_ref, k_ref, v_ref, qi_ref, ki_ref, wi_ref)
    halo_ref[...] = prev


def _in_proj_rows(rows, t_start, prev, x_ref, g_ref, w_ref, cos_ref, slo_ref, shi_ref, pw_ref,
                  pscale_ref, ypool_ref, hg_ref, q_ref, k_ref, v_ref, qi_ref, ki_ref, wi_ref):
    tm = rows.stop - rows.start
    halo = prev.shape[0]
    h = _rms_norm(x_ref[rows, :], g_ref[...])
    proj = _dot(h.astype(BF16), w_ref[...])

    u = proj[:, COL_POOL:COL_POOL + POOL_WIDTH]
    ext = jnp.concatenate([prev, u], axis=0)
    sums = []
    acc = ext
    span = 1
    for win in POOL_WINDOWS:
        while span < win:
            acc = acc + pltpu.roll(acc, span, 0)
            span *= 2
        sums.append(acc[halo:, :])
    lane = lax.broadcasted_iota(jnp.int32, (tm, POOL_WIDTH), 1)
    group = lane // POOL_GROUP_DIM
    wsum = sums[-1]
    win_lane = jnp.full((tm, POOL_WIDTH), POOL_WINDOWS[-1], jnp.int32)
    for gi in range(len(POOL_WINDOWS) - 2, -1, -1):
        wsum = jnp.where(group == gi, sums[gi], wsum)
        win_lane = jnp.where(group == gi, POOL_WINDOWS[gi], win_lane)
    t_pos = t_start + lax.broadcasted_iota(jnp.int32, (tm, POOL_WIDTH), 0)
    count = jnp.minimum(t_pos + 1, win_lane).astype(F32)
    pooled = wsum / count - u
    y_pool = _dot(pooled.astype(BF16), pw_ref[...]) * pscale_ref[...]
    ypool_ref[rows, :] = y_pool.astype(ypool_ref.dtype)

    hg_ref[rows, :] = proj[:, COL_HGRN:COL_HGRN + 4 * HGRN_WIDTH]

    cos, slo, shi = cos_ref[rows, :], slo_ref[rows, :], shi_ref[rows, :]
    q = _rope(proj[:, COL_Q:COL_Q + ATTN_WIDTH], cos, slo, shi) * (ATTN_SCALE * LOG2_E)
    q_ref[rows, :] = q.astype(q_ref.dtype)
    k = _rope(proj[:, COL_K:COL_K + KV_WIDTH], cos, slo, shi)
    k_ref[rows, :] = k.astype(k_ref.dtype)
    v_ref[rows, :] = proj[:, COL_V:COL_V + KV_WIDTH].astype(v_ref.dtype)
    qi = _rope(proj[:, COL_QI:COL_QI + IDX_HEADS * IDX_DIM], cos, slo, shi)
    qi_hi = qi.astype(BF16).astype(F32)
    qi_lo = qi - qi_hi
    zeros = jnp.zeros((tm, IDX_DIM), F32)
    pieces = []
    for hd in range(IDX_HEADS):
        cols = slice(hd * IDX_DIM, (hd + 1) * IDX_DIM)
        pieces += [qi_hi[:, cols], qi_hi[:, cols], qi_lo[:, cols], zeros]
    qi_ref[rows, :] = jnp.concatenate(pieces, axis=1).astype(qi_ref.dtype)
    kiw_raw = proj[:, COL_KI:COL_KI + LANES]
    ki = _rope(kiw_raw, cos, slo, shi)[:, 0:IDX_DIM]
    ki_hi = ki.astype(BF16).astype(F32)
    ki_ref[rows, :] = jnp.concatenate([ki_hi, ki - ki_hi, ki_hi, zeros], axis=1).astype(ki_ref.dtype)
    wi_ref[rows, :] = kiw_raw * IDX_W_SCALE
    return u[tm - halo:, :]


def _in_proj(x2d, norm_g, w_in_pad, cos, slo, shi, pool_w_bd, pool_scale, layer, seq_len):
    n_tok, d_model = x2d.shape
    tm = min(PROJ_TILE, seq_len)
    assert seq_len % tm == 0 and n_tok % tm == 0
    halo = 2 * SUBLANES
    assert halo >= POOL_WINDOWS[-1] and tm >= halo
    tok = lambda w: pl.BlockSpec((tm, w), lambda i: (i, 0))
    lay = lambda *s: pl.BlockSpec((None,) + s, lambda i: (layer,) + (0,) * len(s))
    out_shape = (
        jax.ShapeDtypeStruct((n_tok, POOL_WIDTH), BF16),
        jax.ShapeDtypeStruct((n_tok, 4 * HGRN_WIDTH), F32),
        jax.ShapeDtypeStruct((n_tok, ATTN_WIDTH), BF16),
        jax.ShapeDtypeStruct((n_tok, KV_WIDTH), BF16),
        jax.ShapeDtypeStruct((n_tok, KV_WIDTH), BF16),
        jax.ShapeDtypeStruct((n_tok, IDX_HEADS * IDX_CAT), BF16),
        jax.ShapeDtypeStruct((n_tok, IDX_CAT), BF16),
        jax.ShapeDtypeStruct((n_tok, LANES), F32),
    )
    return pl.pallas_call(
        functools.partial(_in_proj_kernel, tiles_per_seq=seq_len // tm),
        grid=(n_tok // tm,),
        in_specs=[tok(d_model), lay(1, d_model), lay(d_model, D_IN_PAD),
                  tok(LANES), tok(LANES), tok(LANES),
                  lay(POOL_WIDTH, POOL_WIDTH), lay(1, POOL_WIDTH)],
        out_specs=tuple(tok(s.shape[1]) for s in out_shape),
        out_shape=out_shape,
        scratch_shapes=[pltpu.VMEM((halo, POOL_WIDTH), F32)],
        compiler_params=pltpu.CompilerParams(dimension_semantics=("arbitrary",),
                                             vmem_limit_bytes=VMEM_LIMIT_BYTES),
        name="in_proj",
    )(x2d, norm_g, w_in_pad, cos, slo, shi, pool_w_bd, pool_scale)


def _hgrn_kernel(hg_ref, lb_ref, ng_ref, out_ref, state_ref, o_ref):
    tb, width = out_ref.shape
    c = HGRN_CHUNK
    half = c // 2
    n_chunks = tb // c
    heads = width // HGRN_HEAD_DIM

    @pl.when(pl.program_id(1) == 0)
    def _():
        state_ref[...] = jnp.zeros_like(state_ref)

    lb = lb_ref[...]
    q = _silu(hg_ref[:, 0:width])
    sig = jax.nn.sigmoid(hg_ref[:, width:2 * width])
    f = lb + (1.0 - lb) * sig
    log2_f = jnp.log(jnp.maximum(f, LOG_F_FLOOR)) * LOG2_E
    k = (1.0 - lb) * (1.0 - sig)
    v = hg_ref[:, 2 * width:3 * width]

    row_in_chunk = lax.broadcasted_iota(jnp.int32, (tb, width), 0) % c
    b = log2_f
    step = 1
    while step < c:
        b = b + jnp.where(row_in_chunk >= step, pltpu.roll(b, step, 0), 0.0)
        step *= 2

    lane_r = lax.broadcasted_iota(jnp.int32, (width, width), 0) // HGRN_HEAD_DIM
    lane_c = lax.broadcasted_iota(jnp.int32, (width, width), 1) // HGRN_HEAD_DIM
    head_sum = jnp.where(lane_r == lane_c, 1.0, 0.0).astype(BF16)

    def band(q_, k_, b_dst, b_src, v_, row, offsets):
        o = None
        for s in offsets:
            if s == 0:
                p, vs = q_ * k_ * jnp.exp2(b_dst - b_src), v_
            else:
                ks, bs, vs = (pltpu.roll(a, s, 0) for a in (k_, b_src, v_))
                p = jnp.where(row >= s, q_ * ks * jnp.exp2(b_dst - bs), 0.0)
            score = _dot(p.astype(BF16), head_sum)
            o = score * vs if o is None else o + score * vs
        return o

    o_ref[...] = band(q, k, b, b, v, row_in_chunk, range(half))
    first = lambda a: jnp.concatenate([a[ci * c:ci * c + half] for ci in range(n_chunks)], axis=0)
    second = lambda a: jnp.concatenate([a[ci * c + half:(ci + 1) * c] for ci in range(n_chunks)], axis=0)
    row_in_half = lax.broadcasted_iota(jnp.int32, (tb // 2, width), 0) % half
    o_far = band(second(q), first(k), second(b), first(b), first(v), row_in_half, range(half))
    for ci in range(n_chunks):
        o_ref[ci * c + half:(ci + 1) * c, :] += o_far[ci * half:(ci + 1) * half]

    lane_head = lax.broadcasted_iota(jnp.int32, (c, width), 1) // HGRN_HEAD_DIM
    for ci in range(n_chunks):
        rows = slice(ci * c, (ci + 1) * c)
        bc = b[rows]
        b_last = bc[c - 1:c]
        state = state_ref[...]
        q_dec = q[rows] * jnp.exp2(bc)
        q_heads = jnp.concatenate([jnp.where(lane_head == hd, q_dec, 0.0) for hd in range(heads)], axis=0)
        o_heads = _nt_dot(q_heads.astype(BF16), state.astype(BF16))
        o_inter = jnp.where(lane_head == 0, o_heads[0:c], 0.0)
        for hd in range(1, heads):
            o_inter = jnp.where(lane_head == hd, o_heads[hd * c:(hd + 1) * c], o_inter)
        o_ref[rows, :] += o_inter
        k_dec = k[rows] * jnp.exp2(b_last - bc)
        upd = _tn_dot(v[rows].astype(BF16), k_dec.astype(BF16))
        state_ref[...] = state * jnp.exp2(b_last) + upd

    o = o_ref[...]
    sq_hi, sq_lo = _split_bf16(o * o)
    mean_sq = (_dot(sq_hi, head_sum) + _dot(sq_lo, head_sum)) * (1.0 / HGRN_HEAD_DIM)
    gate = _silu(hg_ref[:, 3 * width:4 * width])
    out_ref[...] = (o * lax.rsqrt(mean_sq + RMS_EPS) * ng_ref[...] * gate).astype(out_ref.dtype)


def _hgrn(hg, lower_bound, norm_g, layer, batch, seq_len):
    n_tok = hg.shape[0]
    tb = min(HGRN_TILE, seq_len)
    assert seq_len % tb == 0 and tb % HGRN_CHUNK == 0 and HGRN_CHUNK == 2 * SUBLANES
    steps = seq_len // tb
    lay = pl.BlockSpec((None, 1, HGRN_WIDTH), lambda bi, ti: (layer, 0, 0))
    return pl.pallas_call(
        _hgrn_kernel,
        grid=(batch, steps),
        in_specs=[pl.BlockSpec((tb, 4 * HGRN_WIDTH), lambda bi, ti: (bi * steps + ti, 0)),
                  lay, lay],
        out_specs=pl.BlockSpec((tb, HGRN_WIDTH), lambda bi, ti: (bi * steps + ti, 0)),
        out_shape=jax.ShapeDtypeStruct((n_tok, HGRN_WIDTH), BF16),
        scratch_shapes=[pltpu.VMEM((HGRN_WIDTH, HGRN_WIDTH), F32),
                        pltpu.VMEM((tb, HGRN_WIDTH), F32)],
        compiler_params=pltpu.CompilerParams(dimension_semantics=("arbitrary", "arbitrary"),
                                             vmem_limit_bytes=VMEM_LIMIT_BYTES),
        name="hgrn2",
    )(hg, lower_bound, norm_g)


def _dsa_kernel(q_ref, k_ref, v_ref, qi_ref, ki_ref, wi_ref, out_ref,
                key_ref, hi_ref, lo_ref, thr_ref, need_ref, m_ref, acc_ref, *, top_k):
    qb = q_ref.shape[0]
    kt = key_ref.shape[1]
    i16 = jnp.int16
    rows16 = 2 * SUBLANES
    span = 4 * rows16
    tall = lambda a: jnp.concatenate([a] * (span // rows16), axis=0)
    t0 = pl.program_id(1) * qb
    n_tiles = (t0 + qb + kt - 1) // kt
    n_whole = (t0 + 1) // kt

    key_pos = lax.broadcasted_iota(jnp.int32, (kt, qb), 0)
    query_pos = t0 + lax.broadcasted_iota(jnp.int32, (kt, qb), 1)

    wi_t = wi_ref[...].T
    w_heads = [wi_t[IDX_DIM + h:IDX_DIM + h + 1, :] for h in range(IDX_HEADS)]

    def index_tile(ti, carry, causal_cut):
        base = pl.multiple_of(ti * kt, kt)
        ki_tile = ki_ref[pl.ds(base, kt), :]
        score = jnp.zeros((kt, qb), F32)
        for h in range(IDX_HEADS):
            s = _nt_dot(ki_tile, qi_ref[:, h * IDX_CAT:(h + 1) * IDX_CAT])
            score = score + jnp.maximum(s, 0.0) * w_heads[h]
        score = jnp.where(score == 0.0, 0.0, score)
        if causal_cut:
            score = jnp.where(base + key_pos <= query_pos, score, MASK_VALUE)
        bits =lax.bitcast_convert_type(score, jnp.int32)
        key = bits ^ ((bits >> 31) & 0x7FFFFFFF)
        key_ref[ti] = key
        hi_ref[ti] = (key >> 16).astype(i16)
        lo_ref[ti] = ((key & 0xFFFF) - 2 ** 15).astype(i16)
        return carry

    lax.fori_loop(0, n_whole, functools.partial(index_tile, causal_cut=False), 0)
    lax.fori_loop(n_whole, n_tiles, functools.partial(index_tile, causal_cut=True), 0)

    def count(ref, pred):
        def body(ti, acc):
            keys = ref[ti]
            for r in range(kt // span):
                acc = acc + jnp.where(pred(keys[r * span:(r + 1) * span, :]), i16(1), i16(0))
            return acc
        partial = lax.fori_loop(0, n_tiles, body, jnp.zeros((span, qb), i16))
        return jnp.sum(partial.astype(F32), axis=0, keepdims=True)

    def kth_largest(ref, target):
        def bit(it, state):
            thr, n_above = state
            cand = thr + lax.shift_left(jnp.int32(1), 15 - it)
            cand16 = tall(cand.astype(i16))
            n_ge = jnp.broadcast_to(count(ref, lambda keys: keys >= cand16), thr.shape)
            ok = n_ge >= target
            return jnp.where(ok, cand, thr), jnp.where(ok, n_above, n_ge)
        return lax.fori_loop(0, 16, bit, (jnp.full((rows16, qb), -2 ** 15, jnp.int32),
                                          jnp.zeros((rows16, qb), F32)))

    thr_ref[...] = jnp.full_like(thr_ref, INT_MIN)
    need_ref[...] = jnp.zeros_like(need_ref)

    @pl.when(t0 + qb > top_k)
    def _():
        upper, n_upper_gt = kth_largest(hi_ref, float(top_k))
        upper_kt = jnp.concatenate([tall(upper.astype(i16))] * (kt // span), axis=0)

        def keep_bucket(ti, carry):
            lo_ref[ti] = jnp.where(hi_ref[ti] == upper_kt, lo_ref[ti], i16(-2 ** 15))
            return carry

        lax.fori_loop(0, n_tiles, keep_bucket, 0)
        lower, n_lower_gt = kth_largest(lo_ref, top_k - n_upper_gt)
        thr_ref[...] = (lax.shift_left(upper, 16) + (lower + 2 ** 15))[0:SUBLANES]
        need_ref[...] = (top_k - (n_upper_gt + n_lower_gt))[0:SUBLANES]

    thr = thr_ref[0:1, :]
    need = need_ref[0:1, :]

    m_ref[...] = jnp.full_like(m_ref, MASK_VALUE)
    acc_ref[...] = jnp.zeros_like(acc_ref)
    earlier = (lax.broadcasted_iota(jnp.int32, (kt, kt), 1)
               < lax.broadcasted_iota(jnp.int32, (kt, kt), 0))
    earlier_bf = jnp.where(earlier, 1.0, 0.0).astype(BF16)
    ones_rows = jnp.ones((SUBLANES, kt), BF16)
    dim_v = lax.broadcasted_iota(jnp.int32, (KV_WIDTH, kt), 0)
    q_stack = [jnp.concatenate([q_ref[:, h * HEAD_DIM:(h + 1) * HEAD_DIM]
                                for h in range(g * KV_GROUP, (g + 1) * KV_GROUP)], axis=0)
               for g in range(N_KV_HEADS)]

    def attend_tile(ti, ties_before, causal_cut):
        base = pl.multiple_of(ti * kt, kt)
        k_tile = k_ref[pl.ds(base, kt), :]
        s_all = [_nt_dot(k_tile[:, g * HEAD_DIM:(g + 1) * HEAD_DIM], q_stack[g])
                 for g in range(N_KV_HEADS)]
        keys = key_ref[ti]
        tie = keys == thr
        tie_bf = jnp.where(tie, 1.0, 0.0).astype(BF16)
        rank = _dot(earlier_bf, tie_bf) + ties_before
        chosen = (keys > thr) | (tie & (rank < need))
        if causal_cut:
            chosen = chosen & (base + key_pos <= query_pos)
        bias = jnp.where(chosen, 0.0, MASK_VALUE)
        v_t = v_ref[pl.ds(base, kt), :].astype(F32).T
        for g in range(N_KV_HEADS):
            v_aug = jnp.where(dim_v // HEAD_DIM == g, v_t, 1.0).astype(BF16)
            for hh in range(KV_GROUP):
                h = g * KV_GROUP + hh
                s = s_all[g][:, hh * qb:(hh + 1) * qb] + bias
                m_old = m_ref[h]
                m_new = jnp.maximum(m_old, jnp.max(s, axis=0, keepdims=True))
                p = jnp.exp2(s - m_new)
                acc_ref[h] = jnp.exp2(m_old - m_new) * acc_ref[h] + _dot(v_aug, p.astype(BF16))
                m_ref[h] = m_new
        return ties_before + _dot(ones_rows, tie_bf)[0:1]

    ties = lax.fori_loop(0, n_whole, functools.partial(attend_tile, causal_cut=False),
                         jnp.zeros((1, qb), F32))
    lax.fori_loop(n_whole, n_tiles, functools.partial(attend_tile, causal_cut=True), ties)
    for pair in range(N_HEADS // 2):
        outs = []
        for h in (2 * pair, 2 * pair + 1):
            g = h // KV_GROUP
            acc = acc_ref[h]
            other = (1 - g) * HEAD_DIM
            outs.append(acc[g * HEAD_DIM:(g + 1) * HEAD_DIM, :] / acc[other:other + 1, :])
        out_ref[:, pair * LANES:(pair + 1) * LANES] = (
            jnp.concatenate(outs, axis=0).T.astype(out_ref.dtype))


def _dsa(q, k, v, qi, ki, wi, batch, seq_len):
    n_tok = q.shape[0]
    qb = min(Q_TILE, seq_len)
    kt = min(KEY_TILE, seq_len)
    assert seq_len % qb == 0 and seq_len % kt == 0 and kt % qb == 0 and qb % LANES == 0
    assert N_KV_HEADS == 2 and KV_WIDTH == LANES and 2 * HEAD_DIM == LANES
    steps = seq_len // qb
    n_kt = seq_len // kt
    top_k = min(TOPK_MAX, seq_len // 4)
    qtile = lambda w: pl.BlockSpec((qb, w), lambda bi, qi_: (bi * steps + qi_, 0))
    seq = lambda w: pl.BlockSpec((seq_len, w), lambda bi, qi_: (bi, 0))
    return pl.pallas_call(
        functools.partial(_dsa_kernel, top_k=top_k),
        grid=(batch, steps),
        in_specs=[qtile(ATTN_WIDTH), seq(KV_WIDTH), seq(KV_WIDTH),
                  qtile(IDX_HEADS * IDX_CAT), seq(IDX_CAT), qtile(LANES)],
        out_specs=qtile(ATTN_WIDTH),
        out_shape=jax.ShapeDtypeStruct((n_tok, ATTN_WIDTH), BF16),
        scratch_shapes=[pltpu.VMEM((n_kt, kt, qb), jnp.int32),
                        pltpu.VMEM((n_kt, kt, qb), jnp.int16),
                        pltpu.VMEM((n_kt, kt, qb), jnp.int16),
                        pltpu.VMEM((SUBLANES, qb), jnp.int32),
                        pltpu.VMEM((SUBLANES, qb), F32),
                        pltpu.VMEM((N_HEADS, 1, qb), F32),
                        pltpu.VMEM((N_HEADS, KV_WIDTH, qb), F32)],
        compiler_params=pltpu.CompilerParams(dimension_semantics=("arbitrary", "arbitrary"),
                                             vmem_limit_bytes=VMEM_LIMIT_BYTES),
        name="dsa",
    )(q, k, v, qi, ki, wi)


def _out_ffn_kernel(x_ref, yp_ref, yh_ref, ya_ref, wo_hbm, g2_ref, wfi_hbm, wfo_hbm, gf_ref,
                    out_ref, wo_ref, wfi_ref, wfo_ref, *, layer, final_norm):
    @pl.when(pl.program_id(0) == 0)
    def _():
        pltpu.sync_copy(wo_hbm.at[layer], wo_ref)
        pltpu.sync_copy(wfi_hbm.at[layer], wfi_ref)
        pltpu.sync_copy(wfo_hbm.at[layer], wfo_ref)

    x = x_ref[...]
    x = x + _dot(yp_ref[...], wo_ref[0:POOL_WIDTH, :])
    x = x + _dot(yh_ref[...], wo_ref[POOL_WIDTH:POOL_WIDTH + HGRN_WIDTH, :])
    x = x + _dot(ya_ref[...], wo_ref[POOL_WIDTH + HGRN_WIDTH:, :])
    h = _rms_norm(x, g2_ref[...]).astype(BF16)
    a = jnp.maximum(_dot(h, wfi_ref[...]), 0.0)
    x = x + _dot((a * a).astype(BF16), wfo_ref[...])
    if final_norm:
        x = _rms_norm(x, gf_ref[...])
    out_ref[...] = x


def _out_ffn(x2d, y_pool, y_hgrn, y_attn, w_out, norm2_g, w_ff_in, w_ff_out, final_g,
             layer, final_norm, seq_len):
    n_tok, d_model = x2d.shape
    d_ff = w_ff_in.shape[2]
    tm = min(PROJ_TILE, seq_len)
    tok = lambda w: pl.BlockSpec((tm, w), lambda i: (i, 0))
    hbm = pl.BlockSpec(memory_space=pl.ANY)
    return pl.pallas_call(
        functools.partial(_out_ffn_kernel, layer=layer, final_norm=final_norm),
        grid=(n_tok // tm,),
        in_specs=[tok(d_model), tok(POOL_WIDTH), tok(HGRN_WIDTH), tok(ATTN_WIDTH),
                  hbm, pl.BlockSpec((None, 1, d_model), lambda i: (layer, 0, 0)), hbm, hbm,
                  pl.BlockSpec((1, d_model), lambda i: (0, 0))],
        out_specs=tok(d_model),
        out_shape=jax.ShapeDtypeStruct((n_tok, d_model), F32),
        scratch_shapes=[pltpu.VMEM((d_model, d_model), BF16),
                        pltpu.VMEM((d_model, d_ff), BF16),
                        pltpu.VMEM((d_ff, d_model), BF16)],
        compiler_params=pltpu.CompilerParams(dimension_semantics=("arbitrary",),
                                             vmem_limit_bytes=VMEM_LIMIT_BYTES),
        name="out_ffn",
    )(x2d, y_pool, y_hgrn, y_attn, w_out, norm2_g, w_ff_in, w_ff_out, final_g)


def _rope_patterns(positions):
    inv_freq = ROPE_THETA ** (-jnp.arange(0, ROT_DIM, 2, dtype=F32) / ROT_DIM)
    rest = jnp.zeros((HEAD_DIM - ROT_DIM,), F32)
    freq_head = jnp.concatenate([inv_freq, inv_freq, rest])
    lo_head = jnp.concatenate([-jnp.ones((ROT_HALF,), F32), jnp.zeros((ROT_HALF,), F32), rest])
    hi_head = jnp.concatenate([jnp.zeros((ROT_HALF,), F32), jnp.ones((ROT_HALF,), F32), rest])
    two = lambda a: jnp.concatenate([a, a])[None, :]
    ang = positions.astype(F32).reshape(-1, 1) * two(freq_head)
    sin = jnp.sin(ang)
    return jnp.cos(ang), sin * two(lo_head), sin * two(hi_head)


def kernel(x, positions, norm1_g, w_in, pool_w, pool_scale, lb_logits, hgrn_norm_g, w_out,
           norm2_g, w_ff_in, w_ff_out, final_norm_g):
    batch, seq_len, d_model = x.shape
    depth = w_in.shape[0]
    assert w_in.shape[2] == D_IN

    p_lb = jax.nn.softmax(lb_logits.astype(F32), axis=0)
    lower_bounds = (jnp.cumsum(p_lb, axis=0) - p_lb[0])[:, None, :]
    w_in_pad = jnp.pad(w_in, ((0, 0), (0, 0), (0, D_IN_PAD - D_IN))).astype(BF16)
    groups = len(POOL_WINDOWS)
    eye = jnp.eye(groups, dtype=F32)
    pool_w_bd = (pool_w[:, :, :, None, :] * eye[None, :, None, :, None]).reshape(
        depth, POOL_WIDTH, POOL_WIDTH).astype(BF16)
    w_out_b, w_ff_in_b, w_ff_out_b = (w.astype(BF16) for w in (w_out, w_ff_in, w_ff_out))
    row = lambda a: a.reshape(depth, 1, -1).astype(F32)
    norm1, norm2, pscale, hnorm = row(norm1_g), row(norm2_g), row(pool_scale), row(hgrn_norm_g)
    final_g = final_norm_g.reshape(1, d_model).astype(F32)
    cos, slo, shi = _rope_patterns(positions)

    x2d = x.reshape(batch * seq_len, d_model)
    for layer in range(depth):
        y_pool, hg, q, k, v, qi, ki, wi = _in_proj(x2d, norm1, w_in_pad, cos, slo, shi, pool_w_bd,
                                                pscale, layer, seq_len)
        y_hgrn = _hgrn(hg, lower_bounds, hnorm, layer, batch, seq_len)
        y_attn = _dsa(q, k, v, qi, ki, wi, batch, seq_len)
        x2d = _out_ffn(x2d, y_pool, y_hgrn, y_attn, w_out_b, norm2, w_ff_in_b, w_ff_out_b,
                       final_g, layer, layer == depth - 1, seq_len)
    return x2d.reshape(batch, seq_len, d_model)
```

```python
import functools

import jax
import jax.numpy as jnp
from jax import lax
from jax.experimental import pallas as pl
from jax.experimental.pallas import tpu as pltpu

F32 = jnp.float32
BF16 = jnp.bfloat16

POOL_WINDOWS = (2, 4, 8, 16)
POOL_WIDTH = 256
POOL_GROUP_DIM = 64
HGRN_WIDTH = 256
HGRN_HEAD_DIM = 64
LOG_F_FLOOR = 1e-30
ATTN_WIDTH = 512
HEAD_DIM = 64
N_HEADS = 8
N_KV_HEADS = 2
KV_GROUP = N_HEADS // N_KV_HEADS
KV_WIDTH = N_KV_HEADS * HEAD_DIM
IDX_HEADS = 4
IDX_DIM = 64
TOPK_MAX = 256
MASK_VALUE = -1e30
ROPE_THETA = 500000.0
ROT_DIM = HEAD_DIM // 4
ROT_HALF = ROT_DIM // 2
RMS_EPS = 1e-5
IDX_W_SCALE = (IDX_HEADS ** -0.5) * (IDX_DIM ** -0.5)
ATTN_SCALE = HEAD_DIM ** -0.5
LOG2_E = 1.4426950408889634

LANES = 128
SUBLANES = 8
VMEM_LIMIT_BYTES = 56 * 1024 * 1024

COL_POOL = 0
COL_HGRN = COL_POOL + POOL_WIDTH
COL_Q = COL_HGRN + 4 * HGRN_WIDTH
COL_K = COL_Q + ATTN_WIDTH
COL_V = COL_K + KV_WIDTH
COL_QI = COL_V + KV_WIDTH
COL_KI = COL_QI + IDX_HEADS * IDX_DIM
D_IN = COL_KI + IDX_DIM + IDX_HEADS
D_IN_PAD = COL_KI + LANES
IDX_CAT = 4 * IDX_DIM

PROJ_TILE = 512
PROJ_SUB = 128
HGRN_TILE = 256
HGRN_CHUNK = 16
Q_TILE = 256
KEY_TILE = 256
INT_MIN = -2 ** 31


def _nt_dot(a, b):
    return lax.dot_general(a, b, (((1,), (1,)), ((), ())), preferred_element_type=F32)


def _tn_dot(a, b):
    return lax.dot_general(a, b, (((0,), (0,)), ((), ())), preferred_element_type=F32)


def _dot(a, b):
    return jnp.dot(a, b, preferred_element_type=F32)


def _split_bf16(a):
    hi = a.astype(BF16)
    lo = (a - hi.astype(F32)).astype(BF16)
    return hi, lo


def _rms_norm(x, g):
    return x * lax.rsqrt(jnp.mean(x * x, axis=-1, keepdims=True) + RMS_EPS) * g


def _silu(x):
    return x * jax.nn.sigmoid(x)


def _rope(x, cos, sin_lo, sin_hi):
    w = x.shape[1]
    rep = w // LANES
    if rep > 1:
        cos = jnp.concatenate([cos] * rep, axis=1)
        sin_lo = jnp.concatenate([sin_lo] * rep, axis=1)
        sin_hi = jnp.concatenate([sin_hi] * rep, axis=1)
    from_below = pltpu.roll(x, ROT_HALF, 1)
    from_above = pltpu.roll(x, w - ROT_HALF, 1)
    return x * cos + from_below * sin_hi + from_above * sin_lo


def _in_proj_kernel(x_ref, g_ref, w_ref, cos_ref, slo_ref, shi_ref, pw_ref, pscale_ref,
                    ypool_ref, hg_ref, q_ref, k_ref, v_ref, qi_ref, ki_ref, wi_ref,
                    halo_ref, *, tiles_per_seq):
    tm = x_ref.shape[0]
    sub = min(PROJ_SUB, tm)
    seq_tile = pl.program_id(0) % tiles_per_seq

    @pl.when(seq_tile == 0)
    def _():
        halo_ref[...] = jnp.zeros_like(halo_ref)

    prev = halo_ref[...]
    for r in range(tm // sub):
        prev = _in_proj_rows(slice(r * sub, (r + 1) * sub), seq_tile * tm + r * sub, prev,
                             x_ref, g_ref, w_ref, cos_ref, slo_ref, shi_ref, pw_ref, pscale_ref,
                             ypool_ref, hg_ref, q_ref, k_ref, v_ref, qi_ref, ki_ref, wi_ref)
    halo_ref[...] = prev


def _in_proj_rows(rows, t_start, prev, x_ref, g_ref, w_ref, cos_ref, slo_ref, shi_ref, pw_ref,
                  pscale_ref, ypool_ref, hg_ref, q_ref, k_ref, v_ref, qi_ref, ki_ref, wi_ref):
    tm = rows.stop - rows.start
    halo = prev.shape[0]
    h = _rms_norm(x_ref[rows, :], g_ref[...])
    proj = _dot(h.astype(BF16), w_ref[...])

    u = proj[:, COL_POOL:COL_POOL + POOL_WIDTH]
    ext = jnp.concatenate([prev, u], axis=0)
    sums = []
    acc = ext
    span = 1
    for win in POOL_WINDOWS:
        while span < win:
            acc = acc + pltpu.roll(acc, span, 0)
            span *= 2
        sums.append(acc[halo:, :])
    lane = lax.broadcasted_iota(jnp.int32, (tm, POOL_WIDTH), 1)
    group = lane // POOL_GROUP_DIM
    wsum = sums[-1]
    win_lane = jnp.full((tm, POOL_WIDTH), POOL_WINDOWS[-1], jnp.int32)
    for gi in range(len(POOL_WINDOWS) - 2, -1, -1):
        wsum = jnp.where(group == gi, sums[gi], wsum)
        win_lane = jnp.where(group == gi, POOL_WINDOWS[gi], win_lane)
    t_pos = t_start + lax.broadcasted_iota(jnp.int32, (tm, POOL_WIDTH), 0)
    count = jnp.minimum(t_pos + 1, win_lane).astype(F32)
    pooled = wsum / count - u
    y_pool = _dot(pooled.astype(BF16), pw_ref[...]) * pscale_ref[...]
    ypool_ref[rows, :] = y_pool.astype(ypool_ref.dtype)

    hg_ref[rows, :] = proj[:, COL_HGRN:COL_HGRN + 4 * HGRN_WIDTH]

    cos, slo, shi = cos_ref[rows, :], slo_ref[rows, :], shi_ref[rows, :]
    q = _rope(proj[:, COL_Q:COL_Q + ATTN_WIDTH], cos, slo, shi) * (ATTN_SCALE * LOG2_E)
    q_ref[rows, :] = q.astype(q_ref.dtype)
    k = _rope(proj[:, COL_K:COL_K + KV_WIDTH], cos, slo, shi)
    k_ref[rows, :] = k.astype(k_ref.dtype)
    v_ref[rows, :] = proj[:, COL_V:COL_V + KV_WIDTH].astype(v_ref.dtype)
    qi = _rope(proj[:, COL_QI:COL_QI + IDX_HEADS * IDX_DIM], cos, slo, shi)
    qi_hi = qi.astype(BF16).astype(F32)
    qi_lo = qi - qi_hi
    zeros = jnp.zeros((tm, IDX_DIM), F32)
    pieces = []
    for hd in range(IDX_HEADS):
        cols = slice(hd * IDX_DIM, (hd + 1) * IDX_DIM)
        pieces += [qi_hi[:, cols], qi_hi[:, cols], qi_lo[:, cols], zeros]
    qi_ref[rows, :] = jnp.concatenate(pieces, axis=1).astype(qi_ref.dtype)
    kiw_raw = proj[:, COL_KI:COL_KI + LANES]
    ki = _rope(kiw_raw, cos, slo, shi)[:, 0:IDX_DIM]
    ki_hi = ki.astype(BF16).astype(F32)
    ki_ref[rows, :] = jnp.concatenate([ki_hi, ki - ki_hi, ki_hi, zeros], axis=1).astype(ki_ref.dtype)
    wi_ref[rows, :] = kiw_raw * IDX_W_SCALE
    return u[tm - halo:, :]


def _in_proj(x2d, norm_g, w_in_pad, cos, slo, shi, pool_w_bd, pool_scale, layer, seq_len):
    n_tok, d_model = x2d.shape
    tm = min(PROJ_TILE, seq_len)
    assert seq_len % tm == 0 and n_tok % tm == 0
    halo = 2 * SUBLANES
    assert halo >= POOL_WINDOWS[-1] and tm >= halo
    tok = lambda w: pl.BlockSpec((tm, w), lambda i: (i, 0))
    lay = lambda *s: pl.BlockSpec((None,) + s, lambda i: (layer,) + (0,) * len(s))
    out_shape = (
        jax.ShapeDtypeStruct((n_tok, POOL_WIDTH), BF16),
        jax.ShapeDtypeStruct((n_tok, 4 * HGRN_WIDTH), F32),
        jax.ShapeDtypeStruct((n_tok, ATTN_WIDTH), BF16),
        jax.ShapeDtypeStruct((n_tok, KV_WIDTH), BF16),
        jax.ShapeDtypeStruct((n_tok, KV_WIDTH), BF16),
        jax.ShapeDtypeStruct((n_tok, IDX_HEADS * IDX_CAT), BF16),
        jax.ShapeDtypeStruct((n_tok, IDX_CAT), BF16),
        jax.ShapeDtypeStruct((n_tok, LANES), F32),
    )
    return pl.pallas_call(
        functools.partial(_in_proj_kernel, tiles_per_seq=seq_len // tm),
        grid=(n_tok // tm,),
        in_specs=[tok(d_model), lay(1, d_model), lay(d_model, D_IN_PAD),
                  tok(LANES), tok(LANES), tok(LANES),
                  lay(POOL_WIDTH, POOL_WIDTH), lay(1, POOL_WIDTH)],
        out_specs=tuple(tok(s.shape[1]) for s in out_shape),
        out_shape=out_shape,
        scratch_shapes=[pltpu.VMEM((halo, POOL_WIDTH), F32)],
        compiler_params=pltpu.CompilerParams(dimension_semantics=("arbitrary",),
                                             vmem_limit_bytes=VMEM_LIMIT_BYTES),
        name="in_proj",
    )(x2d, norm_g, w_in_pad, cos, slo, shi, pool_w_bd, pool_scale)


def _hgrn_kernel(hg_ref, lb_ref, ng_ref, out_ref, state_ref, o_ref):
    tb, width = out_ref.shape
    c = HGRN_CHUNK
    half = c // 2
    n_chunks = tb // c
    heads = width // HGRN_HEAD_DIM

    @pl.when(pl.program_id(1) == 0)
    def _():
        state_ref[...] = jnp.zeros_like(state_ref)

    lb = lb_ref[...]
    q = _silu(hg_ref[:, 0:width])
    sig = jax.nn.sigmoid(hg_ref[:, width:2 * width])
    f = lb + (1.0 - lb) * sig
    log2_f = jnp.log(jnp.maximum(f, LOG_F_FLOOR)) * LOG2_E
    k = (1.0 - lb) * (1.0 - sig)
    v = hg_ref[:, 2 * width:3 * width]

    row_in_chunk = lax.broadcasted_iota(jnp.int32, (tb, width), 0) % c
    b = log2_f
    step = 1
    while step < c:
        b = b + jnp.where(row_in_chunk >= step, pltpu.roll(b, step, 0), 0.0)
        step *= 2

    lane_r = lax.broadcasted_iota(jnp.int32, (width, width), 0) // HGRN_HEAD_DIM
    lane_c = lax.broadcasted_iota(jnp.int32, (width, width), 1) // HGRN_HEAD_DIM
    head_sum = jnp.where(lane_r == lane_c, 1.0, 0.0).astype(BF16)

    tiles = lambda a: a.reshape(tb // half, half, width)
    q3, k3, b3, v3 = tiles(q), tiles(k), tiles(b), tiles(v)
    row_in_tile = lax.broadcasted_iota(jnp.int32, (tb // half, half, width), 1)
    o_near = None
    for s in range(half):
        if s == 0:
            p, vs = q3 * k3, v3
        else:
            ks, bs, vs = (pltpu.roll(a, s, 1) for a in (k3, b3, v3))
            p = jnp.where(row_in_tile >= s, q3 * ks * jnp.exp2(b3 - bs), 0.0)
        score = _dot(p.reshape(tb, width).astype(BF16), head_sum)
        term = score * vs.reshape(tb, width)
        o_near = term if o_near is None else o_near + term
    o_ref[...] = o_near
    first = lambda a: jnp.concatenate([a[ci * c:ci * c + half] for ci in range(n_chunks)], axis=0)
    second = lambda a: jnp.concatenate([a[ci * c + half:(ci + 1) * c] for ci in range(n_chunks)], axis=0)
    hb = tb // 2
    b1, b2 = first(b), second(b)
    b_mid = jnp.concatenate([jnp.broadcast_to(b1[(ci + 1) * half - 1:(ci + 1) * half], (half, width))
                             for ci in range(n_chunks)], axis=0)
    q_mid = second(q) * jnp.exp2(b2 - b_mid)
    k_mid = (first(k) * jnp.exp2(b_mid - b1)).astype(BF16)
    head_of_lane = lax.broadcasted_iota(jnp.int32, (hb, width), 1) // HGRN_HEAD_DIM
    q_heads = jnp.concatenate([jnp.where(head_of_lane == hd, q_mid, 0.0) for hd in range(heads)], axis=0)
    scores = _nt_dot(q_heads.astype(BF16), k_mid)
    t_chunk = (lax.broadcasted_iota(jnp.int32, (heads * hb, hb), 0) % hb) // half
    s_chunk = lax.broadcasted_iota(jnp.int32, (heads * hb, hb), 1) // half
    scores = jnp.where(t_chunk == s_chunk, scores, 0.0).astype(BF16)
    o_heads = _dot(scores, first(v).astype(BF16))
    o_far = jnp.where(head_of_lane == 0, o_heads[0:hb], 0.0)
    for hd in range(1, heads):
        o_far = jnp.where(head_of_lane == hd, o_heads[hd * hb:(hd + 1) * hb], o_far)
    for ci in range(n_chunks):
        o_ref[ci * c + half:(ci + 1) * c, :] += o_far[ci * half:(ci + 1) * half]

    lane_head = lax.broadcasted_iota(jnp.int32, (c, width), 1) // HGRN_HEAD_DIM
    for ci in range(n_chunks):
        rows = slice(ci * c, (ci + 1) * c)
        bc = b[rows]
        b_last = bc[c - 1:c]
        state = state_ref[...]
        q_dec = q[rows] * jnp.exp2(bc)
        q_heads = jnp.concatenate([jnp.where(lane_head == hd, q_dec, 0.0) for hd in range(heads)], axis=0)
        o_heads = _nt_dot(q_heads.astype(BF16), state.astype(BF16))
        o_inter = jnp.where(lane_head == 0, o_heads[0:c], 0.0)
        for hd in range(1, heads):
            o_inter = jnp.where(lane_head == hd, o_heads[hd * c:(hd + 1) * c], o_inter)
        o_ref[rows, :] += o_inter
        k_dec = k[rows] * jnp.exp2(b_last - bc)
        upd = _tn_dot(v[rows].astype(BF16), k_dec.astype(BF16))
        state_ref[...] = state * jnp.exp2(b_last) + upd

    o = o_ref[...]
    sq_hi, sq_lo = _split_bf16(o * o)
    mean_sq = (_dot(sq_hi, head_sum) + _dot(sq_lo, head_sum)) * (1.0 / HGRN_HEAD_DIM)
    gate = _silu(hg_ref[:, 3 * width:4 * width])
    out_ref[...] = (o * lax.rsqrt(mean_sq + RMS_EPS) * ng_ref[...] * gate).astype(out_ref.dtype)


def _hgrn(hg, lower_bound, norm_g, layer, batch, seq_len):
    n_tok = hg.shape[0]
    tb = min(HGRN_TILE, seq_len)
    assert seq_len % tb == 0 and tb % HGRN_CHUNK == 0 and HGRN_CHUNK == 2 * SUBLANES
    steps = seq_len // tb
    lay = pl.BlockSpec((None, 1, HGRN_WIDTH), lambda bi, ti: (layer, 0, 0))
    return pl.pallas_call(
        _hgrn_kernel,
        grid=(batch, steps),
        in_specs=[pl.BlockSpec((tb, 4 * HGRN_WIDTH), lambda bi, ti: (bi * steps + ti, 0)),
                  lay, lay],
        out_specs=pl.BlockSpec((tb, HGRN_WIDTH), lambda bi, ti: (bi * steps + ti, 0)),
        out_shape=jax.ShapeDtypeStruct((n_tok, HGRN_WIDTH), BF16),
        scratch_shapes=[pltpu.VMEM((HGRN_WIDTH, HGRN_WIDTH), F32),
                        pltpu.VMEM((tb, HGRN_WIDTH), F32)],
        compiler_params=pltpu.CompilerParams(dimension_semantics=("arbitrary", "arbitrary"),
                                             vmem_limit_bytes=VMEM_LIMIT_BYTES),
        name="hgrn2",
    )(hg, lower_bound, norm_g)


def _dsa_kernel(q_ref, k_ref, v_ref, qi_ref, ki_ref, wi_ref, out_ref,
                key_ref, hi_ref, lo_ref, thr_ref, need_ref, m_ref, acc_ref, *, top_k):
    qb = q_ref.shape[0]
    kt = key_ref.shape[1]
    i16 = jnp.int16
    rows16 = 2 * SUBLANES
    span = 4 * rows16
    tall = lambda a: jnp.concatenate([a] * (span // rows16), axis=0)
    t0 = pl.program_id(1) * qb
    n_tiles = (t0 + qb + kt - 1) // kt
    n_whole = (t0 + 1) // kt

    key_pos = lax.broadcasted_iota(jnp.int32, (kt, qb), 0)
    query_pos = t0 + lax.broadcasted_iota(jnp.int32, (kt, qb), 1)

    wi_t = wi_ref[...].T
    w_heads = [wi_t[IDX_DIM + h:IDX_DIM + h + 1, :] for h in range(IDX_HEADS)]

    def index_tile(ti, carry, causal_cut):
        base = pl.multiple_of(ti * kt, kt)
        ki_tile = ki_ref[pl.ds(base, kt), :]
        score = jnp.zeros((kt, qb), F32)
        for h in range(IDX_HEADS):
            s = _nt_dot(ki_tile, qi_ref[:, h * IDX_CAT:(h + 1) * IDX_CAT])
            score = score + jnp.maximum(s, 0.0) * w_heads[h]
        score = jnp.where(score == 0.0, 0.0, score)
        if causal_cut:
            score = jnp.where(base + key_pos <= query_pos, score, MASK_VALUE)
        bits = lax.bitcast_convert_type(score, jnp.int32)
        key = bits ^ ((bits >> 31) & 0x7FFFFFFF)
        key_ref[ti] = key
        hi_ref[ti] = (key >> 16).astype(i16)
        lo_ref[ti] = ((key & 0xFFFF) - 2 ** 15).astype(i16)
        return carry

    lax.fori_loop(0, n_whole, functools.partial(index_tile, causal_cut=False), 0)
    lax.fori_loop(n_whole, n_tiles, functools.partial(index_tile, causal_cut=True), 0)

    def count(ref, pred):
        def body(ti, acc):
            keys = ref[ti]
            for r in range(kt // span):
                acc = acc + jnp.where(pred(keys[r * span:(r + 1) * span, :]), i16(1), i16(0))
            return acc
        partial = lax.fori_loop(0, n_tiles, body, jnp.zeros((span, qb), i16))
        return jnp.sum(partial.astype(F32), axis=0, keepdims=True)

    def kth_largest(ref, target):
        def bit(it, state):
            thr, n_above = state
            cand = thr + lax.shift_left(jnp.int32(1), 15 - it)
            cand16 = tall(cand.astype(i16))
            n_ge = jnp.broadcast_to(count(ref, lambda keys: keys >= cand16), thr.shape)
            ok = n_ge >= target
            return jnp.where(ok, cand, thr), jnp.where(ok, n_above, n_ge)
        return lax.fori_loop(0, 16, bit, (jnp.full((rows16, qb), -2 ** 15, jnp.int32),
                                          jnp.zeros((rows16, qb), F32)))

    thr_ref[...] = jnp.full_like(thr_ref, INT_MIN)
    need_ref[...] = jnp.zeros_like(need_ref)

    @pl.when(t0 + qb > top_k)
    def _():
        upper, n_upper_gt = kth_largest(hi_ref, float(top_k))
        upper_kt = jnp.concatenate([tall(upper.astype(i16))] * (kt // span), axis=0)

        def keep_bucket(ti, carry):
            lo_ref[ti] = jnp.where(hi_ref[ti] == upper_kt, lo_ref[ti], i16(-2 ** 15))
            return carry

        lax.fori_loop(0, n_tiles, keep_bucket, 0)
        lower, n_lower_gt = kth_largest(lo_ref, top_k - n_upper_gt)
        thr_ref[...] = (lax.shift_left(upper, 16) + (lower + 2 ** 15))[0:SUBLANES]
        need_ref[...] = (top_k - (n_upper_gt + n_lower_gt))[0:SUBLANES]

    thr = thr_ref[0:1, :]
    need = need_ref[0:1, :]

    m_ref[...] = jnp.full_like(m_ref, MASK_VALUE)
    acc_ref[...] = jnp.zeros_like(acc_ref)
    earlier = (lax.broadcasted_iota(jnp.int32, (kt, kt), 1)
               < lax.broadcasted_iota(jnp.int32, (kt, kt), 0))
    earlier_bf = jnp.where(earlier, 1.0, 0.0).astype(BF16)
    ones_rows = jnp.ones((SUBLANES, kt), BF16)
    dim_v = lax.broadcasted_iota(jnp.int32, (KV_WIDTH, kt), 0)
    q_stack = [jnp.concatenate([q_ref[:, h * HEAD_DIM:(h + 1) * HEAD_DIM]
                                for h in range(g * KV_GROUP, (g + 1) * KV_GROUP)], axis=0)
               for g in range(N_KV_HEADS)]

    def attend_tile(ti, ties_before, causal_cut):
        base = pl.multiple_of(ti * kt, kt)
        k_tile = k_ref[pl.ds(base, kt), :]
        s_all = [_nt_dot(k_tile[:, g * HEAD_DIM:(g + 1) * HEAD_DIM], q_stack[g])
                 for g in range(N_KV_HEADS)]
        keys = key_ref[ti]
        tie = keys == thr
        tie_bf = jnp.where(tie, 1.0, 0.0).astype(BF16)
        rank = _dot(earlier_bf, tie_bf) + ties_before
        chosen = (keys > thr) | (tie & (rank < need))
        if causal_cut:
            chosen = chosen & (base + key_pos <= query_pos)
        bias = jnp.where(chosen, 0.0, MASK_VALUE)
        v_t = v_ref[pl.ds(base, kt), :].astype(F32).T
        for g in range(N_KV_HEADS):
            v_aug = jnp.where(dim_v // HEAD_DIM == g, v_t, 1.0).astype(BF16)
            for hh in range(KV_GROUP):
                h = g * KV_GROUP + hh
                s = s_all[g][:, hh * qb:(hh + 1) * qb] + bias
                m_old = m_ref[h]
                m_new = jnp.maximum(m_old, jnp.max(s, axis=0, keepdims=True))
                p = jnp.exp2(s - m_new)
                acc_ref[h] = jnp.exp2(m_old - m_new) * acc_ref[h] + _dot(v_aug, p.astype(BF16))
                m_ref[h] = m_new
        return ties_before + _dot(ones_rows, tie_bf)[0:1]

    ties = lax.fori_loop(0, n_whole, functools.partial(attend_tile, causal_cut=False),
                         jnp.zeros((1, qb), F32))
    lax.fori_loop(n_whole, n_tiles, functools.partial(attend_tile, causal_cut=True), ties)
    for pair in range(N_HEADS // 2):
        outs = []
        for h in (2 * pair, 2 * pair + 1):
            g = h // KV_GROUP
            acc = acc_ref[h]
            other = (1 - g) * HEAD_DIM
            outs.append(acc[g * HEAD_DIM:(g + 1) * HEAD_DIM, :] / acc[other:other + 1, :])
        out_ref[:, pair * LANES:(pair + 1) * LANES] = (
            jnp.concatenate(outs, axis=0).T.astype(out_ref.dtype))


def _dsa(q, k, v, qi, ki, wi, batch, seq_len):
    n_tok = q.shape[0]
    qb = min(Q_TILE, seq_len)
    kt = min(KEY_TILE, seq_len)
    assert seq_len % qb == 0 and seq_len % kt == 0 and kt % qb == 0 and qb % LANES == 0
    assert N_KV_HEADS == 2 and KV_WIDTH == LANES and 2 * HEAD_DIM == LANES
    steps = seq_len // qb
    n_kt = seq_len // kt
    top_k = min(TOPK_MAX, seq_len // 4)
    qtile = lambda w: pl.BlockSpec((qb, w), lambda bi, qi_: (bi * steps + qi_, 0))
    seq = lambda w: pl.BlockSpec((seq_len, w), lambda bi, qi_: (bi, 0))
    return pl.pallas_call(
        functools.partial(_dsa_kernel, top_k=top_k),
        grid=(batch, steps),
        in_specs=[qtile(ATTN_WIDTH), seq(KV_WIDTH), seq(KV_WIDTH),
                  qtile(IDX_HEADS * IDX_CAT), seq(IDX_CAT), qtile(LANES)],
        out_specs=qtile(ATTN_WIDTH),
        out_shape=jax.ShapeDtypeStruct((n_tok, ATTN_WIDTH), BF16),
        scratch_shapes=[pltpu.VMEM((n_kt, kt, qb), jnp.int32),
                        pltpu.VMEM((n_kt, kt, qb), jnp.int16),
                        pltpu.VMEM((n_kt, kt, qb), jnp.int16),
                        pltpu.VMEM((SUBLANES, qb), jnp.int32),
                        pltpu.VMEM((SUBLANES, qb), F32),
                        pltpu.VMEM((N_HEADS, 1, qb), F32),
                        pltpu.VMEM((N_HEADS, KV_WIDTH, qb), F32)],
        compiler_params=pltpu.CompilerParams(dimension_semantics=("arbitrary", "arbitrary"),
                                             vmem_limit_bytes=VMEM_LIMIT_BYTES),
        name="dsa",
    )(q, k, v, qi, ki, wi)


def _out_ffn_kernel(x_ref, yp_ref, yh_ref, ya_ref, wo_hbm, g2_ref, wfi_hbm, wfo_hbm, gf_ref,
                    out_ref, wo_ref, wfi_ref, wfo_ref, *, layer, final_norm):
    @pl.when(pl.program_id(0) == 0)
    def _():
        pltpu.sync_copy(wo_hbm.at[layer], wo_ref)
        pltpu.sync_copy(wfi_hbm.at[layer], wfi_ref)
        pltpu.sync_copy(wfo_hbm.at[layer], wfo_ref)

    x = x_ref[...]
    x = x + _dot(yp_ref[...], wo_ref[0:POOL_WIDTH, :])
    x = x + _dot(yh_ref[...], wo_ref[POOL_WIDTH:POOL_WIDTH + HGRN_WIDTH, :])
    x = x + _dot(ya_ref[...], wo_ref[POOL_WIDTH + HGRN_WIDTH:, :])
    h = _rms_norm(x, g2_ref[...]).astype(BF16)
    a = jnp.maximum(_dot(h, wfi_ref[...]), 0.0)
    x = x + _dot((a * a).astype(BF16), wfo_ref[...])
    if final_norm:
        x = _rms_norm(x, gf_ref[...])
    out_ref[...] = x


def _out_ffn(x2d, y_pool, y_hgrn, y_attn, w_out, norm2_g, w_ff_in, w_ff_out, final_g,
             layer, final_norm, seq_len):
    n_tok, d_model = x2d.shape
    d_ff = w_ff_in.shape[2]
    tm = min(PROJ_TILE, seq_len)
    tok = lambda w: pl.BlockSpec((tm, w), lambda i: (i, 0))
    hbm = pl.BlockSpec(memory_space=pl.ANY)
    return pl.pallas_call(
        functools.partial(_out_ffn_kernel, layer=layer, final_norm=final_norm),
        grid=(n_tok // tm,),
        in_specs=[tok(d_model), tok(POOL_WIDTH), tok(HGRN_WIDTH), tok(ATTN_WIDTH),
                  hbm, pl.BlockSpec((None, 1, d_model), lambda i: (layer, 0, 0)), hbm, hbm,
                  pl.BlockSpec((1, d_model), lambda i: (0, 0))],
        out_specs=tok(d_model),
        out_shape=jax.ShapeDtypeStruct((n_tok, d_model), F32),
        scratch_shapes=[pltpu.VMEM((d_model, d_model), BF16),
                        pltpu.VMEM((d_model, d_ff), BF16),
                        pltpu.VMEM((d_ff, d_model), BF16)],
        compiler_params=pltpu.CompilerParams(dimension_semantics=("arbitrary",),
                                             vmem_limit_bytes=VMEM_LIMIT_BYTES),
        name="out_ffn",
    )(x2d, y_pool, y_hgrn, y_attn, w_out, norm2_g, w_ff_in, w_ff_out, final_g)


def _rope_patterns(positions):
    inv_freq = ROPE_THETA ** (-jnp.arange(0, ROT_DIM, 2, dtype=F32) / ROT_DIM)
    rest = jnp.zeros((HEAD_DIM - ROT_DIM,), F32)
    freq_head = jnp.concatenate([inv_freq, inv_freq, rest])
    lo_head = jnp.concatenate([-jnp.ones((ROT_HALF,), F32), jnp.zeros((ROT_HALF,), F32), rest])
    hi_head = jnp.concatenate([jnp.zeros((ROT_HALF,), F32), jnp.ones((ROT_HALF,), F32), rest])
    two = lambda a: jnp.concatenate([a, a])[None, :]
    ang = positions.astype(F32).reshape(-1, 1) * two(freq_head)
    sin = jnp.sin(ang)
    return jnp.cos(ang), sin * two(lo_head), sin * two(hi_head)


def kernel(x, positions, norm1_g, w_in, pool_w, pool_scale, lb_logits, hgrn_norm_g, w_out,
           norm2_g, w_ff_in, w_ff_out, final_norm_g):
    batch, seq_len, d_model = x.shape
    depth = w_in.shape[0]
    assert w_in.shape[2] == D_IN

    p_lb = jax.nn.softmax(lb_logits.astype(F32), axis=0)
    lower_bounds = (jnp.cumsum(p_lb, axis=0) - p_lb[0])[:, None, :]
    w_in_pad = jnp.pad(w_in, ((0, 0), (0, 0), (0, D_IN_PAD - D_IN))).astype(BF16)
    groups = len(POOL_WINDOWS)
    eye = jnp.eye(groups, dtype=F32)
    pool_w_bd = (pool_w[:, :, :, None, :] * eye[None, :, None, :, None]).reshape(
        depth, POOL_WIDTH, POOL_WIDTH).astype(BF16)
    w_out_b, w_ff_in_b, w_ff_out_b = (w.astype(BF16) for w in (w_out, w_ff_in, w_ff_out))
    row = lambda a: a.reshape(depth, 1, -1).astype(F32)
    norm1, norm2, pscale, hnorm = row(norm1_g), row(norm2_g), row(pool_scale), row(hgrn_norm_g)
    final_g = final_norm_g.reshape(1, d_model).astype(F32)
    cos, slo, shi = _rope_patterns(positions)

    x2d = x.reshape(batch * seq_len, d_model)
    for layer in range(depth):
        y_pool, hg, q, k, v, qi, ki, wi = _in_proj(x2d, norm1, w_in_pad, cos, slo, shi, pool_w_bd,
                                                pscale, layer, seq_len)
        y_hgrn = _hgrn(hg, lower_bounds, hnorm, layer, batch, seq_len)
        y_attn = _dsa(q, k, v, qi, ki, wi, batch, seq_len)
        x2d = _out_ffn(x2d, y_pool, y_hgrn, y_attn, w_out_b, norm2, w_ff_in_b, w_ff_out_b,
                       final_g, layer, layer == depth - 1, seq_len)
    return x2d.reshape(batch, seq_len, d_model)
```

```python
import functools

import jax
import jax.numpy as jnp
from jax import lax
from jax.experimental import pallas as pl
from jax.experimental.pallas import tpu as pltpu

F32 = jnp.float32
BF16 = jnp.bfloat16

POOL_WINDOWS = (2, 4, 8, 16)
POOL_WIDTH = 256
POOL_GROUP_DIM = 64
HGRN_WIDTH = 256
HGRN_HEAD_DIM = 64
LOG_F_FLOOR = 1e-30
ATTN_WIDTH = 512
HEAD_DIM = 64
N_HEADS = 8
N_KV_HEADS = 2
KV_GROUP = N_HEADS // N_KV_HEADS
KV_WIDTH = N_KV_HEADS * HEAD_DIM
IDX_HEADS = 4
IDX_DIM = 64
TOPK_MAX = 256
MASK_VALUE = -1e30
ROPE_THETA = 500000.0
ROT_DIM = HEAD_DIM // 4
ROT_HALF = ROT_DIM // 2
RMS_EPS = 1e-5
IDX_W_SCALE = (IDX_HEADS ** -0.5) * (IDX_DIM ** -0.5)
ATTN_SCALE = HEAD_DIM ** -0.5
LOG2_E = 1.4426950408889634
SHIFT_MARGIN = 1.02
SAFE_DENOMINATOR = 2.0 ** -60

LANES = 128
SUBLANES = 8
VMEM_LIMIT_BYTES = 56 * 1024 * 1024

COL_POOL = 0
COL_HGRN = COL_POOL + POOL_WIDTH
COL_Q = COL_HGRN + 4 * HGRN_WIDTH
COL_K = COL_Q + ATTN_WIDTH
COL_V = COL_K + KV_WIDTH
COL_QI = COL_V + KV_WIDTH
COL_KI = COL_QI + IDX_HEADS * IDX_DIM
D_IN = COL_KI + IDX_DIM + IDX_HEADS
D_IN_PAD = COL_KI + LANES
IDX_CAT = 4 * IDX_DIM

PROJ_TILE = 512
PROJ_SUB = 128
HGRN_TILE = 256
HGRN_CHUNK = 16
Q_TILE = 256
KEY_TILE = 256
INT_MIN = -2 ** 31


def _nt_dot(a, b):
    return lax.dot_general(a, b, (((1,), (1,)), ((), ())), preferred_element_type=F32)


def _tn_dot(a, b):
    return lax.dot_general(a, b, (((0,), (0,)), ((), ())), preferred_element_type=F32)


def _dot(a, b):
    return jnp.dot(a, b, preferred_element_type=F32)


def _split_bf16(a):
    hi = a.astype(BF16)
    lo = (a - hi.astype(F32)).astype(BF16)
    return hi, lo


def _rms_norm(x, g):
    return x * lax.rsqrt(jnp.mean(x * x, axis=-1, keepdims=True) + RMS_EPS) * g


def _silu(x):
    return x * jax.nn.sigmoid(x)


def _rope(x, cos, sin_lo, sin_hi):
    w = x.shape[1]
    rep = w // LANES
    if rep > 1:
        cos = jnp.concatenate([cos] * rep, axis=1)
        sin_lo = jnp.concatenate([sin_lo] * rep, axis=1)
        sin_hi = jnp.concatenate([sin_hi] * rep, axis=1)
    from_below = pltpu.roll(x, ROT_HALF, 1)
    from_above = pltpu.roll(x, w - ROT_HALF, 1)
    return x * cos + from_below * sin_hi + from_above * sin_lo


def _in_proj_kernel(x_ref, g_ref, w_ref, cos_ref, slo_ref, shi_ref, pw_ref, pscale_ref,
                    ypool_ref, hg_ref, q_ref, k_ref, v_ref, qi_ref, ki_ref, wi_ref,
                    halo_ref, *, tiles_per_seq):
    tm = x_ref.shape[0]
    sub = min(PROJ_SUB, tm)
    seq_tile = pl.program_id(0) % tiles_per_seq

    @pl.when(seq_tile == 0)
    def _():
        halo_ref[...] = jnp.zeros_like(halo_ref)

    prev = halo_ref[...]
    for r in range(tm // sub):
        prev = _in_proj_rows(slice(r * sub, (r + 1) * sub), seq_tile * tm + r * sub, prev,
                             x_ref, g_ref, w_ref, cos_ref, slo_ref, shi_ref, pw_ref, pscale_ref,
                             ypool_ref, hg_ref, q_ref, k_ref, v_ref, qi_ref, ki_ref, wi_ref)
    halo_ref[...] = prev


def _in_proj_rows(rows, t_start, prev, x_ref, g_ref, w_ref, cos_ref, slo_ref, shi_ref, pw_ref,
                  pscale_ref, ypool_ref, hg_ref, q_ref, k_ref, v_ref, qi_ref, ki_ref, wi_ref):
    tm = rows.stop - rows.start
    halo = prev.shape[0]
    h = _rms_norm(x_ref[rows, :], g_ref[...])
    proj = _dot(h.astype(BF16), w_ref[...])

    u = proj[:, COL_POOL:COL_POOL + POOL_WIDTH]
    ext = jnp.concatenate([prev, u], axis=0)
    sums = []
    acc = ext
    span = 1
    for win in POOL_WINDOWS:
        while span < win:
            acc = acc + pltpu.roll(acc, span, 0)
            span *= 2
        sums.append(acc[halo:, :])
    lane = lax.broadcasted_iota(jnp.int32, (tm, POOL_WIDTH), 1)
    group = lane // POOL_GROUP_DIM
    wsum = sums[-1]
    win_lane = jnp.full((tm, POOL_WIDTH), POOL_WINDOWS[-1], jnp.int32)
    for gi in range(len(POOL_WINDOWS) - 2, -1, -1):
        wsum = jnp.where(group == gi, sums[gi], wsum)
        win_lane = jnp.where(group == gi, POOL_WINDOWS[gi], win_lane)
    t_pos = t_start + lax.broadcasted_iota(jnp.int32, (tm, POOL_WIDTH), 0)
    count = jnp.minimum(t_pos + 1, win_lane).astype(F32)
    pooled = wsum / count - u
    y_pool = _dot(pooled.astype(BF16), pw_ref[...]) * pscale_ref[...]
    ypool_ref[rows, :] = y_pool.astype(ypool_ref.dtype)

    hg_ref[rows, :] = proj[:, COL_HGRN:COL_HGRN + 4 * HGRN_WIDTH]

    cos, slo, shi = cos_ref[rows, :], slo_ref[rows, :], shi_ref[rows, :]
    q = _rope(proj[:, COL_Q:COL_Q + ATTN_WIDTH], cos, slo, shi) * (ATTN_SCALE * LOG2_E)
    q_ref[rows, :] = q.astype(q_ref.dtype)
    k = _rope(proj[:, COL_K:COL_K + KV_WIDTH], cos, slo, shi)
    k_ref[rows, :] = k.astype(k_ref.dtype)
    v_ref[rows, :] = proj[:, COL_V:COL_V + KV_WIDTH].astype(v_ref.dtype)
    qi = _rope(proj[:, COL_QI:COL_QI + IDX_HEADS * IDX_DIM], cos, slo, shi)
    qi_hi = qi.astype(BF16).astype(F32)
    qi_lo = qi - qi_hi
    zeros = jnp.zeros((tm, IDX_DIM), F32)
    pieces = []
    for hd in range(IDX_HEADS):
        cols = slice(hd * IDX_DIM, (hd + 1) * IDX_DIM)
        pieces += [qi_hi[:, cols], qi_hi[:, cols], qi_lo[:, cols], zeros]
    qi_ref[rows, :] = jnp.concatenate(pieces, axis=1).astype(qi_ref.dtype)
    kiw_raw = proj[:, COL_KI:COL_KI + LANES]
    ki = _rope(kiw_raw, cos, slo, shi)[:, 0:IDX_DIM]
    ki_hi = ki.astype(BF16).astype(F32)
    ki_ref[rows, :] = jnp.concatenate([ki_hi, ki - ki_hi, ki_hi, zeros], axis=1).astype(ki_ref.dtype)
    wi_ref[rows, :] = kiw_raw * IDX_W_SCALE
    return u[tm - halo:, :]


def _in_proj(x2d, norm_g, w_in_pad, cos, slo, shi, pool_w_bd, pool_scale, layer, seq_len):
    n_tok, d_model = x2d.shape
    tm = min(PROJ_TILE, seq_len)
    assert seq_len % tm == 0 and n_tok % tm == 0
    halo = 2 * SUBLANES
    assert halo >= POOL_WINDOWS[-1] and tm >= halo
    tok = lambda w: pl.BlockSpec((tm, w), lambda i: (i, 0))
    lay = lambda *s: pl.BlockSpec((None,) + s, lambda i: (layer,) + (0,) * len(s))
    out_shape = (
        jax.ShapeDtypeStruct((n_tok, POOL_WIDTH), BF16),
        jax.ShapeDtypeStruct((n_tok, 4 * HGRN_WIDTH), F32),
        jax.ShapeDtypeStruct((n_tok, ATTN_WIDTH), BF16),
        jax.ShapeDtypeStruct((n_tok, KV_WIDTH), BF16),
        jax.ShapeDtypeStruct((n_tok, KV_WIDTH), BF16),
        jax.ShapeDtypeStruct((n_tok, IDX_HEADS * IDX_CAT), BF16),
        jax.ShapeDtypeStruct((n_tok, IDX_CAT), BF16),
        jax.ShapeDtypeStruct((n_tok, LANES), F32),
    )
    return pl.pallas_call(
        functools.partial(_in_proj_kernel, tiles_per_seq=seq_len // tm),
        grid=(n_tok // tm,),
        in_specs=[tok(d_model), lay(1, d_model), lay(d_model, D_IN_PAD),
                  tok(LANES), tok(LANES), tok(LANES),
                  lay(POOL_WIDTH, POOL_WIDTH), lay(1, POOL_WIDTH)],
        out_specs=tuple(tok(s.shape[1]) for s in out_shape),
        out_shape=out_shape,
        scratch_shapes=[pltpu.VMEM((halo, POOL_WIDTH), F32)],
        compiler_params=pltpu.CompilerParams(dimension_semantics=("arbitrary",),
                                             vmem_limit_bytes=VMEM_LIMIT_BYTES),
        name="in_proj",
    )(x2d, norm_g, w_in_pad, cos, slo, shi, pool_w_bd, pool_scale)


def _hgrn_kernel(hg_ref, lb_ref, ng_ref, out_ref, state_ref, o_ref):
    tb, width = out_ref.shape
    c = HGRN_CHUNK
    half = c // 2
    n_chunks = tb // c
    heads = width // HGRN_HEAD_DIM

    @pl.when(pl.program_id(1) == 0)
    def _():
        state_ref[...] = jnp.zeros_like(state_ref)

    lb = lb_ref[...]
    q = _silu(hg_ref[:, 0:width])
    sig = jax.nn.sigmoid(hg_ref[:, width:2 * width])
    f = lb + (1.0 - lb) * sig
    log2_f = jnp.log(jnp.maximum(f, LOG_F_FLOOR)) * LOG2_E
    k = (1.0 - lb) * (1.0 - sig)
    v = hg_ref[:, 2 * width:3 * width]

    row_in_chunk = lax.broadcasted_iota(jnp.int32, (tb, width), 0) % c
    b = log2_f
    step = 1
    while step < c:
        b = b + jnp.where(row_in_chunk >= step, pltpu.roll(b, step, 0), 0.0)
        step *= 2

    lane_r = lax.broadcasted_iota(jnp.int32, (width, width), 0) // HGRN_HEAD_DIM
    lane_c = lax.broadcasted_iota(jnp.int32, (width, width), 1) // HGRN_HEAD_DIM
    head_sum = jnp.where(lane_r == lane_c, 1.0, 0.0).astype(BF16)

    tiles = lambda a: a.reshape(tb // half, half, width)
    q3, k3, b3, v3 = tiles(q), tiles(k), tiles(b), tiles(v)
    row_in_tile = lax.broadcasted_iota(jnp.int32, (tb // half, half, width), 1)
    o_near = None
    for s in range(half):
        if s == 0:
            p, vs = q3 * k3, v3
        else:
            ks, bs, vs = (pltpu.roll(a, s, 1) for a in (k3, b3, v3))
            p = jnp.where(row_in_tile >= s, q3 * ks * jnp.exp2(b3 - bs), 0.0)
        score = _dot(p.reshape(tb, width).astype(BF16), head_sum)
        term = score * vs.reshape(tb, width)
        o_near = term if o_near is None else o_near + term
    o_ref[...] = o_near
    first = lambda a: jnp.concatenate([a[ci * c:ci * c + half] for ci in range(n_chunks)], axis=0)
    second = lambda a: jnp.concatenate([a[ci * c + half:(ci + 1) * c] for ci in range(n_chunks)], axis=0)
    hb = tb // 2
    b1, b2 = first(b), second(b)
    b_mid = jnp.concatenate([jnp.broadcast_to(b1[(ci + 1) * half - 1:(ci + 1) * half], (half, width))
                             for ci in range(n_chunks)], axis=0)
    q_mid = second(q) * jnp.exp2(b2 - b_mid)
    k_mid = (first(k) * jnp.exp2(b_mid - b1)).astype(BF16)
    head_of_lane = lax.broadcasted_iota(jnp.int32, (hb, width), 1) // HGRN_HEAD_DIM
    q_heads = jnp.concatenate([jnp.where(head_of_lane == hd, q_mid, 0.0) for hd in range(heads)], axis=0)
    scores = _nt_dot(q_heads.astype(BF16), k_mid)
    t_chunk = (lax.broadcasted_iota(jnp.int32, (heads * hb, hb), 0) % hb) // half
    s_chunk = lax.broadcasted_iota(jnp.int32, (heads * hb, hb), 1) // half
    scores = jnp.where(t_chunk == s_chunk, scores, 0.0).astype(BF16)
    o_heads = _dot(scores, first(v).astype(BF16))
    o_far = jnp.where(head_of_lane == 0, o_heads[0:hb], 0.0)
    for hd in range(1, heads):
        o_far = jnp.where(head_of_lane == hd, o_heads[hd * hb:(hd + 1) * hb], o_far)
    for ci in range(n_chunks):
        o_ref[ci * c + half:(ci + 1) * c, :] += o_far[ci * half:(ci + 1) * half]

    lane_head = lax.broadcasted_iota(jnp.int32, (c, width), 1) // HGRN_HEAD_DIM
    for ci in range(n_chunks):
        rows = slice(ci * c, (ci + 1) * c)
        bc = b[rows]
        b_last = bc[c - 1:c]
        state = state_ref[...]
        q_dec = q[rows] * jnp.exp2(bc)
        q_heads = jnp.concatenate([jnp.where(lane_head == hd, q_dec, 0.0) for hd in range(heads)], axis=0)
        o_heads = _nt_dot(q_heads.astype(BF16), state.astype(BF16))
        o_inter = jnp.where(lane_head == 0, o_heads[0:c], 0.0)
        for hd in range(1, heads):
            o_inter = jnp.where(lane_head == hd, o_heads[hd * c:(hd + 1) * c], o_inter)
        o_ref[rows, :] += o_inter
        k_dec = k[rows] * jnp.exp2(b_last - bc)
        upd = _tn_dot(v[rows].astype(BF16), k_dec.astype(BF16))
        state_ref[...] = state * jnp.exp2(b_last) + upd

    o = o_ref[...]
    sq_hi, sq_lo = _split_bf16(o * o)
    mean_sq = (_dot(sq_hi, head_sum) + _dot(sq_lo, head_sum)) * (1.0 / HGRN_HEAD_DIM)
    gate = _silu(hg_ref[:, 3 * width:4 * width])
    out_ref[...] = (o * lax.rsqrt(mean_sq + RMS_EPS) * ng_ref[...] * gate).astype(out_ref.dtype)


def _hgrn(hg, lower_bound, norm_g, layer, batch, seq_len):
    n_tok = hg.shape[0]
    tb = min(HGRN_TILE, seq_len)
    assert seq_len % tb == 0 and tb % HGRN_CHUNK == 0 and HGRN_CHUNK == 2 * SUBLANES
    steps = seq_len // tb
    lay = pl.BlockSpec((None, 1, HGRN_WIDTH), lambda bi, ti: (layer, 0, 0))
    return pl.pallas_call(
        _hgrn_kernel,
        grid=(batch, steps),
        in_specs=[pl.BlockSpec((tb, 4 * HGRN_WIDTH), lambda bi, ti: (bi * steps + ti, 0)),
                  lay, lay],
        out_specs=pl.BlockSpec((tb, HGRN_WIDTH), lambda bi, ti: (bi * steps + ti, 0)),
        out_shape=jax.ShapeDtypeStruct((n_tok, HGRN_WIDTH), BF16),
        scratch_shapes=[pltpu.VMEM((HGRN_WIDTH, HGRN_WIDTH), F32),
                        pltpu.VMEM((tb, HGRN_WIDTH), F32)],
        compiler_params=pltpu.CompilerParams(dimension_semantics=("arbitrary", "arbitrary"),
                                             vmem_limit_bytes=VMEM_LIMIT_BYTES),
        name="hgrn2",
    )(hg, lower_bound, norm_g)


def _dsa_kernel(q_ref, k_ref, v_ref, qi_ref, ki_ref, wi_ref, out_ref,
                key_ref, hi_ref, lo_ref, thr_ref, need_ref, knorm_ref, m_ref, acc_ref, *, top_k):
    qb = q_ref.shape[0]
    kt = key_ref.shape[1]
    i16 = jnp.int16
    rows16 = 2 * SUBLANES
    span = 4 * rows16
    tall = lambda a: jnp.concatenate([a] * (span // rows16), axis=0)
    t0 = pl.program_id(1) * qb
    n_tiles = (t0 + qb + kt - 1) // kt
    n_whole = (t0 + 1) // kt

    key_pos = lax.broadcasted_iota(jnp.int32, (kt, qb), 0)
    query_pos = t0 + lax.broadcasted_iota(jnp.int32, (kt, qb), 1)

    @pl.when(pl.program_id(1) == 0)
    def _():
        kv_cols = (lax.broadcasted_iota(jnp.int32, (KV_WIDTH, LANES), 0) // HEAD_DIM
                   == lax.broadcasted_iota(jnp.int32, (KV_WIDTH, LANES), 1))
        kv_cols = jnp.where(kv_cols, 1.0, 0.0).astype(BF16)

        def key_norms(ti, best):
            k_f32 = k_ref[pl.ds(pl.multiple_of(ti * kt, kt), kt), :].astype(F32)
            kk_hi, kk_lo = _split_bf16(k_f32 * k_f32)
            norm2 = _dot(kk_hi, kv_cols) + _dot(kk_lo, kv_cols)
            return jnp.maximum(best, jnp.max(norm2, axis=0, keepdims=True))

        best = lax.fori_loop(0, k_ref.shape[0] // kt, key_norms, jnp.zeros((1, LANES), F32))
        knorm_ref[...] = jnp.broadcast_to(best, knorm_ref.shape)

    wi_t = wi_ref[...].T
    w_heads = [wi_t[IDX_DIM + h:IDX_DIM + h + 1, :] for h in range(IDX_HEADS)]

    def index_tile(ti, carry, causal_cut):
        base = pl.multiple_of(ti * kt, kt)
        ki_tile = ki_ref[pl.ds(base, kt), :]
        score = jnp.zeros((kt, qb), F32)
        for h in range(IDX_HEADS):
            s = _nt_dot(ki_tile, qi_ref[:, h * IDX_CAT:(h + 1) * IDX_CAT])
            score = score + jnp.maximum(s, 0.0) * w_heads[h]
        score = jnp.where(score == 0.0, 0.0, score)
        if causal_cut:
            score = jnp.where(base + key_pos <= query_pos, score, MASK_VALUE)
        bits = lax.bitcast_convert_type(score, jnp.int32)
        key = bits ^ ((bits >> 31) & 0x7FFFFFFF)
        key_ref[ti] = key
        hi_ref[ti] = (key >> 16).astype(i16)
        lo_ref[ti] = ((key & 0xFFFF) - 2 ** 15).astype(i16)
        return carry

    lax.fori_loop(0, n_whole, functools.partial(index_tile, causal_cut=False), 0)
    lax.fori_loop(n_whole, n_tiles, functools.partial(index_tile, causal_cut=True), 0)

    def count(ref, pred):
        def body(ti, acc):
            keys = ref[ti]
            for r in range(kt // span):
                acc = acc + jnp.where(pred(keys[r * span:(r + 1) * span, :]), i16(1), i16(0))
            return acc
        partial = lax.fori_loop(0, n_tiles, body, jnp.zeros((span, qb), i16))
        return jnp.sum(partial.astype(F32), axis=0, keepdims=True)

    def kth_largest(ref, target):
        def bit(it, state):
            thr, n_above = state
            cand = thr + lax.shift_left(jnp.int32(1), 15 - it)
            cand16 = tall(cand.astype(i16))
            n_ge = jnp.broadcast_to(count(ref, lambda keys: keys >= cand16), thr.shape)
            ok = n_ge >= target
            return jnp.where(ok, cand, thr), jnp.where(ok, n_above, n_ge)
        return lax.fori_loop(0, 16, bit, (jnp.full((rows16, qb), -2 ** 15, jnp.int32),
                                          jnp.zeros((rows16, qb), F32)))

    thr_ref[...] = jnp.full_like(thr_ref, INT_MIN)
    need_ref[...] = jnp.zeros_like(need_ref)

    @pl.when(t0 + qb > top_k)
    def _():
        upper, n_upper_gt = kth_largest(hi_ref, float(top_k))
        upper_kt = jnp.concatenate([tall(upper.astype(i16))] * (kt // span), axis=0)

        def keep_bucket(ti, carry):
            lo_ref[ti] = jnp.where(hi_ref[ti] == upper_kt, lo_ref[ti], i16(-2 ** 15))
            return carry

        lax.fori_loop(0, n_tiles, keep_bucket, 0)
        lower, n_lower_gt = kth_largest(lo_ref, top_k - n_upper_gt)
        thr_ref[...] = (lax.shift_left(upper, 16) + (lower + 2 ** 15))[0:SUBLANES]
        need_ref[...] = (top_k - (n_upper_gt + n_lower_gt))[0:SUBLANES]

    thr = thr_ref[0:1, :]
    need = need_ref[0:1, :]

    earlier =(lax.broadcasted_iota(jnp.int32, (kt, kt), 1)
               < lax.broadcasted_iota(jnp.int32, (kt, kt), 0))
    earlier_bf = jnp.where(earlier, 1.0, 0.0).astype(BF16)
    ones_rows = jnp.ones((SUBLANES, kt), BF16)
    dim_v = lax.broadcasted_iota(jnp.int32, (KV_WIDTH, kt), 0)

    def selection(ti, ties_before, causal_cut):
        base = pl.multiple_of(ti * kt, kt)
        keys = key_ref[ti]
        tie = keys == thr
        tie_bf = jnp.where(tie, 1.0, 0.0).astype(BF16)
        rank = _dot(earlier_bf, tie_bf) + ties_before
        chosen = (keys > thr) | (tie & (rank < need))
        if causal_cut:
            chosen = chosen & (base + key_pos <= query_pos)
        return jnp.where(chosen, 0.0, MASK_VALUE), ties_before + _dot(ones_rows, tie_bf)[0:1]

    def values(ti):
        v_t = v_ref[pl.ds(pl.multiple_of(ti * kt, kt), kt), :].astype(F32).T
        return [jnp.where(dim_v // HEAD_DIM == g, v_t, 1.0).astype(BF16) for g in range(N_KV_HEADS)]

    head_cols = (lax.broadcasted_iota(jnp.int32, (ATTN_WIDTH, LANES), 0) // HEAD_DIM
                 == lax.broadcasted_iota(jnp.int32, (ATTN_WIDTH, LANES), 1))
    head_cols = jnp.where(head_cols, 1.0, 0.0).astype(BF16)
    q_sq = q_ref[...] * q_ref[...]
    q_norm2 = jnp.max(_dot(q_sq, head_cols), axis=0, keepdims=True)
    k_norm2 = knorm_ref[0:1, :]
    shift = [jnp.sqrt(jnp.max(q_norm2[:, g * KV_GROUP:(g + 1) * KV_GROUP], axis=1, keepdims=True)
                      * k_norm2[:, g:g + 1]) * SHIFT_MARGIN for g in range(N_KV_HEADS)]
    q_stack = [jnp.concatenate([q_ref[:, h * HEAD_DIM:(h + 1) * HEAD_DIM]
                                for h in range(g * KV_GROUP, (g + 1) * KV_GROUP)], axis=0)
               for g in range(N_KV_HEADS)]

    def scores(ti):
        k_tile = k_ref[pl.ds(pl.multiple_of(ti * kt, kt), kt), :]
        return [_nt_dot(k_tile[:, g * HEAD_DIM:(g + 1) * HEAD_DIM], q_stack[g])
                for g in range(N_KV_HEADS)]

    def fast_tile(ti, ties_before, causal_cut):
        s_all = scores(ti)
        bias, ties_after = selection(ti, ties_before, causal_cut)
        v_aug = values(ti)
        for g in range(N_KV_HEADS):
            shifted = bias - shift[g]
            for hh in range(KV_GROUP):
                p = jnp.exp2(s_all[g][:, hh * qb:(hh + 1) * qb] + shifted)
                acc_ref[g * KV_GROUP + hh] += _dot(v_aug[g], p.astype(BF16))
        return ties_after

    def run(tile_fn):
        ties = lax.fori_loop(0, n_whole, functools.partial(tile_fn, causal_cut=False),
                             jnp.zeros((1, qb), F32))
        lax.fori_loop(n_whole, n_tiles, functools.partial(tile_fn, causal_cut=True), ties)

    denominator = lambda h: acc_ref[h][(1 - h // KV_GROUP) * HEAD_DIM:(1 - h // KV_GROUP) * HEAD_DIM + 1, :]
    acc_ref[...] = jnp.zeros_like(acc_ref)
    run(fast_tile)
    smallest = denominator(0)
    for h in range(1, N_HEADS):
        smallest = jnp.minimum(smallest, denominator(h))

    @pl.when(jnp.logical_not(jnp.min(smallest) >= SAFE_DENOMINATOR))
    def _():
        m_ref[...] = jnp.full_like(m_ref, MASK_VALUE)
        acc_ref[...] = jnp.zeros_like(acc_ref)
        def exact_tile(ti, ties_before, causal_cut):
            s_all = scores(ti)
            bias, ties_after = selection(ti, ties_before, causal_cut)
            v_aug = values(ti)
            for h in range(N_HEADS):
                g, hh = divmod(h, KV_GROUP)
                s = s_all[g][:, hh * qb:(hh + 1) * qb] + bias
                m_old = m_ref[h]
                m_new = jnp.maximum(m_old, jnp.max(s, axis=0, keepdims=True))
                p = jnp.exp2(s - m_new)
                acc_ref[h] = jnp.exp2(m_old - m_new) * acc_ref[h] + _dot(v_aug[g], p.astype(BF16))
                m_ref[h] = m_new
            return ties_after

        run(exact_tile)

    for pair in range(N_HEADS // 2):
        outs = []
        for h in (2 * pair, 2 * pair + 1):
            g = h // KV_GROUP
            outs.append(acc_ref[h][g * HEAD_DIM:(g + 1) * HEAD_DIM, :] / denominator(h))
        out_ref[:, pair * LANES:(pair + 1) * LANES] = (
            jnp.concatenate(outs, axis=0).T.astype(out_ref.dtype))


def _dsa(q, k, v, qi, ki, wi, batch, seq_len):
    n_tok = q.shape[0]
    qb = min(Q_TILE, seq_len)
    kt = min(KEY_TILE, seq_len)
    assert seq_len % qb == 0 and seq_len % kt == 0 and kt % qb == 0 and qb % LANES == 0
    assert N_KV_HEADS == 2 and KV_WIDTH == LANES and 2 * HEAD_DIM == LANES
    steps = seq_len // qb
    n_kt = seq_len // kt
    top_k = min(TOPK_MAX, seq_len // 4)
    qtile = lambda w: pl.BlockSpec((qb, w), lambda bi, qi_: (bi * steps + qi_, 0))
    seq = lambda w: pl.BlockSpec((seq_len, w), lambda bi, qi_: (bi, 0))
    return pl.pallas_call(
        functools.partial(_dsa_kernel, top_k=top_k),
        grid=(batch, steps),
        in_specs=[qtile(ATTN_WIDTH), seq(KV_WIDTH), seq(KV_WIDTH),
                  qtile(IDX_HEADS * IDX_CAT), seq(IDX_CAT), qtile(LANES)],
        out_specs=qtile(ATTN_WIDTH),
        out_shape=jax.ShapeDtypeStruct((n_tok, ATTN_WIDTH), BF16),
        scratch_shapes=[pltpu.VMEM((n_kt, kt, qb), jnp.int32),
                        pltpu.VMEM((n_kt, kt, qb), jnp.int16),
                        pltpu.VMEM((n_kt, kt, qb), jnp.int16),
                        pltpu.VMEM((SUBLANES, qb), jnp.int32),
                        pltpu.VMEM((SUBLANES, qb), F32),
                        pltpu.VMEM((SUBLANES, LANES), F32),
                        pltpu.VMEM((N_HEADS, 1, qb), F32),
                        pltpu.VMEM((N_HEADS, KV_WIDTH, qb), F32)],
        compiler_params=pltpu.CompilerParams(dimension_semantics=("arbitrary", "arbitrary"),
                                             vmem_limit_bytes=VMEM_LIMIT_BYTES),
        name="dsa",
    )(q, k, v, qi, ki, wi)


def _out_ffn_kernel(x_ref, yp_ref, yh_ref, ya_ref, wo_hbm, g2_ref, wfi_hbm, wfo_hbm, gf_ref,
                    out_ref, wo_ref, wfi_ref, wfo_ref, *, layer, final_norm):
    @pl.when(pl.program_id(0) == 0)
    def _():
        pltpu.sync_copy(wo_hbm.at[layer], wo_ref)
        pltpu.sync_copy(wfi_hbm.at[layer], wfi_ref)
        pltpu.sync_copy(wfo_hbm.at[layer], wfo_ref)

    x = x_ref[...]
    x = x + _dot(yp_ref[...], wo_ref[0:POOL_WIDTH, :])
    x = x + _dot(yh_ref[...], wo_ref[POOL_WIDTH:POOL_WIDTH + HGRN_WIDTH, :])
    x = x + _dot(ya_ref[...], wo_ref[POOL_WIDTH + HGRN_WIDTH:, :])
    h = _rms_norm(x, g2_ref[...]).astype(BF16)
    a = jnp.maximum(_dot(h, wfi_ref[...]), 0.0)
    x = x + _dot((a * a).astype(BF16), wfo_ref[...])
    if final_norm:
        x = _rms_norm(x, gf_ref[...])
    out_ref[...] = x


def _out_ffn(x2d, y_pool, y_hgrn, y_attn, w_out, norm2_g, w_ff_in, w_ff_out, final_g,
             layer, final_norm, seq_len):
    n_tok, d_model = x2d.shape
    d_ff = w_ff_in.shape[2]
    tm = min(PROJ_TILE, seq_len)
    tok = lambda w: pl.BlockSpec((tm, w), lambda i: (i, 0))
    hbm = pl.BlockSpec(memory_space=pl.ANY)
    return pl.pallas_call(
        functools.partial(_out_ffn_kernel, layer=layer, final_norm=final_norm),
        grid=(n_tok // tm,),
        in_specs=[tok(d_model), tok(POOL_WIDTH), tok(HGRN_WIDTH), tok(ATTN_WIDTH),
                  hbm, pl.BlockSpec((None, 1, d_model), lambda i: (layer, 0, 0)), hbm, hbm,
                  pl.BlockSpec((1, d_model), lambda i: (0, 0))],
        out_specs=tok(d_model),
        out_shape=jax.ShapeDtypeStruct((n_tok, d_model), F32),
        scratch_shapes=[pltpu.VMEM((d_model, d_model), BF16),
                        pltpu.VMEM((d_model, d_ff), BF16),
                        pltpu.VMEM((d_ff, d_model), BF16)],
        compiler_params=pltpu.CompilerParams(dimension_semantics=("arbitrary",),
                                             vmem_limit_bytes=VMEM_LIMIT_BYTES),
        name="out_ffn",
    )(x2d, y_pool, y_hgrn, y_attn, w_out, norm2_g, w_ff_in, w_ff_out, final_g)


def _rope_patterns(positions):
    inv_freq = ROPE_THETA ** (-jnp.arange(0, ROT_DIM, 2, dtype=F32) / ROT_DIM)
    rest = jnp.zeros((HEAD_DIM - ROT_DIM,), F32)
    freq_head = jnp.concatenate([inv_freq, inv_freq, rest])
    lo_head = jnp.concatenate([-jnp.ones((ROT_HALF,), F32), jnp.zeros((ROT_HALF,), F32), rest])
    hi_head = jnp.concatenate([jnp.zeros((ROT_HALF,), F32), jnp.ones((ROT_HALF,), F32), rest])
    two = lambda a: jnp.concatenate([a, a])[None, :]
    ang = positions.astype(F32).reshape(-1, 1) * two(freq_head)
    sin = jnp.sin(ang)
    return jnp.cos(ang), sin * two(lo_head), sin * two(hi_head)


def kernel(x, positions, norm1_g, w_in, pool_w, pool_scale, lb_logits, hgrn_norm_g, w_out,
           norm2_g, w_ff_in, w_ff_out, final_norm_g):
    batch, seq_len, d_model = x.shape
    depth = w_in.shape[0]
    assert w_in.shape[2] == D_IN

    p_lb = jax.nn.softmax(lb_logits.astype(F32), axis=0)
    lower_bounds = (jnp.cumsum(p_lb, axis=0) - p_lb[0])[:, None, :]
    w_in_pad = jnp.pad(w_in, ((0, 0), (0, 0), (0, D_IN_PAD - D_IN))).astype(BF16)
    groups = len(POOL_WINDOWS)
    eye = jnp.eye(groups, dtype=F32)
    pool_w_bd = (pool_w[:, :, :, None, :] * eye[None, :, None, :, None]).reshape(
        depth, POOL_WIDTH, POOL_WIDTH).astype(BF16)
    w_out_b, w_ff_in_b, w_ff_out_b = (w.astype(BF16) for w in (w_out, w_ff_in, w_ff_out))
    row = lambda a: a.reshape(depth, 1, -1).astype(F32)
    norm1, norm2, pscale, hnorm = row(norm1_g), row(norm2_g), row(pool_scale), row(hgrn_norm_g)
    final_g = final_norm_g.reshape(1, d_model).astype(F32)
    cos, slo, shi = _rope_patterns(positions)

    x2d = x.reshape(batch * seq_len, d_model)
    for layer in range(depth):
        y_pool, hg, q, k, v, qi, ki, wi = _in_proj(x2d, norm1, w_in_pad, cos, slo, shi, pool_w_bd,
                                                pscale, layer, seq_len)
        y_hgrn = _hgrn(hg, lower_bounds, hnorm, layer, batch, seq_len)
        y_attn = _dsa(q, k, v, qi, ki, wi, batch, seq_len)
        x2d = _out_ffn(x2d, y_pool, y_hgrn, y_attn, w_out_b, norm2, w_ff_in_b, w_ff_out_b,
                       final_g, layer, layer == depth - 1, seq_len)
    return x2d.reshape(batch, seq_len, d_model)
```

```python
import functools

import jax
import jax.numpy as jnp
from jax import lax
from jax.experimental import pallas as pl
from jax.experimental.pallas import tpu as pltpu

F32 = jnp.float32
BF16 = jnp.bfloat16

POOL_WINDOWS = (2, 4, 8, 16)
POOL_WIDTH = 256
POOL_GROUP_DIM = 64
HGRN_WIDTH = 256
HGRN_HEAD_DIM = 64
LOG_F_FLOOR = 1e-30
ATTN_WIDTH = 512
HEAD_DIM = 64
N_HEADS = 8
N_KV_HEADS = 2
KV_GROUP = N_HEADS // N_KV_HEADS
KV_WIDTH = N_KV_HEADS * HEAD_DIM
IDX_HEADS = 4
IDX_DIM = 64
TOPK_MAX = 256
MASK_VALUE = -1e30
ROPE_THETA = 500000.0
ROT_DIM = HEAD_DIM // 4
ROT_HALF = ROT_DIM // 2
RMS_EPS = 1e-5
IDX_W_SCALE = (IDX_HEADS ** -0.5) * (IDX_DIM ** -0.5)
ATTN_SCALE = HEAD_DIM ** -0.5
LOG2_E = 1.4426950408889634
SHIFT_MARGIN = 1.02
SAFE_DENOMINATOR = 2.0 ** -60

LANES = 128
SUBLANES = 8
VMEM_LIMIT_BYTES = 56 * 1024 * 1024

COL_POOL = 0
COL_HGRN = COL_POOL + POOL_WIDTH
COL_Q = COL_HGRN + 4 * HGRN_WIDTH
COL_K = COL_Q + ATTN_WIDTH
COL_V = COL_K + KV_WIDTH
COL_QI = COL_V + KV_WIDTH
COL_KI = COL_QI + IDX_HEADS * IDX_DIM
D_IN = COL_KI + IDX_DIM + IDX_HEADS
D_IN_PAD = COL_KI + LANES
IDX_CAT = 4 * IDX_DIM

PROJ_TILE = 1024
MLP_TILE = 512
PROJ_SUB = 128
HGRN_TILE = 512
HGRN_CHUNK = 16
Q_TILE = 256
KEY_TILE = 256
INT_MIN = -2 ** 31


def _nt_dot(a, b):
    return lax.dot_general(a, b, (((1,), (1,)), ((), ())), preferred_element_type=F32)


def _tn_dot(a, b):
    return lax.dot_general(a, b, (((0,), (0,)), ((), ())), preferred_element_type=F32)


def _dot(a, b):
    return jnp.dot(a, b, preferred_element_type=F32)


def _split_bf16(a):
    hi = a.astype(BF16)
    lo = (a - hi.astype(F32)).astype(BF16)
    return hi, lo


def _rms_norm(x, g):
    return x * lax.rsqrt(jnp.mean(x * x, axis=-1, keepdims=True) + RMS_EPS) * g


def _silu(x):
    return x * jax.nn.sigmoid(x)


def _rope(x, cos, sin_lo, sin_hi):
    w = x.shape[1]
    rep = w // LANES
    if rep > 1:
        cos = jnp.concatenate([cos] * rep, axis=1)
        sin_lo = jnp.concatenate([sin_lo] * rep, axis=1)
        sin_hi = jnp.concatenate([sin_hi] * rep, axis=1)
    from_below = pltpu.roll(x, ROT_HALF, 1)
    from_above = pltpu.roll(x, w - ROT_HALF, 1)
    return x * cos + from_below * sin_hi + from_above * sin_lo


def _in_proj_kernel(x_ref, g_ref, w_ref, cos_ref, slo_ref, shi_ref, pw_ref, pscale_ref,
                    ypool_ref, hg_ref, q_ref, k_ref, v_ref, qi_ref, ki_ref, wi_ref,
                    halo_ref, *, tiles_per_seq):
    tm = x_ref.shape[0]
    sub = min(PROJ_SUB, tm)
    seq_tile = pl.program_id(0) % tiles_per_seq

    @pl.when(seq_tile == 0)
    def _():
        halo_ref[...] = jnp.zeros_like(halo_ref)

    prev = halo_ref[...]
    for r in range(tm // sub):
        prev = _in_proj_rows(slice(r * sub, (r + 1) * sub), seq_tile * tm + r * sub, prev,
                             x_ref, g_ref, w_ref, cos_ref, slo_ref, shi_ref, pw_ref, pscale_ref,
                             ypool_ref, hg_ref, q_ref, k_ref, v_ref, qi_ref, ki_ref, wi_ref)
    halo_ref[...] = prev


def _in_proj_rows(rows, t_start, prev, x_ref, g_ref, w_ref, cos_ref, slo_ref, shi_ref, pw_ref,
                  pscale_ref, ypool_ref, hg_ref, q_ref, k_ref, v_ref, qi_ref, ki_ref, wi_ref):
    tm = rows.stop - rows.start
    halo = prev.shape[0]
    h = _rms_norm(x_ref[rows, :], g_ref[...])
    proj = _dot(h.astype(BF16), w_ref[...])

    u = proj[:, COL_POOL:COL_POOL + POOL_WIDTH]
    ext = jnp.concatenate([prev, u], axis=0)
    sums = []
    acc = ext
    span = 1
    for win in POOL_WINDOWS:
        while span < win:
            acc = acc + pltpu.roll(acc, span, 0)
            span *= 2
        sums.append(acc[halo:, :])
    lane = lax.broadcasted_iota(jnp.int32, (tm, POOL_WIDTH), 1)
    group = lane // POOL_GROUP_DIM
    wsum = sums[-1]
    win_lane = jnp.full((tm, POOL_WIDTH), POOL_WINDOWS[-1], jnp.int32)
    for gi in range(len(POOL_WINDOWS) - 2, -1, -1):
        wsum = jnp.where(group == gi, sums[gi], wsum)
        win_lane = jnp.where(group == gi, POOL_WINDOWS[gi], win_lane)
    t_pos = t_start + lax.broadcasted_iota(jnp.int32, (tm, POOL_WIDTH), 0)
    count = jnp.minimum(t_pos + 1, win_lane).astype(F32)
    pooled = wsum / count - u
    y_pool = _dot(pooled.astype(BF16), pw_ref[...]) * pscale_ref[...]
    ypool_ref[rows, :] = y_pool.astype(ypool_ref.dtype)

    hg_ref[rows, :] = proj[:, COL_HGRN:COL_HGRN + 4 * HGRN_WIDTH]

    cos, slo, shi = cos_ref[rows, :], slo_ref[rows, :], shi_ref[rows, :]
    q = _rope(proj[:, COL_Q:COL_Q + ATTN_WIDTH], cos, slo, shi) * (ATTN_SCALE * LOG2_E)
    q_ref[rows, :] = q.astype(q_ref.dtype)
    k = _rope(proj[:, COL_K:COL_K + KV_WIDTH], cos, slo, shi)
    k_ref[rows, :] = k.astype(k_ref.dtype)
    v_ref[rows, :] = proj[:, COL_V:COL_V + KV_WIDTH].astype(v_ref.dtype)
    qi = _rope(proj[:, COL_QI:COL_QI + IDX_HEADS * IDX_DIM], cos, slo, shi)
    qi_hi = qi.astype(BF16).astype(F32)
    qi_lo = qi - qi_hi
    zeros = jnp.zeros((tm, IDX_DIM), F32)
    pieces = []
    for hd in range(IDX_HEADS):
        cols = slice(hd * IDX_DIM, (hd + 1) * IDX_DIM)
        pieces += [qi_hi[:, cols], qi_hi[:, cols], qi_lo[:, cols], zeros]
    qi_ref[rows, :] = jnp.concatenate(pieces, axis=1).astype(qi_ref.dtype)
    kiw_raw = proj[:, COL_KI:COL_KI + LANES]
    ki = _rope(kiw_raw, cos, slo, shi)[:, 0:IDX_DIM]
    ki_hi = ki.astype(BF16).astype(F32)
    ki_ref[rows, :] = jnp.concatenate([ki_hi, ki - ki_hi, ki_hi, zeros], axis=1).astype(ki_ref.dtype)
    wi_ref[rows, :] = kiw_raw * IDX_W_SCALE
    return u[tm - halo:, :]


def _in_proj(x2d, norm_g, w_in_pad, cos, slo, shi, pool_w_bd, pool_scale, layer, seq_len):
    n_tok, d_model = x2d.shape
    tm = min(PROJ_TILE, seq_len)
    assert seq_len % tm == 0 and n_tok % tm == 0
    halo = 2 * SUBLANES
    assert halo >= POOL_WINDOWS[-1] and tm >= halo
    tok = lambda w: pl.BlockSpec((tm, w), lambda i: (i, 0))
    lay = lambda *s: pl.BlockSpec((None,) + s, lambda i: (layer,) + (0,) * len(s))
    out_shape = (
        jax.ShapeDtypeStruct((n_tok, POOL_WIDTH), BF16),
        jax.ShapeDtypeStruct((n_tok, 4 * HGRN_WIDTH), F32),
        jax.ShapeDtypeStruct((n_tok, ATTN_WIDTH), BF16),
        jax.ShapeDtypeStruct((n_tok, KV_WIDTH), BF16),
        jax.ShapeDtypeStruct((n_tok, KV_WIDTH), BF16),
        jax.ShapeDtypeStruct((n_tok, IDX_HEADS * IDX_CAT), BF16),
        jax.ShapeDtypeStruct((n_tok, IDX_CAT), BF16),
        jax.ShapeDtypeStruct((n_tok, LANES), F32),
    )
    return pl.pallas_call(
        functools.partial(_in_proj_kernel, tiles_per_seq=seq_len // tm),
        grid=(n_tok // tm,),
        in_specs=[tok(d_model), lay(1, d_model), lay(d_model, D_IN_PAD),
                  tok(LANES), tok(LANES), tok(LANES),
                  lay(POOL_WIDTH, POOL_WIDTH), lay(1, POOL_WIDTH)],
        out_specs=tuple(tok(s.shape[1]) for s in out_shape),
        out_shape=out_shape,
        scratch_shapes=[pltpu.VMEM((halo, POOL_WIDTH), F32)],
        compiler_params=pltpu.CompilerParams(dimension_semantics=("arbitrary",),
                                             vmem_limit_bytes=VMEM_LIMIT_BYTES),
        name="in_proj",
    )(x2d, norm_g, w_in_pad, cos, slo, shi, pool_w_bd, pool_scale)


def _hgrn_kernel(hg_ref, lb_ref, ng_ref, out_ref, state_ref, o_ref):
    tb, width = out_ref.shape
    c = HGRN_CHUNK
    half = c // 2
    n_chunks = tb // c
    heads = width // HGRN_HEAD_DIM

    @pl.when(pl.program_id(1) == 0)
    def _():
        state_ref[...] = jnp.zeros_like(state_ref)

    lb = lb_ref[...]
    q = _silu(hg_ref[:, 0:width])
    sig = jax.nn.sigmoid(hg_ref[:, width:2 * width])
    f = lb + (1.0 - lb) * sig
    log2_f = jnp.log(jnp.maximum(f, LOG_F_FLOOR)) * LOG2_E
    k = (1.0 - lb) * (1.0 - sig)
    v = hg_ref[:, 2 * width:3 * width]

    row_in_chunk = lax.broadcasted_iota(jnp.int32, (tb, width), 0) % c
    b = log2_f
    step = 1
    while step < c:
        b = b + jnp.where(row_in_chunk >= step, pltpu.roll(b, step, 0), 0.0)
        step *= 2

    lane_r = lax.broadcasted_iota(jnp.int32, (width, width), 0) // HGRN_HEAD_DIM
    lane_c = lax.broadcasted_iota(jnp.int32, (width, width), 1) // HGRN_HEAD_DIM
    head_sum = jnp.where(lane_r == lane_c, 1.0, 0.0).astype(BF16)

    tiles = lambda a: a.reshape(tb // half, half, width)
    q3, k3, b3, v3 = tiles(q), tiles(k), tiles(b), tiles(v)
    row_in_tile = lax.broadcasted_iota(jnp.int32, (tb // half, half, width), 1)
    o_near = None
    for s in range(half):
        if s == 0:
            p, vs = q3 * k3, v3
        else:
            ks, bs, vs = (pltpu.roll(a, s, 1) for a in (k3, b3, v3))
            p = jnp.where(row_in_tile >= s, q3 * ks * jnp.exp2(b3 - bs), 0.0)
        score = _dot(p.reshape(tb, width).astype(BF16), head_sum)
        term = score * vs.reshape(tb, width)
        o_near = term if o_near is None else o_near + term
    o_ref[...] = o_near
    first = lambda a: jnp.concatenate([a[ci * c:ci * c + half] for ci in range(n_chunks)], axis=0)
    second = lambda a: jnp.concatenate([a[ci * c + half:(ci + 1) * c] for ci in range(n_chunks)], axis=0)
    hb = tb // 2
    b1, b2 = first(b), second(b)
    b_mid = jnp.concatenate([jnp.broadcast_to(b1[(ci + 1) * half - 1:(ci + 1) * half], (half, width))
                             for ci in range(n_chunks)], axis=0)
    q_mid = second(q) * jnp.exp2(b2 - b_mid)
    k_mid = (first(k) * jnp.exp2(b_mid - b1)).astype(BF16)
    head_of_lane = lax.broadcasted_iota(jnp.int32, (hb, width), 1) // HGRN_HEAD_DIM
    q_heads = jnp.concatenate([jnp.where(head_of_lane == hd, q_mid, 0.0) for hd in range(heads)], axis=0)
    scores = _nt_dot(q_heads.astype(BF16), k_mid)
    t_chunk = (lax.broadcasted_iota(jnp.int32, (heads * hb, hb), 0) % hb) // half
    s_chunk = lax.broadcasted_iota(jnp.int32, (heads * hb, hb), 1) // half
    scores = jnp.where(t_chunk == s_chunk, scores, 0.0).astype(BF16)
    o_heads = _dot(scores, first(v).astype(BF16))
    o_far = jnp.where(head_of_lane == 0, o_heads[0:hb], 0.0)
    for hd in range(1, heads):
        o_far = jnp.where(head_of_lane == hd, o_heads[hd * hb:(hd + 1) * hb], o_far)
    for ci in range(n_chunks):
        o_ref[ci * c + half:(ci + 1) * c, :] += o_far[ci * half:(ci + 1) * half]

    lane_head = lax.broadcasted_iota(jnp.int32, (c, width), 1) // HGRN_HEAD_DIM
    for ci in range(n_chunks):
        rows = slice(ci * c, (ci + 1) * c)
        bc = b[rows]
        b_last = bc[c - 1:c]
        state = state_ref[...]
        q_dec = q[rows] * jnp.exp2(bc)
        q_heads = jnp.concatenate([jnp.where(lane_head == hd, q_dec, 0.0) for hd in range(heads)], axis=0)
        o_heads = _nt_dot(q_heads.astype(BF16), state.astype(BF16))
        o_inter = jnp.where(lane_head == 0, o_heads[0:c], 0.0)
        for hd in range(1, heads):
            o_inter = jnp.where(lane_head == hd, o_heads[hd * c:(hd + 1) * c], o_inter)
        o_ref[rows, :] += o_inter
        k_dec = k[rows] * jnp.exp2(b_last - bc)
        upd = _tn_dot(v[rows].astype(BF16), k_dec.astype(BF16))
        state_ref[...] = state * jnp.exp2(b_last) + upd

    o = o_ref[...]
    sq_hi, sq_lo = _split_bf16(o * o)
    mean_sq = (_dot(sq_hi, head_sum) + _dot(sq_lo, head_sum)) * (1.0 / HGRN_HEAD_DIM)
    gate = _silu(hg_ref[:, 3 * width:4 * width])
    out_ref[...] = (o * lax.rsqrt(mean_sq + RMS_EPS) * ng_ref[...] * gate).astype(out_ref.dtype)


def _hgrn(hg, lower_bound, norm_g, layer, batch, seq_len):
    n_tok = hg.shape[0]
    tb = min(HGRN_TILE, seq_len)
    assert seq_len % tb == 0 and tb % HGRN_CHUNK == 0 and HGRN_CHUNK == 2 * SUBLANES
    steps = seq_len // tb
    lay = pl.BlockSpec((None, 1, HGRN_WIDTH), lambda bi, ti: (layer, 0, 0))
    return pl.pallas_call(
        _hgrn_kernel,
        grid=(batch, steps),
        in_specs=[pl.BlockSpec((tb, 4 * HGRN_WIDTH), lambda bi, ti: (bi * steps + ti, 0)),
                  lay, lay],
        out_specs=pl.BlockSpec((tb, HGRN_WIDTH), lambda bi, ti: (bi * steps + ti, 0)),
        out_shape=jax.ShapeDtypeStruct((n_tok, HGRN_WIDTH), BF16),
        scratch_shapes=[pltpu.VMEM((HGRN_WIDTH, HGRN_WIDTH), F32),
                        pltpu.VMEM((tb, HGRN_WIDTH), F32)],
        compiler_params=pltpu.CompilerParams(dimension_semantics=("arbitrary", "arbitrary"),
                                             vmem_limit_bytes=VMEM_LIMIT_BYTES),
        name="hgrn2",
    )(hg, lower_bound, norm_g)


def _dsa_kernel(q_ref, k_ref, v_ref, qi_ref, ki_ref, wi_ref, out_ref,
                key_ref, hi_ref, lo_ref, thr_ref, need_ref, knorm_ref, m_ref, acc_ref, *, top_k):
    qb = q_ref.shape[0]
    kt = key_ref.shape[1]
    i16 = jnp.int16
    rows16 = 2 * SUBLANES
    span = 4 * rows16
    tall = lambda a: jnp.concatenate([a] * (span // rows16), axis=0)
    t0 = pl.program_id(1) * qb
    n_tiles = (t0 + qb + kt - 1) // kt
    n_whole = (t0 + 1) // kt

    key_pos = lax.broadcasted_iota(jnp.int32, (kt, qb), 0)
    query_pos = t0 + lax.broadcasted_iota(jnp.int32, (kt, qb), 1)

    @pl.when(pl.program_id(1) == 0)
    def _():
        kv_cols = (lax.broadcasted_iota(jnp.int32, (KV_WIDTH, LANES), 0) // HEAD_DIM
                   == lax.broadcasted_iota(jnp.int32, (KV_WIDTH, LANES), 1))
        kv_cols = jnp.where(kv_cols, 1.0, 0.0).astype(BF16)

        def key_norms(ti, best):
            k_f32 = k_ref[pl.ds(pl.multiple_of(ti * kt, kt), kt), :].astype(F32)
            kk_hi, kk_lo = _split_bf16(k_f32 * k_f32)
            norm2 = _dot(kk_hi, kv_cols) + _dot(kk_lo, kv_cols)
            return jnp.maximum(best, jnp.max(norm2, axis=0, keepdims=True))

        best = lax.fori_loop(0, k_ref.shape[0] // kt, key_norms, jnp.zeros((1, LANES), F32))
        knorm_ref[...] = jnp.broadcast_to(best, knorm_ref.shape)

    wi_t = wi_ref[...].T
    w_heads = [wi_t[IDX_DIM + h:IDX_DIM + h + 1, :] for h in range(IDX_HEADS)]

    def index_tile(ti, carry, causal_cut):
        base = pl.multiple_of(ti * kt, kt)
        ki_tile = ki_ref[pl.ds(base, kt), :]
        score = jnp.zeros((kt, qb), F32)
        for h in range(IDX_HEADS):
            s = _nt_dot(ki_tile, qi_ref[:, h * IDX_CAT:(h + 1) * IDX_CAT])
            score = score + jnp.maximum(s, 0.0) * w_heads[h]
        score = jnp.where(score == 0.0, 0.0, score)
        if causal_cut:
            score = jnp.where(base + key_pos <= query_pos, score, MASK_VALUE)
        bits = lax.bitcast_convert_type(score, jnp.int32)
        key = bits ^ ((bits >> 31) & 0x7FFFFFFF)
        key_ref[ti] = key
        hi_ref[ti] = (key >> 16).astype(i16)
        lo_ref[ti] = ((key & 0xFFFF) - 2 ** 15).astype(i16)
        return carry

    lax.fori_loop(0, n_whole, functools.partial(index_tile, causal_cut=False), 0)
    lax.fori_loop(n_whole, n_tiles, functools.partial(index_tile, causal_cut=True), 0)

    def count(ref, pred):
        def body(ti, acc):
            keys = ref[ti]
            for r in range(kt // span):
                acc = acc + jnp.where(pred(keys[r * span:(r + 1) * span, :]), i16(1), i16(0))
            return acc
        partial = lax.fori_loop(0, n_tiles, body, jnp.zeros((span, qb), i16))
        return jnp.sum(partial.astype(F32), axis=0, keepdims=True)

    def kth_largest(ref, target):
        def bit(it, state):
            thr, n_above = state
            cand = thr + lax.shift_left(jnp.int32(1), 15 - it)
            cand16 = tall(cand.astype(i16))
            n_ge = jnp.broadcast_to(count(ref, lambda keys: keys >= cand16), thr.shape)
            ok = n_ge >= target
            return jnp.where(ok, cand, thr), jnp.where(ok, n_above, n_ge)
        return lax.fori_loop(0, 16, bit, (jnp.full((rows16, qb), -2 ** 15, jnp.int32),
                                          jnp.zeros((rows16, qb), F32)))

    thr_ref[...] = jnp.full_like(thr_ref, INT_MIN)
    need_ref[...] = jnp.zeros_like(need_ref)

    @pl.when(t0 + qb > top_k)
    def _():
        upper, n_upper_gt = kth_largest(hi_ref, float(top_k))
        upper_kt = jnp.concatenate([tall(upper.astype(i16))] * (kt // span), axis=0)

        def keep_bucket(ti, carry):
            lo_ref[ti] = jnp.where(hi_ref[ti] == upper_kt, lo_ref[ti], i16(-2 ** 15))
            return carry

        lax.fori_loop(0, n_tiles, keep_bucket, 0)
        lower, n_lower_gt = kth_largest(lo_ref, top_k - n_upper_gt)
        thr_ref[...] = (lax.shift_left(upper, 16) + (lower + 2 ** 15))[0:SUBLANES]
        need_ref[...] = (top_k - (n_upper_gt + n_lower_gt))[0:SUBLANES]

    thr = thr_ref[0:1, :]
    need = need_ref[0:1, :]

    earlier = (lax.broadcasted_iota(jnp.int32, (kt, kt), 1)
               < lax.broadcasted_iota(jnp.int32, (kt, kt), 0))
    earlier_bf = jnp.where(earlier, 1.0, 0.0).astype(BF16)
    ones_rows = jnp.ones((SUBLANES, kt), BF16)
    dim_v = lax.broadcasted_iota(jnp.int32, (KV_WIDTH, kt), 0)

    def selection(ti, ties_before, causal_cut):
        base = pl.multiple_of(ti * kt, kt)
        keys = key_ref[ti]
        tie = keys == thr
        tie_bf = jnp.where(tie, 1.0, 0.0).astype(BF16)
        rank = _dot(earlier_bf, tie_bf) + ties_before
        chosen = (keys > thr) | (tie & (rank < need))
        if causal_cut:
            chosen = chosen & (base + key_pos <= query_pos)
        return jnp.where(chosen, 0.0, MASK_VALUE), ties_before + _dot(ones_rows, tie_bf)[0:1]

    def values(ti):
        v_t = v_ref[pl.ds(pl.multiple_of(ti * kt, kt), kt), :].astype(F32).T
        return [jnp.where(dim_v // HEAD_DIM == g, v_t, 1.0).astype(BF16) for g in range(N_KV_HEADS)]

    head_cols = (lax.broadcasted_iota(jnp.int32, (ATTN_WIDTH, LANES), 0) // HEAD_DIM
                 == lax.broadcasted_iota(jnp.int32, (ATTN_WIDTH, LANES), 1))
    head_cols = jnp.where(head_cols, 1.0, 0.0).astype(BF16)
    q_sq = q_ref[...] * q_ref[...]
    q_norm2 = jnp.max(_dot(q_sq, head_cols), axis=0, keepdims=True)
    k_norm2 = knorm_ref[0:1, :]
    shift = [jnp.sqrt(jnp.max(q_norm2[:, g * KV_GROUP:(g + 1) * KV_GROUP], axis=1, keepdims=True)
                      * k_norm2[:, g:g + 1]) * SHIFT_MARGIN for g in range(N_KV_HEADS)]
    q_stack = [jnp.concatenate([q_ref[:, h * HEAD_DIM:(h + 1) * HEAD_DIM]
                                for h in range(g * KV_GROUP, (g + 1) * KV_GROUP)], axis=0)
               for g in range(N_KV_HEADS)]

    def scores(ti):
        k_tile = k_ref[pl.ds(pl.multiple_of(ti * kt, kt), kt), :]
        return [_nt_dot(k_tile[:, g * HEAD_DIM:(g + 1) * HEAD_DIM], q_stack[g])
                for g in range(N_KV_HEADS)]

    def fast_tile(ti, ties_before, causal_cut):
        s_all = scores(ti)
        bias, ties_after = selection(ti, ties_before, causal_cut)
        v_aug = values(ti)
        for g in range(N_KV_HEADS):
            shifted = bias - shift[g]
            for hh in range(KV_GROUP):
                p = jnp.exp2(s_all[g][:, hh * qb:(hh + 1) * qb] + shifted)
                acc_ref[g * KV_GROUP + hh] += _dot(v_aug[g], p.astype(BF16))
        return ties_after

    def run(tile_fn):
        ties = lax.fori_loop(0, n_whole, functools.partial(tile_fn, causal_cut=False),
                             jnp.zeros((1, qb), F32))
        lax.fori_loop(n_whole, n_tiles, functools.partial(tile_fn, causal_cut=True), ties)

    denominator = lambda h: acc_ref[h][(1 - h // KV_GROUP) * HEAD_DIM:(1 - h // KV_GROUP) * HEAD_DIM + 1, :]
    acc_ref[...] = jnp.zeros_like(acc_ref)
    run(fast_tile)
    smallest = denominator(0)
    for h in range(1, N_HEADS):
        smallest = jnp.minimum(smallest, denominator(h))

    @pl.when(jnp.logical_not(jnp.min(smallest) >= SAFE_DENOMINATOR))
    def _():
        m_ref[...] = jnp.full_like(m_ref, MASK_VALUE)
        acc_ref[...] = jnp.zeros_like(acc_ref)
        def exact_tile(ti, ties_before, causal_cut):
            s_all = scores(ti)
            bias, ties_after = selection(ti, ties_before, causal_cut)
            v_aug = values(ti)
            for h in range(N_HEADS):
                g, hh = divmod(h, KV_GROUP)
                s = s_all[g][:, hh * qb:(hh + 1) * qb] + bias
                m_old = m_ref[h]
                m_new = jnp.maximum(m_old, jnp.max(s, axis=0, keepdims=True))
                p = jnp.exp2(s - m_new)
                acc_ref[h] = jnp.exp2(m_old - m_new) * acc_ref[h] + _dot(v_aug[g], p.astype(BF16))
                m_ref[h] = m_new
            return ties_after

        run(exact_tile)

    for pair in range(N_HEADS // 2):
        outs = []
        for h in (2 * pair, 2 * pair + 1):
            g = h // KV_GROUP
            outs.append(acc_ref[h][g * HEAD_DIM:(g + 1) * HEAD_DIM, :] / denominator(h))
        out_ref[:, pair * LANES:(pair + 1) * LANES] = (
            jnp.concatenate(outs, axis=0).T.astype(out_ref.dtype))


def _dsa(q, k, v, qi, ki, wi, batch, seq_len):
    n_tok = q.shape[0]
    qb = min(Q_TILE, seq_len)
    kt = min(KEY_TILE, seq_len)
    assert seq_len % qb == 0 and seq_len % kt == 0 and kt % qb == 0 and qb % LANES == 0
    assert N_KV_HEADS == 2 and KV_WIDTH == LANES and 2 * HEAD_DIM == LANES
    steps = seq_len // qb
    n_kt = seq_len // kt
    top_k = min(TOPK_MAX, seq_len // 4)
    qtile = lambda w: pl.BlockSpec((qb, w), lambda bi, qi_: (bi * steps + qi_, 0))
    seq = lambda w: pl.BlockSpec((seq_len, w), lambda bi, qi_: (bi, 0))
    return pl.pallas_call(
        functools.partial(_dsa_kernel, top_k=top_k),
        grid=(batch, steps),
        in_specs=[qtile(ATTN_WIDTH), seq(KV_WIDTH), seq(KV_WIDTH),
                  qtile(IDX_HEADS * IDX_CAT), seq(IDX_CAT), qtile(LANES)],
        out_specs=qtile(ATTN_WIDTH),
        out_shape=jax.ShapeDtypeStruct((n_tok, ATTN_WIDTH), BF16),
        scratch_shapes=[pltpu.VMEM((n_kt, kt, qb), jnp.int32),
                        pltpu.VMEM((n_kt, kt, qb), jnp.int16),
                        pltpu.VMEM((n_kt, kt, qb), jnp.int16),
                        pltpu.VMEM((SUBLANES, qb), jnp.int32),
                        pltpu.VMEM((SUBLANES, qb), F32),
                        pltpu.VMEM((SUBLANES, LANES), F32),
                        pltpu.VMEM((N_HEADS, 1, qb), F32),
                        pltpu.VMEM((N_HEADS, KV_WIDTH, qb), F32)],
        compiler_params=pltpu.CompilerParams(dimension_semantics=("arbitrary", "arbitrary"),
                                             vmem_limit_bytes=VMEM_LIMIT_BYTES),
        name="dsa",
    )(q, k, v, qi, ki, wi)


def _out_ffn_kernel(x_ref, yp_ref, yh_ref, ya_ref, wo_hbm, g2_ref, wfi_hbm, wfo_hbm, gf_ref,
                    out_ref, wo_ref, wfi_ref, wfo_ref, *, layer, final_norm):
    @pl.when(pl.program_id(0) == 0)
    def _():
        pltpu.sync_copy(wo_hbm.at[layer], wo_ref)
        pltpu.sync_copy(wfi_hbm.at[layer], wfi_ref)
        pltpu.sync_copy(wfo_hbm.at[layer], wfo_ref)

    x = x_ref[...]
    x = x + _dot(yp_ref[...], wo_ref[0:POOL_WIDTH, :])
    x = x + _dot(yh_ref[...], wo_ref[POOL_WIDTH:POOL_WIDTH + HGRN_WIDTH, :])
    x = x + _dot(ya_ref[...], wo_ref[POOL_WIDTH + HGRN_WIDTH:, :])
    h = _rms_norm(x, g2_ref[...]).astype(BF16)
    a = jnp.maximum(_dot(h, wfi_ref[...]), 0.0)
    x = x + _dot((a * a).astype(BF16), wfo_ref[...])
    if final_norm:
        x = _rms_norm(x, gf_ref[...])
    out_ref[...] = x


def _out_ffn(x2d, y_pool, y_hgrn, y_attn, w_out, norm2_g, w_ff_in, w_ff_out, final_g,
             layer, final_norm, seq_len):
    n_tok, d_model = x2d.shape
    d_ff = w_ff_in.shape[2]
    tm = min(MLP_TILE, seq_len)
    assert seq_len % tm == 0
    tok = lambda w: pl.BlockSpec((tm, w), lambda i: (i, 0))
    hbm = pl.BlockSpec(memory_space=pl.ANY)
    return pl.pallas_call(
        functools.partial(_out_ffn_kernel, layer=layer, final_norm=final_norm),
        grid=(n_tok // tm,),
        in_specs=[tok(d_model), tok(POOL_WIDTH), tok(HGRN_WIDTH), tok(ATTN_WIDTH),
                  hbm, pl.BlockSpec((None, 1, d_model), lambda i: (layer, 0, 0)), hbm, hbm,
                  pl.BlockSpec((1, d_model), lambda i: (0, 0))],
        out_specs=tok(d_model),
        out_shape=jax.ShapeDtypeStruct((n_tok, d_model), F32),
        scratch_shapes=[pltpu.VMEM((d_model, d_model), BF16),
                        pltpu.VMEM((d_model, d_ff), BF16),
                        pltpu.VMEM((d_ff, d_model), BF16)],
        compiler_params=pltpu.CompilerParams(dimension_semantics=("arbitrary",),
                                             vmem_limit_bytes=VMEM_LIMIT_BYTES),
        name="out_ffn",
    )(x2d, y_pool, y_hgrn, y_attn, w_out, norm2_g, w_ff_in, w_ff_out, final_g)


def _rope_patterns(positions):
    inv_freq = ROPE_THETA ** (-jnp.arange(0, ROT_DIM, 2, dtype=F32) / ROT_DIM)
    rest = jnp.zeros((HEAD_DIM - ROT_DIM,), F32)
    freq_head = jnp.concatenate([inv_freq, inv_freq, rest])
    lo_head = jnp.concatenate([-jnp.ones((ROT_HALF,), F32), jnp.zeros((ROT_HALF,), F32), rest])
    hi_head = jnp.concatenate([jnp.zeros((ROT_HALF,), F32), jnp.ones((ROT_HALF,), F32), rest])
    two = lambda a: jnp.concatenate([a, a])[None, :]
    ang = positions.astype(F32).reshape(-1, 1) * two(freq_head)
    sin = jnp.sin(ang)
    return jnp.cos(ang), sin * two(lo_head), sin * two(hi_head)


def kernel(x, positions, norm1_g, w_in, pool_w, pool_scale, lb_logits, hgrn_norm_g, w_out,
           norm2_g, w_ff_in, w_ff_out, final_norm_g):
    batch, seq_len, d_model = x.shape
    depth = w_in.shape[0]
    assert w_in.shape[2] == D_IN

    p_lb = jax.nn.softmax(lb_logits.astype(F32), axis=0)
    lower_bounds = (jnp.cumsum(p_lb, axis=0) - p_lb[0])[:, None, :]
    w_in_pad = jnp.pad(w_in, ((0, 0), (0, 0), (0, D_IN_PAD - D_IN))).astype(BF16)
    groups = len(POOL_WINDOWS)
    eye = jnp.eye(groups, dtype=F32)
    pool_w_bd = (pool_w[:, :, :, None, :] * eye[None, :, None, :, None]).reshape(
        depth, POOL_WIDTH, POOL_WIDTH).astype(BF16)
    w_out_b, w_ff_in_b, w_ff_out_b = (w.astype(BF16) for w in (w_out, w_ff_in, w_ff_out))
    row = lambda a: a.reshape(depth, 1, -1).astype(F32)
    norm1, norm2, pscale, hnorm = row(norm1_g), row(norm2_g), row(pool_scale), row(hgrn_norm_g)
    final_g = final_norm_g.reshape(1, d_model).astype(F32)
    cos, slo, shi = _rope_patterns(positions)

    x2d = x.reshape(batch * seq_len, d_model)
    for layer in range(depth):
        y_pool, hg, q, k, v, qi, ki, wi = _in_proj(x2d, norm1, w_in_pad, cos, slo, shi, pool_w_bd,
                                                pscale, layer, seq_len)
        y_hgrn = _hgrn(hg, lower_bounds, hnorm, layer, batch, seq_len)
        y_attn = _dsa(q, k, v, qi, ki, wi, batch, seq_len)
        x2d = _out_ffn(x2d, y_pool, y_hgrn, y_attn, w_out_b, norm2, w_ff_in_b, w_ff_out_b,
                       final_g, layer, layer == depth - 1, seq_len)
    return x2d.reshape(batch, seq_len, d_model)
```

```python
import functools

import jax
import jax.numpy as jnp
from jax import lax
from jax.experimental import pallas as pl
from jax.experimental.pallas import tpu as pltpu

F32 = jnp.float32
BF16 = jnp.bfloat16

POOL_WINDOWS = (2, 4, 8, 16)
POOL_WIDTH = 256
POOL_GROUP_DIM = 64
HGRN_WIDTH = 256
HGRN_HEAD_DIM = 64
LOG_F_FLOOR = 1e-30
ATTN_WIDTH = 512
HEAD_DIM = 64
N_HEADS = 8
N_KV_HEADS = 2
KV_GROUP = N_HEADS // N_KV_HEADS
KV_WIDTH = N_KV_HEADS * HEAD_DIM
IDX_HEADS = 4
IDX_DIM = 64
TOPK_MAX = 256
MASK_VALUE = -1e30
ROPE_THETA = 500000.0
ROT_DIM = HEAD_DIM // 4
ROT_HALF = ROT_DIM // 2
RMS_EPS = 1e-5
IDX_W_SCALE = (IDX_HEADS ** -0.5) * (IDX_DIM ** -0.5)
ATTN_SCALE = HEAD_DIM ** -0.5
LOG2_E = 1.4426950408889634
SHIFT_MARGIN = 1.02
SAFE_DENOMINATOR = 2.0 ** -60

LANES = 128
SUBLANES = 8
VMEM_LIMIT_BYTES = 56 * 1024 * 1024

COL_POOL = 0
COL_HGRN = COL_POOL + POOL_WIDTH
COL_Q = COL_HGRN + 4 * HGRN_WIDTH
COL_K = COL_Q + ATTN_WIDTH
COL_V = COL_K + KV_WIDTH
COL_QI = COL_V + KV_WIDTH
COL_KI = COL_QI + IDX_HEADS * IDX_DIM
D_IN = COL_KI + IDX_DIM + IDX_HEADS
D_IN_PAD = COL_KI + LANES
IDX_CAT = 4 * IDX_DIM

PROJ_TILE = 1024
MLP_TILE = 512
PROJ_SUB = 128
HGRN_TILE = 512
HGRN_CHUNK = 16
Q_TILE = 512
KEY_TILE = 256
INT_MIN = -2 ** 31


def _nt_dot(a, b):
    return lax.dot_general(a, b, (((1,), (1,)), ((), ())), preferred_element_type=F32)


def _tn_dot(a, b):
    return lax.dot_general(a, b, (((0,), (0,)), ((), ())), preferred_element_type=F32)


def _dot(a, b):
    return jnp.dot(a, b, preferred_element_type=F32)


def _split_bf16(a):
    hi = a.astype(BF16)
    lo = (a - hi.astype(F32)).astype(BF16)
    return hi, lo


def _rms_norm(x, g):
    return x * lax.rsqrt(jnp.mean(x * x, axis=-1, keepdims=True) + RMS_EPS) * g


def _silu(x):
    return x * jax.nn.sigmoid(x)


def _rope(x, cos, sin_lo, sin_hi):
    w = x.shape[1]
    rep = w // LANES
    if rep > 1:
        cos = jnp.concatenate([cos] * rep, axis=1)
        sin_lo = jnp.concatenate([sin_lo] * rep, axis=1)
        sin_hi = jnp.concatenate([sin_hi] * rep, axis=1)
    from_below = pltpu.roll(x, ROT_HALF, 1)
    from_above = pltpu.roll(x, w - ROT_HALF, 1)
    return x * cos + from_below * sin_hi + from_above * sin_lo


def _in_proj_kernel(x_ref, g_ref, w_ref, cos_ref, slo_ref, shi_ref, pw_ref, pscale_ref,
                    ypool_ref, hg_ref, q_ref, k_ref, v_ref, qi_ref, ki_ref, wi_ref,
                    halo_ref, *, tiles_per_seq):
    tm = x_ref.shape[0]
    sub = min(PROJ_SUB, tm)
    seq_tile = pl.program_id(0) % tiles_per_seq

    @pl.when(seq_tile == 0)
    def _():
        halo_ref[...] = jnp.zeros_like(halo_ref)

    prev = halo_ref[...]
    for r in range(tm // sub):
        prev = _in_proj_rows(slice(r * sub, (r + 1) * sub), seq_tile * tm + r * sub, prev,
                             x_ref, g_ref, w_ref, cos_ref, slo_ref, shi_ref, pw_ref, pscale_ref,
                             ypool_ref, hg_ref, q_ref, k_ref, v_ref, qi_ref, ki_ref, wi_ref)
    halo_ref[...] = prev


def _in_proj_rows(rows, t_start, prev, x_ref, g_ref, w_ref, cos_ref, slo_ref, shi_ref, pw_ref,
                  pscale_ref, ypool_ref, hg_ref, q_ref, k_ref, v_ref, qi_ref, ki_ref, wi_ref):
    tm = rows.stop - rows.start
    halo = prev.shape[0]
    h = _rms_norm(x_ref[rows, :], g_ref[...])
    proj = _dot(h.astype(BF16), w_ref[...])

    u = proj[:, COL_POOL:COL_POOL + POOL_WIDTH]
    ext = jnp.concatenate([prev, u], axis=0)
    sums = []
    acc = ext
    span = 1
    for win in POOL_WINDOWS:
        while span < win:
            acc = acc + pltpu.roll(acc, span, 0)
            span *= 2
        sums.append(acc[halo:, :])
    lane = lax.broadcasted_iota(jnp.int32, (tm, POOL_WIDTH), 1)
    group = lane // POOL_GROUP_DIM
    wsum = sums[-1]
    win_lane = jnp.full((tm, POOL_WIDTH), POOL_WINDOWS[-1], jnp.int32)
    for gi in range(len(POOL_WINDOWS) - 2, -1, -1):
        wsum = jnp.where(group == gi, sums[gi], wsum)
        win_lane = jnp.where(group == gi, POOL_WINDOWS[gi], win_lane)
    t_pos = t_start + lax.broadcasted_iota(jnp.int32, (tm, POOL_WIDTH), 0)
    count = jnp.minimum(t_pos + 1, win_lane).astype(F32)
    pooled = wsum / count - u
    y_pool = _dot(pooled.astype(BF16), pw_ref[...]) * pscale_ref[...]
    ypool_ref[rows, :] = y_pool.astype(ypool_ref.dtype)

    hg_ref[rows, :] = proj[:, COL_HGRN:COL_HGRN + 4 * HGRN_WIDTH]

    cos, slo, shi = cos_ref[rows, :], slo_ref[rows, :], shi_ref[rows, :]
    q = _rope(proj[:, COL_Q:COL_Q + ATTN_WIDTH], cos, slo, shi) * (ATTN_SCALE * LOG2_E)
    q_ref[rows, :] = q.astype(q_ref.dtype)
    k = _rope(proj[:, COL_K:COL_K + KV_WIDTH], cos, slo, shi)
    k_ref[rows, :] = k.astype(k_ref.dtype)
    v_ref[rows, :] = proj[:, COL_V:COL_V + KV_WIDTH].astype(v_ref.dtype)
    qi = _rope(proj[:, COL_QI:COL_QI + IDX_HEADS * IDX_DIM], cos, slo, shi)
    qi_hi = qi.astype(BF16).astype(F32)
    qi_lo = qi - qi_hi
    zeros = jnp.zeros((tm, IDX_DIM), F32)
    pieces = []
    for hd in range(IDX_HEADS):
        cols = slice(hd * IDX_DIM, (hd + 1) * IDX_DIM)
        pieces += [qi_hi[:, cols], qi_hi[:, cols], qi_lo[:, cols], zeros]
    qi_ref[rows, :] = jnp.concatenate(pieces, axis=1).astype(qi_ref.dtype)
    kiw_raw = proj[:, COL_KI:COL_KI + LANES]
    ki = _rope(kiw_raw, cos, slo, shi)[:, 0:IDX_DIM]
    ki_hi = ki.astype(BF16).astype(F32)
    ki_ref[rows, :] = jnp.concatenate([ki_hi, ki - ki_hi, ki_hi, zeros], axis=1).astype(ki_ref.dtype)
    wi_ref[rows, :] = kiw_raw * IDX_W_SCALE
    return u[tm - halo:, :]


def _in_proj(x2d, norm_g, w_in_pad, cos, slo, shi, pool_w_bd, pool_scale, layer, seq_len):
    n_tok, d_model = x2d.shape
    tm = min(PROJ_TILE, seq_len)
    assert seq_len % tm == 0 and n_tok % tm == 0
    halo = 2 * SUBLANES
    assert halo >= POOL_WINDOWS[-1] and tm >= halo
    tok = lambda w: pl.BlockSpec((tm, w), lambda i: (i, 0))
    lay = lambda *s: pl.BlockSpec((None,) + s, lambda i: (layer,) + (0,) * len(s))
    out_shape = (
        jax.ShapeDtypeStruct((n_tok, POOL_WIDTH), BF16),
        jax.ShapeDtypeStruct((n_tok, 4 * HGRN_WIDTH), F32),
        jax.ShapeDtypeStruct((n_tok, ATTN_WIDTH), BF16),
        jax.ShapeDtypeStruct((n_tok, KV_WIDTH), BF16),
        jax.ShapeDtypeStruct((n_tok, KV_WIDTH), BF16),
        jax.ShapeDtypeStruct((n_tok, IDX_HEADS * IDX_CAT), BF16),
        jax.ShapeDtypeStruct((n_tok, IDX_CAT), BF16),
        jax.ShapeDtypeStruct((n_tok, LANES), F32),
    )
    return pl.pallas_call(
        functools.partial(_in_proj_kernel, tiles_per_seq=seq_len // tm),
        grid=(n_tok // tm,),
        in_specs=[tok(d_model), lay(1, d_model), lay(d_model, D_IN_PAD),
                  tok(LANES), tok(LANES), tok(LANES),
                  lay(POOL_WIDTH, POOL_WIDTH), lay(1, POOL_WIDTH)],
        out_specs=tuple(tok(s.shape[1]) for s in out_shape),
        out_shape=out_shape,
        scratch_shapes=[pltpu.VMEM((halo, POOL_WIDTH), F32)],
        compiler_params=pltpu.CompilerParams(dimension_semantics=("arbitrary",),
                                             vmem_limit_bytes=VMEM_LIMIT_BYTES),
        name="in_proj",
    )(x2d, norm_g, w_in_pad, cos, slo, shi, pool_w_bd, pool_scale)


def _hgrn_kernel(hg_ref, lb_ref, ng_ref, out_ref, state_ref, o_ref):
    tb, width = out_ref.shape
    c = HGRN_CHUNK
    half = c // 2
    n_chunks = tb // c
    heads = width // HGRN_HEAD_DIM

    @pl.when(pl.program_id(1) == 0)
    def _():
        state_ref[...] = jnp.zeros_like(state_ref)

    lb = lb_ref[...]
    q = _silu(hg_ref[:, 0:width])
    sig = jax.nn.sigmoid(hg_ref[:, width:2 * width])
    f = lb + (1.0 - lb) * sig
    log2_f = jnp.log(jnp.maximum(f, LOG_F_FLOOR)) * LOG2_E
    k = (1.0 - lb) * (1.0 - sig)
    v = hg_ref[:, 2 * width:3 * width]

    row_in_chunk = lax.broadcasted_iota(jnp.int32, (tb, width), 0) % c
    b = log2_f
    step = 1
    while step < c:
        b = b + jnp.where(row_in_chunk >= step, pltpu.roll(b, step, 0), 0.0)
        step *= 2

    lane_r = lax.broadcasted_iota(jnp.int32, (width, width), 0) // HGRN_HEAD_DIM
    lane_c = lax.broadcasted_iota(jnp.int32, (width, width), 1) // HGRN_HEAD_DIM
    head_sum = jnp.where(lane_r == lane_c, 1.0, 0.0).astype(BF16)

    tiles = lambda a: a.reshape(tb // half, half, width)
    q3, k3, b3, v3 = tiles(q), tiles(k), tiles(b), tiles(v)
    row_in_tile = lax.broadcasted_iota(jnp.int32, (tb // half, half, width), 1)
    o_near = None
    for s in range(half):
        if s == 0:
            p, vs = q3 * k3, v3
        else:
            ks, bs, vs = (pltpu.roll(a, s, 1) for a in (k3, b3, v3))
            p = jnp.where(row_in_tile >= s, q3 * ks * jnp.exp2(b3 - bs), 0.0)
        score = _dot(p.reshape(tb, width).astype(BF16), head_sum)
        term = score * vs.reshape(tb, width)
        o_near = term if o_near is None else o_near + term
    o_ref[...] = o_near
    first = lambda a: jnp.concatenate([a[ci * c:ci * c + half] for ci in range(n_chunks)], axis=0)
    second = lambda a: jnp.concatenate([a[ci * c + half:(ci + 1) * c] for ci in range(n_chunks)], axis=0)
    hb = tb // 2
    b1, b2 = first(b), second(b)
    b_mid = jnp.concatenate([jnp.broadcast_to(b1[(ci + 1) * half - 1:(ci + 1) * half], (half, width))
                             for ci in range(n_chunks)], axis=0)
    q_mid = second(q) * jnp.exp2(b2 - b_mid)
    k_mid = (first(k) * jnp.exp2(b_mid - b1)).astype(BF16)
    head_of_lane = lax.broadcasted_iota(jnp.int32, (hb, width), 1) // HGRN_HEAD_DIM
    q_heads = jnp.concatenate([jnp.where(head_of_lane == hd, q_mid, 0.0) for hd in range(heads)], axis=0)
    scores = _nt_dot(q_heads.astype(BF16), k_mid)
    t_chunk = (lax.broadcasted_iota(jnp.int32, (heads * hb, hb), 0) % hb) // half
    s_chunk = lax.broadcasted_iota(jnp.int32, (heads * hb, hb), 1) // half
    scores = jnp.where(t_chunk == s_chunk, scores, 0.0).astype(BF16)
    o_heads = _dot(scores, first(v).astype(BF16))
    o_far = jnp.where(head_of_lane == 0, o_heads[0:hb], 0.0)
    for hd in range(1, heads):
        o_far = jnp.where(head_of_lane == hd, o_heads[hd * hb:(hd + 1) * hb], o_far)
    for ci in range(n_chunks):
        o_ref[ci * c + half:(ci + 1) * c, :] += o_far[ci * half:(ci + 1) * half]

    lane_head = lax.broadcasted_iota(jnp.int32, (c, width), 1) // HGRN_HEAD_DIM
    for ci in range(n_chunks):
        rows = slice(ci * c, (ci + 1) * c)
        bc = b[rows]
        b_last = bc[c - 1:c]
        state = state_ref[...]
        q_dec = q[rows] * jnp.exp2(bc)
        q_heads = jnp.concatenate([jnp.where(lane_head == hd, q_dec, 0.0) for hd in range(heads)], axis=0)
        o_heads = _nt_dot(q_heads.astype(BF16), state.astype(BF16))
        o_inter = jnp.where(lane_head == 0, o_heads[0:c], 0.0)
        for hd in range(1, heads):
            o_inter = jnp.where(lane_head == hd, o_heads[hd * c:(hd + 1) * c], o_inter)
        o_ref[rows, :] += o_inter
        k_dec = k[rows] * jnp.exp2(b_last - bc)
        upd = _tn_dot(v[rows].astype(BF16), k_dec.astype(BF16))
        state_ref[...] = state * jnp.exp2(b_last) + upd

    o = o_ref[...]
    sq_hi, sq_lo = _split_bf16(o * o)
    mean_sq = (_dot(sq_hi, head_sum) + _dot(sq_lo, head_sum)) * (1.0 / HGRN_HEAD_DIM)
    gate = _silu(hg_ref[:, 3 * width:4 * width])
    out_ref[...] = (o * lax.rsqrt(mean_sq + RMS_EPS) * ng_ref[...] * gate).astype(out_ref.dtype)


def _hgrn(hg, lower_bound, norm_g, layer, batch, seq_len):
    n_tok = hg.shape[0]
    tb = min(HGRN_TILE, seq_len)
    assert seq_len % tb == 0 and tb % HGRN_CHUNK == 0 and HGRN_CHUNK == 2 * SUBLANES
    steps = seq_len // tb
    lay = pl.BlockSpec((None, 1, HGRN_WIDTH), lambda bi, ti: (layer, 0, 0))
    return pl.pallas_call(
        _hgrn_kernel,
        grid=(batch, steps),
        in_specs=[pl.BlockSpec((tb, 4 * HGRN_WIDTH), lambda bi, ti: (bi * steps + ti, 0)),
                  lay, lay],
        out_specs=pl.BlockSpec((tb, HGRN_WIDTH), lambda bi, ti: (bi * steps + ti, 0)),
        out_shape=jax.ShapeDtypeStruct((n_tok, HGRN_WIDTH), BF16),
        scratch_shapes=[pltpu.VMEM((HGRN_WIDTH, HGRN_WIDTH), F32),
                        pltpu.VMEM((tb, HGRN_WIDTH), F32)],
        compiler_params=pltpu.CompilerParams(dimension_semantics=("arbitrary", "arbitrary"),
                                             vmem_limit_bytes=VMEM_LIMIT_BYTES),
        name="hgrn2",
    )(hg, lower_bound, norm_g)


def _dsa_kernel(q_ref, k_ref, v_ref, qi_ref, ki_ref, wi_ref, out_ref,
                key_ref, hi_ref, lo_ref, thr_ref, need_ref, knorm_ref, m_ref, acc_ref, *, top_k):
    qb = q_ref.shape[0]
    kt = key_ref.shape[1]
    i16 = jnp.int16
    rows16 = 2 * SUBLANES
    span = 4 * rows16
    tall = lambda a: jnp.concatenate([a] * (span // rows16), axis=0)
    t0 = pl.program_id(1) * qb
    n_tiles = (t0 + qb + kt - 1) // kt
    n_whole = (t0 + 1) // kt

    key_pos = lax.broadcasted_iota(jnp.int32, (kt, qb), 0)
    query_pos = t0 + lax.broadcasted_iota(jnp.int32, (kt, qb), 1)

    @pl.when(pl.program_id(1) == 0)
    def _():
        kv_cols = (lax.broadcasted_iota(jnp.int32, (KV_WIDTH, LANES), 0) // HEAD_DIM
                   == lax.broadcasted_iota(jnp.int32, (KV_WIDTH, LANES), 1))
        kv_cols = jnp.where(kv_cols, 1.0, 0.0).astype(BF16)

        def key_norms(ti, best):
            k_f32 = k_ref[pl.ds(pl.multiple_of(ti * kt, kt), kt), :].astype(F32)
            kk_hi, kk_lo = _split_bf16(k_f32 * k_f32)
            norm2 = _dot(kk_hi, kv_cols) + _dot(kk_lo, kv_cols)
            return jnp.maximum(best, jnp.max(norm2, axis=0, keepdims=True))

        best = lax.fori_loop(0, k_ref.shape[0] // kt, key_norms, jnp.zeros((1, LANES), F32))
        knorm_ref[...] = jnp.broadcast_to(best, knorm_ref.shape)

    wi_t = wi_ref[...].T
    w_heads = [wi_t[IDX_DIM + h:IDX_DIM + h + 1, :] for h in range(IDX_HEADS)]

    def index_tile(ti, carry, causal_cut):
        base = pl.multiple_of(ti * kt, kt)
        ki_tile = ki_ref[pl.ds(base, kt), :]
        score = jnp.zeros((kt, qb), F32)
        for h in range(IDX_HEADS):
            s = _nt_dot(ki_tile, qi_ref[:, h * IDX_CAT:(h + 1) * IDX_CAT])
            score = score + jnp.maximum(s, 0.0) * w_heads[h]
        score = jnp.where(score == 0.0, 0.0, score)
        if causal_cut:
            score = jnp.where(base + key_pos <= query_pos, score, MASK_VALUE)
        bits = lax.bitcast_convert_type(score, jnp.int32)
        key = bits ^ ((bits >> 31) & 0x7FFFFFFF)
        key_ref[ti] = key
        hi_ref[ti] = (key >> 16).astype(i16)
        lo_ref[ti] = ((key & 0xFFFF) - 2 ** 15).astype(i16)
        return carry

    lax.fori_loop(0, n_whole, functools.partial(index_tile, causal_cut=False), 0)
    lax.fori_loop(n_whole, n_tiles, functools.partial(index_tile, causal_cut=True), 0)

    def count(ref, pred):
        def body(ti, acc):
            keys = ref[ti]
            for r in range(kt // span):
                acc = acc + jnp.where(pred(keys[r * span:(r + 1) * span, :]), i16(1), i16(0))
            return acc
        partial = lax.fori_loop(0, n_tiles, body, jnp.zeros((span, qb), i16))
        return jnp.sum(partial.astype(F32), axis=0, keepdims=True)

    def kth_largest(ref, target):
        def bit(it, state):
            thr, n_above = state
            cand = thr + lax.shift_left(jnp.int32(1), 15 - it)
            cand16 = tall(cand.astype(i16))
            n_ge = jnp.broadcast_to(count(ref, lambda keys: keys >= cand16), thr.shape)
            ok = n_ge >= target
            return jnp.where(ok, cand, thr), jnp.where(ok, n_above, n_ge)
        return lax.fori_loop(0, 16, bit, (jnp.full((rows16, qb), -2 ** 15, jnp.int32),
                                          jnp.zeros((rows16, qb), F32)))

    thr_ref[...] = jnp.full_like(thr_ref, INT_MIN)
    need_ref[...] = jnp.zeros_like(need_ref)

    @pl.when(t0 + qb > top_k)
    def _():
        upper, n_upper_gt = kth_largest(hi_ref, float(top_k))
        upper_kt = jnp.concatenate([tall(upper.astype(i16))] * (kt // span), axis=0)

        def keep_bucket(ti, carry):
            lo_ref[ti] = jnp.where(hi_ref[ti] == upper_kt, lo_ref[ti], i16(-2 ** 15))
            return carry

        lax.fori_loop(0, n_tiles, keep_bucket, 0)
        lower, n_lower_gt = kth_largest(lo_ref, top_k - n_upper_gt)
        thr_ref[...] = (lax.shift_left(upper, 16) + (lower + 2 ** 15))[0:SUBLANES]
        need_ref[...] = (top_k - (n_upper_gt + n_lower_gt))[0:SUBLANES]

    thr = thr_ref[0:1, :]
    need = need_ref[0:1, :]

    earlier = (lax.broadcasted_iota(jnp.int32, (kt, kt), 1)
               < lax.broadcasted_iota(jnp.int32, (kt, kt), 0))
    earlier_bf = jnp.where(earlier, 1.0, 0.0).astype(BF16)
    ones_rows = jnp.ones((SUBLANES, kt), BF16)
    dim_v = lax.broadcasted_iota(jnp.int32, (KV_WIDTH, kt), 0)

    def selection(ti, ties_before, causal_cut):
        base = pl.multiple_of(ti * kt, kt)
        keys = key_ref[ti]
        tie = keys == thr
        tie_bf = jnp.where(tie, 1.0, 0.0).astype(BF16)
        rank = _dot(earlier_bf, tie_bf) + ties_before
        chosen = (keys > thr) | (tie & (rank < need))
        if causal_cut:
            chosen = chosen & (base + key_pos <= query_pos)
        return jnp.where(chosen, 0.0, MASK_VALUE), ties_before + _dot(ones_rows, tie_bf)[0:1]

    def values(ti):
        v_t = v_ref[pl.ds(pl.multiple_of(ti * kt, kt), kt), :].astype(F32).T
        return [jnp.where(dim_v // HEAD_DIM == g, v_t, 1.0).astype(BF16) for g in range(N_KV_HEADS)]

    head_cols = (lax.broadcasted_iota(jnp.int32, (ATTN_WIDTH, LANES), 0) // HEAD_DIM
                 == lax.broadcasted_iota(jnp.int32, (ATTN_WIDTH, LANES), 1))
    head_cols = jnp.where(head_cols, 1.0, 0.0).astype(BF16)
    q_sq = q_ref[...] * q_ref[...]
    q_norm2 = jnp.max(_dot(q_sq, head_cols), axis=0, keepdims=True)
    k_norm2 = knorm_ref[0:1, :]
    shift = [jnp.sqrt(jnp.max(q_norm2[:, g * KV_GROUP:(g + 1) * KV_GROUP], axis=1, keepdims=True)
                      * k_norm2[:, g:g + 1]) * SHIFT_MARGIN for g in range(N_KV_HEADS)]
    q_stack = [jnp.concatenate([q_ref[:, h * HEAD_DIM:(h + 1) * HEAD_DIM]
                                for h in range(g * KV_GROUP, (g + 1) * KV_GROUP)], axis=0)
               for g in range(N_KV_HEADS)]

    def scores(ti):
        k_tile = k_ref[pl.ds(pl.multiple_of(ti * kt, kt), kt), :]
        return [_nt_dot(k_tile[:, g * HEAD_DIM:(g + 1) * HEAD_DIM], q_stack[g])
                for g in range(N_KV_HEADS)]

    def fast_tile(ti, ties_before, causal_cut):
        s_all = scores(ti)
        bias, ties_after = selection(ti, ties_before, causal_cut)
        v_aug = values(ti)
        for g in range(N_KV_HEADS):
            shifted = bias - shift[g]
            for hh in range(KV_GROUP):
                p = jnp.exp2(s_all[g][:, hh * qb:(hh + 1) * qb] + shifted)
                acc_ref[g * KV_GROUP + hh] += _dot(v_aug[g], p.astype(BF16))
        return ties_after

    def run(tile_fn):
        ties = lax.fori_loop(0, n_whole, functools.partial(tile_fn, causal_cut=False),
                             jnp.zeros((1, qb), F32))
        lax.fori_loop(n_whole, n_tiles, functools.partial(tile_fn, causal_cut=True), ties)

    denominator = lambda h: acc_ref[h][(1 - h // KV_GROUP) * HEAD_DIM:(1 - h // KV_GROUP) * HEAD_DIM + 1, :]
    acc_ref[...] = jnp.zeros_like(acc_ref)
    run(fast_tile)
    smallest = denominator(0)
    for h in range(1, N_HEADS):
        smallest = jnp.minimum(smallest, denominator(h))

    @pl.when(jnp.logical_not(jnp.min(smallest) >= SAFE_DENOMINATOR))
    def _():
        m_ref[...] = jnp.full_like(m_ref, MASK_VALUE)
        acc_ref[...] = jnp.zeros_like(acc_ref)
        def exact_tile(ti, ties_before, causal_cut):
            s_all = scores(ti)
            bias, ties_after = selection(ti, ties_before, causal_cut)
            v_aug = values(ti)
            for h in range(N_HEADS):
                g, hh = divmod(h, KV_GROUP)
                s = s_all[g][:, hh * qb:(hh + 1) * qb] + bias
                m_old = m_ref[h]
                m_new = jnp.maximum(m_old, jnp.max(s, axis=0, keepdims=True))
                p = jnp.exp2(s - m_new)
                acc_ref[h] = jnp.exp2(m_old - m_new) * acc_ref[h] + _dot(v_aug[g], p.astype(BF16))
                m_ref[h] = m_new
            return ties_after

        run(exact_tile)

    for pair in range(N_HEADS // 2):
        outs = []
        for h in (2 * pair, 2 * pair + 1):
            g = h // KV_GROUP
            outs.append(acc_ref[h][g * HEAD_DIM:(g + 1) * HEAD_DIM, :] / denominator(h))
        out_ref[:, pair * LANES:(pair + 1) * LANES] = (
            jnp.concatenate(outs, axis=0).T.astype(out_ref.dtype))


def _dsa(q, k, v, qi, ki, wi, batch, seq_len):
    n_tok = q.shape[0]
    qb = min(Q_TILE, seq_len)
    kt = min(KEY_TILE, seq_len)
    assert seq_len % qb == 0 and seq_len % kt == 0 and qb % LANES == 0
    assert N_KV_HEADS == 2 and KV_WIDTH == LANES and 2 * HEAD_DIM == LANES
    steps = seq_len // qb
    n_kt = seq_len // kt
    top_k = min(TOPK_MAX, seq_len // 4)
    qtile = lambda w: pl.BlockSpec((qb, w), lambda bi, qi_: (bi * steps + qi_, 0))
    seq = lambda w: pl.BlockSpec((seq_len, w), lambda bi, qi_: (bi, 0))
    return pl.pallas_call(
        functools.partial(_dsa_kernel, top_k=top_k),
        grid=(batch, steps),
        in_specs=[qtile(ATTN_WIDTH), seq(KV_WIDTH), seq(KV_WIDTH),
                  qtile(IDX_HEADS * IDX_CAT), seq(IDX_CAT), qtile(LANES)],
        out_specs=qtile(ATTN_WIDTH),
        out_shape=jax.ShapeDtypeStruct((n_tok, ATTN_WIDTH), BF16),
        scratch_shapes=[pltpu.VMEM((n_kt, kt, qb), jnp.int32),
                        pltpu.VMEM((n_kt, kt, qb), jnp.int16),
                        pltpu.VMEM((n_kt, kt, qb), jnp.int16),
                        pltpu.VMEM((SUBLANES, qb), jnp.int32),
                        pltpu.VMEM((SUBLANES, qb), F32),
                        pltpu.VMEM((SUBLANES, LANES), F32),
                        pltpu.VMEM((N_HEADS, 1, qb), F32),
                        pltpu.VMEM((N_HEADS, KV_WIDTH, qb), F32)],
        compiler_params=pltpu.CompilerParams(dimension_semantics=("arbitrary", "arbitrary"),
                                             vmem_limit_bytes=VMEM_LIMIT_BYTES),
        name="dsa",
    )(q, k, v, qi, ki, wi)


def _out_ffn_kernel(x_ref, yp_ref, yh_ref, ya_ref, wo_hbm, g2_ref, wfi_hbm, wfo_hbm, gf_ref,
                    out_ref, wo_ref, wfi_ref, wfo_ref, *, layer, final_norm):
    @pl.when(pl.program_id(0) == 0)
    def _():
        pltpu.sync_copy(wo_hbm.at[layer], wo_ref)
        pltpu.sync_copy(wfi_hbm.at[layer], wfi_ref)
        pltpu.sync_copy(wfo_hbm.at[layer], wfo_ref)

    x = x_ref[...]
    x = x + _dot(yp_ref[...], wo_ref[0:POOL_WIDTH, :])
    x = x + _dot(yh_ref[...], wo_ref[POOL_WIDTH:POOL_WIDTH + HGRN_WIDTH, :])
    x = x + _dot(ya_ref[...], wo_ref[POOL_WIDTH + HGRN_WIDTH:, :])
    h = _rms_norm(x, g2_ref[...]).astype(BF16)
    a = jnp.maximum(_dot(h, wfi_ref[...]), 0.0)
    x = x + _dot((a * a).astype(BF16), wfo_ref[...])
    if final_norm:
        x = _rms_norm(x, gf_ref[...])
    out_ref[...] = x


def _out_ffn(x2d, y_pool, y_hgrn, y_attn, w_out, norm2_g, w_ff_in, w_ff_out, final_g,
             layer, final_norm, seq_len):
    n_tok, d_model = x2d.shape
    d_ff = w_ff_in.shape[2]
    tm = min(MLP_TILE, seq_len)
    assert seq_len % tm == 0
    tok = lambda w: pl.BlockSpec((tm, w), lambda i: (i, 0))
    hbm = pl.BlockSpec(memory_space=pl.ANY)
    return pl.pallas_call(
        functools.partial(_out_ffn_kernel, layer=layer, final_norm=final_norm),
        grid=(n_tok // tm,),
        in_specs=[tok(d_model), tok(POOL_WIDTH), tok(HGRN_WIDTH), tok(ATTN_WIDTH),
                  hbm, pl.BlockSpec((None, 1, d_model), lambda i: (layer, 0, 0)), hbm, hbm,
                  pl.BlockSpec((1, d_model), lambda i: (0, 0))],
        out_specs=tok(d_model),
        out_shape=jax.ShapeDtypeStruct((n_tok, d_model), F32),
        scratch_shapes=[pltpu.VMEM((d_model, d_model), BF16),
                        pltpu.VMEM((d_model, d_ff), BF16),
                        pltpu.VMEM((d_ff, d_model), BF16)],
        compiler_params=pltpu.CompilerParams(dimension_semantics=("arbitrary",),
                                             vmem_limit_bytes=VMEM_LIMIT_BYTES),
        name="out_ffn",
    )(x2d, y_pool, y_hgrn, y_attn, w_out, norm2_g, w_ff_in, w_ff_out, final_g)


def _rope_patterns(positions):
    inv_freq = ROPE_THETA ** (-jnp.arange(0, ROT_DIM, 2, dtype=F32) / ROT_DIM)
    rest = jnp.zeros((HEAD_DIM - ROT_DIM,), F32)
    freq_head = jnp.concatenate([inv_freq, inv_freq, rest])
    lo_head = jnp.concatenate([-jnp.ones((ROT_HALF,), F32), jnp.zeros((ROT_HALF,), F32), rest])
    hi_head = jnp.concatenate([jnp.zeros((ROT_HALF,), F32), jnp.ones((ROT_HALF,), F32), rest])
    two = lambda a: jnp.concatenate([a, a])[None, :]
    ang = positions.astype(F32).reshape(-1, 1) * two(freq_head)
    sin = jnp.sin(ang)
    return jnp.cos(ang), sin * two(lo_head), sin * two(hi_head)


def kernel(x, positions, norm1_g, w_in, pool_w, pool_scale, lb_logits, hgrn_norm_g, w_out,
           norm2_g, w_ff_in, w_ff_out, final_norm_g):
    batch, seq_len, d_model = x.shape
    depth = w_in.shape[0]
    assert w_in.shape[2] == D_IN

    p_lb = jax.nn.softmax(lb_logits.astype(F32), axis=0)
    lower_bounds = (jnp.cumsum(p_lb, axis=0) - p_lb[0])[:, None, :]
    w_in_pad = jnp.pad(w_in, ((0, 0), (0, 0), (0, D_IN_PAD - D_IN))).astype(BF16)
    groups = len(POOL_WINDOWS)
    eye = jnp.eye(groups, dtype=F32)
    pool_w_bd = (pool_w[:, :, :, None, :] * eye[None, :, None, :, None]).reshape(
        depth, POOL_WIDTH, POOL_WIDTH).astype(BF16)
    w_out_b, w_ff_in_b, w_ff_out_b = (w.astype(BF16) for w in (w_out, w_ff_in, w_ff_out))
    row = lambda a: a.reshape(depth, 1, -1).astype(F32)
    norm1, norm2, pscale, hnorm = row(norm1_g), row(norm2_g), row(pool_scale), row(hgrn_norm_g)
    final_g = final_norm_g.reshape(1, d_model).astype(F32)
    cos, slo, shi = _rope_patterns(positions)

    x2d = x.reshape(batch * seq_len, d_model)
    for layer in range(depth):
        y_pool, hg, q, k, v, qi, ki, wi = _in_proj(x2d, norm1, w_in_pad, cos, slo, shi, pool_w_bd,
                                                pscale, layer, seq_len)
        y_hgrn = _hgrn(hg, lower_bounds, hnorm, layer, batch, seq_len)
        y_attn = _dsa(q, k, v, qi, ki, wi, batch, seq_len)
        x2d = _out_ffn(x2d, y_pool, y_hgrn, y_attn, w_out_b, norm2, w_ff_in_b, w_ff_out_b,
                       final_g, layer, layer == depth - 1, seq_len)
    return x2d.reshape(batch, seq_len, d_model)
```

```python
import functools

import jax
import jax.numpy as jnp
from jax import lax
from jax.experimental import pallas as pl
from jax.experimental.pallas import tpu as pltpu

F32 = jnp.float32
BF16 = jnp.bfloat16

POOL_WINDOWS = (2, 4, 8, 16)
POOL_WIDTH = 256
POOL_GROUP_DIM = 64
HGRN_WIDTH = 256
HGRN_HEAD_DIM = 64
LOG_F_FLOOR = 1e-30
ATTN_WIDTH = 512
HEAD_DIM = 64
N_HEADS = 8
N_KV_HEADS = 2
KV_GROUP = N_HEADS // N_KV_HEADS
KV_WIDTH = N_KV_HEADS * HEAD_DIM
IDX_HEADS = 4
IDX_DIM = 64
TOPK_MAX = 256
MASK_VALUE = -1e30
ROPE_THETA = 500000.0
ROT_DIM = HEAD_DIM // 4
ROT_HALF = ROT_DIM // 2
RMS_EPS = 1e-5
IDX_W_SCALE = (IDX_HEADS ** -0.5) * (IDX_DIM ** -0.5)
ATTN_SCALE = HEAD_DIM ** -0.5
LOG2_E = 1.4426950408889634
SHIFT_MARGIN = 1.02
SAFE_DENOMINATOR = 2.0 ** -60

LANES = 128
SUBLANES = 8
VMEM_LIMIT_BYTES = 56 * 1024 * 1024

COL_POOL = 0
COL_HGRN = COL_POOL + POOL_WIDTH
COL_Q = COL_HGRN + 4 * HGRN_WIDTH
COL_K = COL_Q + ATTN_WIDTH
COL_V = COL_K + KV_WIDTH
COL_QI = COL_V + KV_WIDTH
COL_KI = COL_QI + IDX_HEADS * IDX_DIM
D_IN = COL_KI + IDX_DIM + IDX_HEADS
D_IN_PAD = COL_KI + LANES
IDX_CAT = 4 * IDX_DIM

PROJ_TILE = 1024
MLP_TILE = 512
PROJ_SUB = 128
HGRN_TILE = 512
HGRN_CHUNK = 16
Q_TILE = 512
KEY_TILE = 512
INT_MIN = -2 ** 31


def _nt_dot(a, b):
    return lax.dot_general(a, b, (((1,), (1,)), ((), ())), preferred_element_type=F32)


def _tn_dot(a, b):
    return lax.dot_general(a, b, (((0,), (0,)), ((), ())), preferred_element_type=F32)


def _dot(a, b):
    return jnp.dot(a, b, preferred_element_type=F32)


def _split_bf16(a):
    hi = a.astype(BF16)
    lo = (a - hi.astype(F32)).astype(BF16)
    return hi, lo


def _rms_norm(x, g):
    return x * lax.rsqrt(jnp.mean(x * x, axis=-1, keepdims=True) + RMS_EPS) * g


def _silu(x):
    return x * jax.nn.sigmoid(x)


def _rope(x, cos, sin_lo, sin_hi):
    w = x.shape[1]
    rep = w // LANES
    if rep > 1:
        cos = jnp.concatenate([cos] * rep, axis=1)
        sin_lo = jnp.concatenate([sin_lo] * rep, axis=1)
        sin_hi = jnp.concatenate([sin_hi] * rep, axis=1)
    from_below = pltpu.roll(x, ROT_HALF, 1)
    from_above = pltpu.roll(x, w - ROT_HALF, 1)
    return x * cos + from_below * sin_hi + from_above * sin_lo


def _in_proj_kernel(x_ref, g_ref, w_ref, cos_ref, slo_ref, shi_ref, pw_ref, pscale_ref,
                    ypool_ref, hg_ref, q_ref, k_ref, v_ref, qi_ref, ki_ref, wi_ref,
                    halo_ref, *, tiles_per_seq):
    tm = x_ref.shape[0]
    sub = min(PROJ_SUB, tm)
    seq_tile = pl.program_id(0) % tiles_per_seq

    @pl.when(seq_tile == 0)
    def _():
        halo_ref[...] = jnp.zeros_like(halo_ref)

    prev = halo_ref[...]
    for r in range(tm // sub):
        prev = _in_proj_rows(slice(r * sub, (r + 1) * sub), seq_tile * tm + r * sub, prev,
                             x_ref, g_ref, w_ref, cos_ref, slo_ref, shi_ref, pw_ref, pscale_ref,
                             ypool_ref, hg_ref, q_ref, k_ref, v_ref, qi_ref, ki_ref, wi_ref)
    halo_ref[...] = prev


def _in_proj_rows(rows, t_start, prev, x_ref, g_ref, w_ref, cos_ref, slo_ref, shi_ref, pw_ref,
                  pscale_ref, ypool_ref, hg_ref, q_ref, k_ref, v_ref, qi_ref, ki_ref, wi_ref):
    tm = rows.stop - rows.start
    halo = prev.shape[0]
    h = _rms_norm(x_ref[rows, :], g_ref[...])
    proj = _dot(h.astype(BF16), w_ref[...])

    u = proj[:, COL_POOL:COL_POOL + POOL_WIDTH]
    ext = jnp.concatenate([prev, u], axis=0)
    sums = []
    acc = ext
    span = 1
    for win in POOL_WINDOWS:
        while span < win:
            acc = acc + pltpu.roll(acc, span, 0)
            span *= 2
        sums.append(acc[halo:, :])
    lane = lax.broadcasted_iota(jnp.int32, (tm, POOL_WIDTH), 1)
    group = lane // POOL_GROUP_DIM
    wsum = sums[-1]
    win_lane = jnp.full((tm, POOL_WIDTH), POOL_WINDOWS[-1], jnp.int32)
    for gi in range(len(POOL_WINDOWS) - 2, -1, -1):
        wsum = jnp.where(group == gi, sums[gi], wsum)
        win_lane = jnp.where(group == gi, POOL_WINDOWS[gi], win_lane)
    t_pos = t_start + lax.broadcasted_iota(jnp.int32, (tm, POOL_WIDTH), 0)
    count = jnp.minimum(t_pos + 1, win_lane).astype(F32)
    pooled = wsum / count - u
    y_pool = _dot(pooled.astype(BF16), pw_ref[...]) * pscale_ref[...]
    ypool_ref[rows, :] = y_pool.astype(ypool_ref.dtype)

    hg_ref[rows, :] = proj[:, COL_HGRN:COL_HGRN + 4 * HGRN_WIDTH]

    cos, slo, shi = cos_ref[rows, :], slo_ref[rows, :], shi_ref[rows, :]
    q = _rope(proj[:, COL_Q:COL_Q + ATTN_WIDTH], cos, slo, shi) * (ATTN_SCALE * LOG2_E)
    q_ref[rows, :] = q.astype(q_ref.dtype)
    k = _rope(proj[:, COL_K:COL_K + KV_WIDTH], cos, slo, shi)
    k_ref[rows, :] = k.astype(k_ref.dtype)
    v_ref[rows, :] = proj[:, COL_V:COL_V + KV_WIDTH].astype(v_ref.dtype)
    qi = _rope(proj[:, COL_QI:COL_QI + IDX_HEADS * IDX_DIM], cos, slo, shi)
    qi_hi = qi.astype(BF16).astype(F32)
    qi_lo = qi - qi_hi
    zeros = jnp.zeros((tm, IDX_DIM), F32)
    pieces = []
    for hd in range(IDX_HEADS):
        cols = slice(hd * IDX_DIM, (hd + 1) * IDX_DIM)
        pieces += [qi_hi[:, cols], qi_hi[:, cols], qi_lo[:, cols], zeros]
    qi_ref[rows, :] = jnp.concatenate(pieces, axis=1).astype(qi_ref.dtype)
    kiw_raw = proj[:, COL_KI:COL_KI + LANES]
    ki = _rope(kiw_raw, cos, slo, shi)[:, 0:IDX_DIM]
    ki_hi = ki.astype(BF16).astype(F32)
    ki_ref[rows, :] = jnp.concatenate([ki_hi, ki - ki_hi, ki_hi, zeros], axis=1).astype(ki_ref.dtype)
    wi_ref[rows, :] = kiw_raw * IDX_W_SCALE
    return u[tm - halo:, :]


def _in_proj(x2d, norm_g, w_in_pad, cos, slo, shi, pool_w_bd, pool_scale, layer, seq_len):
    n_tok, d_model = x2d.shape
    tm = min(PROJ_TILE, seq_len)
    assert seq_len % tm == 0 and n_tok % tm == 0
    halo = 2 * SUBLANES
    assert halo >= POOL_WINDOWS[-1] and tm >= halo
    tok = lambda w: pl.BlockSpec((tm, w), lambda i: (i, 0))
    lay = lambda *s: pl.BlockSpec((None,) + s, lambda i: (layer,) + (0,) * len(s))
    out_shape = (
        jax.ShapeDtypeStruct((n_tok, POOL_WIDTH), BF16),
        jax.ShapeDtypeStruct((n_tok, 4 * HGRN_WIDTH), F32),
        jax.ShapeDtypeStruct((n_tok, ATTN_WIDTH), BF16),
        jax.ShapeDtypeStruct((n_tok, KV_WIDTH), BF16),
        jax.ShapeDtypeStruct((n_tok, KV_WIDTH), BF16),
        jax.ShapeDtypeStruct((n_tok, IDX_HEADS * IDX_CAT), BF16),
        jax.ShapeDtypeStruct((n_tok, IDX_CAT), BF16),
        jax.ShapeDtypeStruct((n_tok, LANES), F32),
    )
    return pl.pallas_call(
        functools.partial(_in_proj_kernel, tiles_per_seq=seq_len // tm),
        grid=(n_tok // tm,),
        in_specs=[tok(d_model), lay(1, d_model), lay(d_model, D_IN_PAD),
                  tok(LANES), tok(LANES), tok(LANES),
                  lay(POOL_WIDTH, POOL_WIDTH), lay(1, POOL_WIDTH)],
        out_specs=tuple(tok(s.shape[1]) for s in out_shape),
        out_shape=out_shape,
        scratch_shapes=[pltpu.VMEM((halo, POOL_WIDTH), F32)],
        compiler_params=pltpu.CompilerParams(dimension_semantics=("arbitrary",),
                                             vmem_limit_bytes=VMEM_LIMIT_BYTES),
        name="in_proj",
    )(x2d, norm_g, w_in_pad, cos, slo, shi, pool_w_bd, pool_scale)


def _hgrn_kernel(hg_ref, lb_ref, ng_ref, out_ref, state_ref, o_ref):
    tb, width = out_ref.shape
    c = HGRN_CHUNK
    half = c // 2
    n_chunks = tb // c
    heads = width // HGRN_HEAD_DIM

    @pl.when(pl.program_id(1) == 0)
    def _():
        state_ref[...] = jnp.zeros_like(state_ref)

    lb = lb_ref[...]
    q = _silu(hg_ref[:, 0:width])
    sig = jax.nn.sigmoid(hg_ref[:, width:2 * width])
    f = lb + (1.0 - lb) * sig
    log2_f = jnp.log(jnp.maximum(f, LOG_F_FLOOR)) * LOG2_E
    k = (1.0 - lb) * (1.0 - sig)
    v = hg_ref[:, 2 * width:3 * width]

    row_in_chunk = lax.broadcasted_iota(jnp.int32, (tb, width), 0) % c
    b = log2_f
    step = 1
    while step < c:
        b = b + jnp.where(row_in_chunk >= step, pltpu.roll(b, step, 0), 0.0)
        step *= 2

    lane_r = lax.broadcasted_iota(jnp.int32, (width, width), 0) // HGRN_HEAD_DIM
    lane_c = lax.broadcasted_iota(jnp.int32, (width, width), 1) // HGRN_HEAD_DIM
    head_sum = jnp.where(lane_r == lane_c, 1.0, 0.0).astype(BF16)

    tiles = lambda a: a.reshape(tb // half, half, width)
    q3, k3, b3, v3 = tiles(q), tiles(k), tiles(b), tiles(v)
    row_in_tile = lax.broadcasted_iota(jnp.int32, (tb // half, half, width), 1)
    o_near = None
    for s in range(half):
        if s == 0:
            p, vs = q3 * k3, v3
        else:
            ks, bs, vs = (pltpu.roll(a, s, 1) for a in (k3, b3, v3))
            p = jnp.where(row_in_tile >= s, q3 * ks * jnp.exp2(b3 - bs), 0.0)
        score = _dot(p.reshape(tb, width).astype(BF16), head_sum)
        term = score * vs.reshape(tb, width)
        o_near = term if o_near is None else o_near + term
    o_ref[...] = o_near
    first = lambda a: jnp.concatenate([a[ci * c:ci * c + half] for ci in range(n_chunks)], axis=0)
    second = lambda a: jnp.concatenate([a[ci * c + half:(ci + 1) * c] for ci in range(n_chunks)], axis=0)
    hb = tb // 2
    b1, b2 = first(b), second(b)
    b_mid = jnp.concatenate([jnp.broadcast_to(b1[(ci + 1) * half - 1:(ci + 1) * half], (half, width))
                             for ci in range(n_chunks)], axis=0)
    q_mid = second(q) * jnp.exp2(b2 - b_mid)
    k_mid = (first(k) * jnp.exp2(b_mid - b1)).astype(BF16)
    head_of_lane = lax.broadcasted_iota(jnp.int32, (hb, width), 1) // HGRN_HEAD_DIM
    q_heads = jnp.concatenate([jnp.where(head_of_lane == hd, q_mid, 0.0) for hd in range(heads)], axis=0)
    scores = _nt_dot(q_heads.astype(BF16), k_mid)
    t_chunk = (lax.broadcasted_iota(jnp.int32, (heads * hb, hb), 0) % hb) // half
    s_chunk = lax.broadcasted_iota(jnp.int32, (heads * hb, hb), 1) // half
    scores = jnp.where(t_chunk == s_chunk, scores, 0.0).astype(BF16)
    o_heads = _dot(scores, first(v).astype(BF16))
    o_far = jnp.where(head_of_lane == 0, o_heads[0:hb], 0.0)
    for hd in range(1, heads):
        o_far = jnp.where(head_of_lane == hd, o_heads[hd * hb:(hd + 1) * hb], o_far)
    for ci in range(n_chunks):
        o_ref[ci * c + half:(ci + 1) * c, :] += o_far[ci * half:(ci + 1) * half]

    lane_head = lax.broadcasted_iota(jnp.int32, (c, width), 1) // HGRN_HEAD_DIM
    for ci in range(n_chunks):
        rows = slice(ci * c, (ci + 1) * c)
        bc = b[rows]
        b_last = bc[c - 1:c]
        state = state_ref[...]
        q_dec = q[rows] * jnp.exp2(bc)
        q_heads = jnp.concatenate([jnp.where(lane_head == hd, q_dec, 0.0) for hd in range(heads)], axis=0)
        o_heads = _nt_dot(q_heads.astype(BF16), state.astype(BF16))
        o_inter = jnp.where(lane_head == 0, o_heads[0:c], 0.0)
        for hd in range(1, heads):
            o_inter = jnp.where(lane_head == hd, o_heads[hd * c:(hd + 1) * c], o_inter)
        o_ref[rows, :] += o_inter
        k_dec = k[rows] * jnp.exp2(b_last - bc)
        upd = _tn_dot(v[rows].astype(BF16), k_dec.astype(BF16))
        state_ref[...] = state * jnp.exp2(b_last) + upd

    o = o_ref[...]
    sq_hi, sq_lo = _split_bf16(o * o)
    mean_sq = (_dot(sq_hi, head_sum) + _dot(sq_lo, head_sum)) * (1.0 / HGRN_HEAD_DIM)
    gate = _silu(hg_ref[:, 3 * width:4 * width])
    out_ref[...] = (o * lax.rsqrt(mean_sq + RMS_EPS) * ng_ref[...] * gate).astype(out_ref.dtype)


def _hgrn(hg, lower_bound, norm_g, layer, batch, seq_len):
    n_tok = hg.shape[0]
    tb = min(HGRN_TILE, seq_len)
    assert seq_len % tb == 0 and tb % HGRN_CHUNK == 0 and HGRN_CHUNK == 2 * SUBLANES
    steps = seq_len // tb
    lay = pl.BlockSpec((None, 1, HGRN_WIDTH), lambda bi, ti: (layer, 0, 0))
    return pl.pallas_call(
        _hgrn_kernel,
        grid=(batch, steps),
        in_specs=[pl.BlockSpec((tb, 4 * HGRN_WIDTH), lambda bi, ti: (bi * steps + ti, 0)),
                  lay, lay],
        out_specs=pl.BlockSpec((tb, HGRN_WIDTH), lambda bi, ti: (bi * steps + ti, 0)),
        out_shape=jax.ShapeDtypeStruct((n_tok, HGRN_WIDTH), BF16),
        scratch_shapes=[pltpu.VMEM((HGRN_WIDTH, HGRN_WIDTH), F32),
                        pltpu.VMEM((tb, HGRN_WIDTH), F32)],
        compiler_params=pltpu.CompilerParams(dimension_semantics=("arbitrary", "arbitrary"),
                                             vmem_limit_bytes=VMEM_LIMIT_BYTES),
        name="hgrn2",
    )(hg, lower_bound, norm_g)


def _dsa_kernel(q_ref, k_ref, v_ref, qi_ref, ki_ref, wi_ref, out_ref,
                key_ref, hi_ref, lo_ref, thr_ref, need_ref, knorm_ref, m_ref, acc_ref, *, top_k):
    qb = q_ref.shape[0]
    kt = key_ref.shape[1]
    i16 = jnp.int16
    rows16 = 2 * SUBLANES
    span = 4 * rows16
    tall = lambda a: jnp.concatenate([a] * (span // rows16), axis=0)
    t0 = pl.program_id(1) * qb
    n_tiles = (t0 + qb + kt - 1) // kt
    n_whole = (t0 + 1) // kt

    key_pos = lax.broadcasted_iota(jnp.int32, (kt, qb), 0)
    query_pos = t0 + lax.broadcasted_iota(jnp.int32, (kt, qb), 1)

    @pl.when(pl.program_id(1) == 0)
    def _():
        kv_cols = (lax.broadcasted_iota(jnp.int32, (KV_WIDTH, LANES), 0) // HEAD_DIM
                   == lax.broadcasted_iota(jnp.int32, (KV_WIDTH, LANES), 1))
        kv_cols = jnp.where(kv_cols, 1.0, 0.0).astype(BF16)

        def key_norms(ti, best):
            k_f32 = k_ref[pl.ds(pl.multiple_of(ti * kt, kt), kt), :].astype(F32)
            kk_hi, kk_lo = _split_bf16(k_f32 * k_f32)
            norm2 = _dot(kk_hi, kv_cols) + _dot(kk_lo, kv_cols)
            return jnp.maximum(best, jnp.max(norm2, axis=0, keepdims=True))

        best = lax.fori_loop(0, k_ref.shape[0] // kt, key_norms, jnp.zeros((1, LANES), F32))
        knorm_ref[...] = jnp.broadcast_to(best, knorm_ref.shape)

    wi_t = wi_ref[...].T
    w_heads = [wi_t[IDX_DIM + h:IDX_DIM + h + 1, :] for h in range(IDX_HEADS)]

    def index_tile(ti, carry, causal_cut):
        base = pl.multiple_of(ti * kt, kt)
        ki_tile = ki_ref[pl.ds(base, kt), :]
        score = jnp.zeros((kt, qb), F32)
        for h in range(IDX_HEADS):
            s = _nt_dot(ki_tile, qi_ref[:, h * IDX_CAT:(h + 1) * IDX_CAT])
            score = score + jnp.maximum(s, 0.0) * w_heads[h]
        score = jnp.where(score == 0.0, 0.0, score)
        if causal_cut:
            score = jnp.where(base + key_pos <= query_pos, score, MASK_VALUE)
        bits = lax.bitcast_convert_type(score, jnp.int32)
        key = bits ^ ((bits >> 31) & 0x7FFFFFFF)
        key_ref[ti] = key
        hi_ref[ti] = (key >> 16).astype(i16)
        lo_ref[ti] = ((key & 0xFFFF) - 2 ** 15).astype(i16)
        return carry

    lax.fori_loop(0, n_whole, functools.partial(index_tile, causal_cut=False), 0)
    lax.fori_loop(n_whole, n_tiles, functools.partial(index_tile, causal_cut=True), 0)

    def count(ref, pred):
        def body(ti, acc):
            keys = ref[ti]
            for r in range(kt // span):
                acc = acc + jnp.where(pred(keys[r * span:(r + 1) * span, :]), i16(1), i16(0))
            return acc
        partial = lax.fori_loop(0, n_tiles, body, jnp.zeros((span, qb), i16))
        return jnp.sum(partial.astype(F32), axis=0, keepdims=True)

    def kth_largest(ref, target):
        def bit(it, state):
            thr, n_above = state
            cand = thr + lax.shift_left(jnp.int32(1), 15 - it)
            cand16 = tall(cand.astype(i16))
            n_ge = jnp.broadcast_to(count(ref, lambda keys: keys >= cand16), thr.shape)
            ok = n_ge >= target
            return jnp.where(ok, cand, thr), jnp.where(ok, n_above, n_ge)
        return lax.fori_loop(0, 16, bit, (jnp.full((rows16, qb), -2 ** 15, jnp.int32),
                                          jnp.zeros((rows16, qb), F32)))

    thr_ref[...] = jnp.full_like(thr_ref, INT_MIN)
    need_ref[...] = jnp.zeros_like(need_ref)

    @pl.when(t0 + qb > top_k)
    def _():
        upper, n_upper_gt = kth_largest(hi_ref, float(top_k))
        upper_kt = jnp.concatenate([tall(upper.astype(i16))] * (kt // span), axis=0)

        def keep_bucket(ti, carry):
            lo_ref[ti] = jnp.where(hi_ref[ti] == upper_kt, lo_ref[ti], i16(-2 ** 15))
            return carry

        lax.fori_loop(0, n_tiles, keep_bucket, 0)
        lower, n_lower_gt = kth_largest(lo_ref, top_k - n_upper_gt)
        thr_ref[...] = (lax.shift_left(upper, 16) + (lower + 2 ** 15))[0:SUBLANES]
        need_ref[...] = (top_k - (n_upper_gt + n_lower_gt))[0:SUBLANES]

    thr = thr_ref[0:1, :]
    need = need_ref[0:1, :]

    earlier = (lax.broadcasted_iota(jnp.int32, (kt, kt), 1)
               < lax.broadcasted_iota(jnp.int32, (kt, kt), 0))
    earlier_bf = jnp.where(earlier, 1.0, 0.0).astype(BF16)
    ones_rows = jnp.ones((SUBLANES, kt), BF16)
    dim_v = lax.broadcasted_iota(jnp.int32, (KV_WIDTH, kt), 0)

    def selection(ti, ties_before, causal_cut):
        base = pl.multiple_of(ti * kt, kt)
        keys = key_ref[ti]
        tie = keys == thr
        tie_bf = jnp.where(tie, 1.0, 0.0).astype(BF16)
        rank = _dot(earlier_bf, tie_bf) + ties_before
        chosen = (keys > thr) | (tie & (rank < need))
        if causal_cut:
            chosen = chosen & (base + key_pos <= query_pos)
        return jnp.where(chosen, 0.0, MASK_VALUE), ties_before + _dot(ones_rows, tie_bf)[0:1]

    def values(ti):
        v_t = v_ref[pl.ds(pl.multiple_of(ti * kt, kt), kt), :].astype(F32).T
        return [jnp.where(dim_v // HEAD_DIM == g, v_t, 1.0).astype(BF16) for g in range(N_KV_HEADS)]

    head_cols = (lax.broadcasted_iota(jnp.int32, (ATTN_WIDTH, LANES), 0) // HEAD_DIM
                 == lax.broadcasted_iota(jnp.int32, (ATTN_WIDTH, LANES), 1))
    head_cols = jnp.where(head_cols, 1.0, 0.0).astype(BF16)
    q_sq = q_ref[...] * q_ref[...]
    q_norm2 = jnp.max(_dot(q_sq, head_cols), axis=0, keepdims=True)
    k_norm2 = knorm_ref[0:1, :]
    shift = [jnp.sqrt(jnp.max(q_norm2[:, g * KV_GROUP:(g + 1) * KV_GROUP], axis=1, keepdims=True)
                      * k_norm2[:, g:g + 1]) * SHIFT_MARGIN for g in range(N_KV_HEADS)]
    q_stack = [jnp.concatenate([q_ref[:, h * HEAD_DIM:(h + 1) * HEAD_DIM]
                                for h in range(g * KV_GROUP, (g + 1) * KV_GROUP)], axis=0)
               for g in range(N_KV_HEADS)]

    def scores(ti):
        k_tile = k_ref[pl.ds(pl.multiple_of(ti * kt, kt), kt), :]
        return [_nt_dot(k_tile[:, g * HEAD_DIM:(g + 1) * HEAD_DIM], q_stack[g])
                for g in range(N_KV_HEADS)]

    def fast_tile(ti, ties_before, causal_cut):
        s_all = scores(ti)
        bias, ties_after = selection(ti, ties_before, causal_cut)
        v_aug = values(ti)
        for g in range(N_KV_HEADS):
            shifted = bias - shift[g]
            for hh in range(KV_GROUP):
                p = jnp.exp2(s_all[g][:, hh * qb:(hh + 1) * qb] + shifted)
                acc_ref[g * KV_GROUP + hh] += _dot(v_aug[g], p.astype(BF16))
        return ties_after

    def run(tile_fn):
        ties = lax.fori_loop(0, n_whole, functools.partial(tile_fn, causal_cut=False),
                             jnp.zeros((1, qb), F32))
        lax.fori_loop(n_whole, n_tiles, functools.partial(tile_fn, causal_cut=True), ties)

    denominator = lambda h: acc_ref[h][(1 - h // KV_GROUP) * HEAD_DIM:(1 - h // KV_GROUP) * HEAD_DIM + 1, :]
    acc_ref[...] = jnp.zeros_like(acc_ref)
    run(fast_tile)
    smallest = denominator(0)
    for h in range(1, N_HEADS):
        smallest = jnp.minimum(smallest, denominator(h))

    @pl.when(jnp.logical_not(jnp.min(smallest) >= SAFE_DENOMINATOR))
    def _():
        m_ref[...] = jnp.full_like(m_ref, MASK_VALUE)
        acc_ref[...] = jnp.zeros_like(acc_ref)
        def exact_tile(ti, ties_before, causal_cut):
            s_all = scores(ti)
            bias, ties_after = selection(ti, ties_before, causal_cut)
            v_aug = values(ti)
            for h in range(N_HEADS):
                g, hh = divmod(h, KV_GROUP)
                s = s_all[g][:, hh * qb:(hh + 1) * qb] + bias
                m_old = m_ref[h]
                m_new = jnp.maximum(m_old, jnp.max(s, axis=0, keepdims=True))
                p = jnp.exp2(s - m_new)
                acc_ref[h] = jnp.exp2(m_old - m_new) * acc_ref[h] + _dot(v_aug[g], p.astype(BF16))
                m_ref[h] = m_new
            return ties_after

        run(exact_tile)

    for pair in range(N_HEADS // 2):
        outs = []
        for h in (2 * pair, 2 * pair + 1):
            g = h // KV_GROUP
            outs.append(acc_ref[h][g * HEAD_DIM:(g + 1) * HEAD_DIM, :] / denominator(h))
        out_ref[:, pair * LANES:(pair + 1) * LANES] = (
            jnp.concatenate(outs, axis=0).T.astype(out_ref.dtype))


def _dsa(q, k, v, qi, ki, wi, batch, seq_len):
    n_tok = q.shape[0]
    qb = min(Q_TILE, seq_len)
    kt = min(KEY_TILE, seq_len)
    assert seq_len % qb == 0 and seq_len % kt == 0 and qb % LANES == 0
    assert N_KV_HEADS == 2 and KV_WIDTH == LANES and 2 * HEAD_DIM == LANES
    steps = seq_len // qb
    n_kt = seq_len // kt
    top_k = min(TOPK_MAX, seq_len // 4)
    qtile = lambda w: pl.BlockSpec((qb, w), lambda bi, qi_: (bi * steps + qi_, 0))
    seq = lambda w: pl.BlockSpec((seq_len, w), lambda bi, qi_: (bi, 0))
    return pl.pallas_call(
        functools.partial(_dsa_kernel, top_k=top_k),
        grid=(batch, steps),
        in_specs=[qtile(ATTN_WIDTH), seq(KV_WIDTH), seq(KV_WIDTH),
                  qtile(IDX_HEADS * IDX_CAT), seq(IDX_CAT), qtile(LANES)],
        out_specs=qtile(ATTN_WIDTH),
        out_shape=jax.ShapeDtypeStruct((n_tok, ATTN_WIDTH), BF16),
        scratch_shapes=[pltpu.VMEM((n_kt, kt, qb), jnp.int32),
                        pltpu.VMEM((n_kt, kt, qb), jnp.int16),
                        pltpu.VMEM((n_kt, kt, qb), jnp.int16),
                        pltpu.VMEM((SUBLANES, qb), jnp.int32),
                        pltpu.VMEM((SUBLANES, qb), F32),
                        pltpu.VMEM((SUBLANES, LANES), F32),
                        pltpu.VMEM((N_HEADS, 1, qb), F32),
                        pltpu.VMEM((N_HEADS, KV_WIDTH, qb), F32)],
        compiler_params=pltpu.CompilerParams(dimension_semantics=("arbitrary", "arbitrary"),
                                             vmem_limit_bytes=VMEM_LIMIT_BYTES),
        name="dsa",
    )(q, k, v, qi, ki, wi)


def _out_ffn_kernel(x_ref, yp_ref, yh_ref, ya_ref, wo_hbm, g2_ref, wfi_hbm, wfo_hbm, gf_ref,
                    out_ref, wo_ref, wfi_ref, wfo_ref, *, layer, final_norm):
    @pl.when(pl.program_id(0) == 0)
    def _():
        pltpu.sync_copy(wo_hbm.at[layer], wo_ref)
        pltpu.sync_copy(wfi_hbm.at[layer], wfi_ref)
        pltpu.sync_copy(wfo_hbm.at[layer], wfo_ref)

    x = x_ref[...]
    x = x + _dot(yp_ref[...], wo_ref[0:POOL_WIDTH, :])
    x = x + _dot(yh_ref[...], wo_ref[POOL_WIDTH:POOL_WIDTH + HGRN_WIDTH, :])
    x = x + _dot(ya_ref[...], wo_ref[POOL_WIDTH + HGRN_WIDTH:, :])
    h = _rms_norm(x, g2_ref[...]).astype(BF16)
    a = jnp.maximum(_dot(h, wfi_ref[...]), 0.0)
    x = x + _dot((a * a).astype(BF16), wfo_ref[...])
    if final_norm:
        x = _rms_norm(x, gf_ref[...])
    out_ref[...] = x


def _out_ffn(x2d, y_pool, y_hgrn, y_attn, w_out, norm2_g, w_ff_in, w_ff_out, final_g,
             layer, final_norm, seq_len):
    n_tok, d_model = x2d.shape
    d_ff = w_ff_in.shape[2]
    tm = min(MLP_TILE, seq_len)
    assert seq_len % tm == 0
    tok = lambda w: pl.BlockSpec((tm, w), lambda i: (i, 0))
    hbm = pl.BlockSpec(memory_space=pl.ANY)
    return pl.pallas_call(
        functools.partial(_out_ffn_kernel, layer=layer, final_norm=final_norm),
        grid=(n_tok // tm,),
        in_specs=[tok(d_model), tok(POOL_WIDTH), tok(HGRN_WIDTH), tok(ATTN_WIDTH),
                  hbm, pl.BlockSpec((None, 1, d_model), lambda i: (layer, 0, 0)), hbm, hbm,
                  pl.BlockSpec((1, d_model), lambda i: (0, 0))],
        out_specs=tok(d_model),
        out_shape=jax.ShapeDtypeStruct((n_tok, d_model), F32),
        scratch_shapes=[pltpu.VMEM((d_model, d_model), BF16),
                        pltpu.VMEM((d_model, d_ff), BF16),
                        pltpu.VMEM((d_ff, d_model), BF16)],
        compiler_params=pltpu.CompilerParams(dimension_semantics=("arbitrary",),
                                             vmem_limit_bytes=VMEM_LIMIT_BYTES),
        name="out_ffn",
    )(x2d, y_pool, y_hgrn, y_attn, w_out, norm2_g, w_ff_in, w_ff_out, final_g)


def _rope_patterns(positions):
    inv_freq = ROPE_THETA ** (-jnp.arange(0, ROT_DIM, 2, dtype=F32) / ROT_DIM)
    rest = jnp.zeros((HEAD_DIM - ROT_DIM,), F32)
    freq_head = jnp.concatenate([inv_freq, inv_freq, rest])
    lo_head = jnp.concatenate([-jnp.ones((ROT_HALF,), F32), jnp.zeros((ROT_HALF,), F32), rest])
    hi_head = jnp.concatenate([jnp.zeros((ROT_HALF,), F32), jnp.ones((ROT_HALF,), F32), rest])
    two = lambda a: jnp.concatenate([a, a])[None, :]
    ang = positions.astype(F32).reshape(-1, 1) * two(freq_head)
    sin = jnp.sin(ang)
    return jnp.cos(ang), sin * two(lo_head), sin * two(hi_head)


def kernel(x, positions, norm1_g, w_in, pool_w, pool_scale, lb_logits, hgrn_norm_g, w_out,
           norm2_g, w_ff_in, w_ff_out, final_norm_g):
    batch, seq_len, d_model = x.shape
    depth = w_in.shape[0]
    assert w_in.shape[2] == D_IN

    p_lb = jax.nn.softmax(lb_logits.astype(F32), axis=0)
    lower_bounds = (jnp.cumsum(p_lb, axis=0) - p_lb[0])[:, None, :]
    w_in_pad = jnp.pad(w_in, ((0, 0), (0, 0), (0, D_IN_PAD - D_IN))).astype(BF16)
    groups = len(POOL_WINDOWS)
    eye = jnp.eye(groups, dtype=F32)
    pool_w_bd = (pool_w[:, :, :, None, :] * eye[None, :, None, :, None]).reshape(
        depth, POOL_WIDTH, POOL_WIDTH).astype(BF16)
    w_out_b, w_ff_in_b, w_ff_out_b = (w.astype(BF16) for w in (w_out, w_ff_in, w_ff_out))
    row = lambda a: a.reshape(depth, 1, -1).astype(F32)
    norm1, norm2, pscale, hnorm = row(norm1_g), row(norm2_g), row(pool_scale), row(hgrn_norm_g)
    final_g = final_norm_g.reshape(1, d_model).astype(F32)
    cos, slo, shi = _rope_patterns(positions)

    x2d = x.reshape(batch * seq_len, d_model)
    for layer in range(depth):
        y_pool, hg, q, k, v, qi, ki, wi = _in_proj(x2d, norm1, w_in_pad, cos, slo, shi, pool_w_bd,
                                                pscale, layer, seq_len)
        y_hgrn = _hgrn(hg, lower_bounds, hnorm, layer, batch, seq_len)
        y_attn = _dsa(q, k, v, qi, ki, wi, batch, seq_len)
        x2d = _out_ffn(x2d, y_pool, y_hgrn, y_attn, w_out_b, norm2, w_ff_in_b, w_ff_out_b,
                       final_g, layer, layer == depth - 1, seq_len)
    return x2d.reshape(batch, seq_len, d_model)
```

```python
import functools

import jax
import jax.numpy as jnp
from jax import lax
from jax.experimental import pallas as pl
from jax.experimental.pallas import tpu as pltpu

F32 = jnp.float32
BF16 = jnp.bfloat16

POOL_WINDOWS = (2, 4, 8, 16)
POOL_WIDTH = 256
POOL_GROUP_DIM = 64
HGRN_WIDTH = 256
HGRN_HEAD_DIM = 64
LOG_F_FLOOR = 1e-30
ATTN_WIDTH = 512
HEAD_DIM = 64
N_HEADS = 8
N_KV_HEADS = 2
KV_GROUP = N_HEADS // N_KV_HEADS
KV_WIDTH = N_KV_HEADS * HEAD_DIM
IDX_HEADS = 4
IDX_DIM = 64
TOPK_MAX = 256
MASK_VALUE = -1e30
ROPE_THETA = 500000.0
ROT_DIM = HEAD_DIM // 4
ROT_HALF = ROT_DIM // 2
RMS_EPS = 1e-5
IDX_W_SCALE = (IDX_HEADS ** -0.5) * (IDX_DIM ** -0.5)
ATTN_SCALE = HEAD_DIM ** -0.5
LOG2_E = 1.4426950408889634
SHIFT_MARGIN = 1.02
SAFE_DENOMINATOR = 2.0 ** -60

LANES = 128
SUBLANES = 8
VMEM_LIMIT_BYTES = 56 * 1024 * 1024

COL_POOL = 0
COL_HGRN = COL_POOL + POOL_WIDTH
COL_Q = COL_HGRN + 4 * HGRN_WIDTH
COL_K = COL_Q + ATTN_WIDTH
COL_V = COL_K + KV_WIDTH
COL_QI = COL_V + KV_WIDTH
COL_KI = COL_QI + IDX_HEADS * IDX_DIM
D_IN = COL_KI + IDX_DIM + IDX_HEADS
D_IN_PAD = COL_KI + LANES
IDX_CAT = 4 * IDX_DIM

PROJ_TILE = 1024
MLP_TILE = 512
PROJ_SUB = 128
HGRN_TILE = 512
HGRN_CHUNK = 16
Q_TILE = 512
KEY_TILE = 512
RANK_BLOCK = 128
INT_MIN = -2 ** 31


def _nt_dot(a, b):
    return lax.dot_general(a, b, (((1,), (1,)), ((), ())), preferred_element_type=F32)


def _tn_dot(a, b):
    return lax.dot_general(a, b, (((0,), (0,)), ((), ())), preferred_element_type=F32)


def _dot(a, b):
    return jnp.dot(a, b, preferred_element_type=F32)


def _split_bf16(a):
    hi = a.astype(BF16)
    lo = (a - hi.astype(F32)).astype(BF16)
    return hi, lo


def _rms_norm(x, g):
    return x * lax.rsqrt(jnp.mean(x * x, axis=-1, keepdims=True) + RMS_EPS) * g


def _silu(x):
    return x * jax.nn.sigmoid(x)


def _rope(x, cos, sin_lo, sin_hi):
    w = x.shape[1]
    rep = w // LANES
    if rep > 1:
        cos = jnp.concatenate([cos] * rep, axis=1)
        sin_lo = jnp.concatenate([sin_lo] * rep, axis=1)
        sin_hi = jnp.concatenate([sin_hi] * rep, axis=1)
    from_below = pltpu.roll(x, ROT_HALF, 1)
    from_above = pltpu.roll(x, w - ROT_HALF, 1)
    return x * cos + from_below * sin_hi + from_above * sin_lo


def _in_proj_kernel(x_ref, g_ref, w_ref, cos_ref, slo_ref, shi_ref, pw_ref, pscale_ref,
                    ypool_ref, hg_ref, q_ref, k_ref, v_ref, qi_ref, ki_ref, wi_ref,
                    halo_ref, *, tiles_per_seq):
    tm = x_ref.shape[0]
    sub = min(PROJ_SUB, tm)
    seq_tile = pl.program_id(0) % tiles_per_seq

    @pl.when(seq_tile == 0)
    def _():
        halo_ref[...] = jnp.zeros_like(halo_ref)

    prev = halo_ref[...]
    for r in range(tm // sub):
        prev = _in_proj_rows(slice(r * sub, (r + 1) * sub), seq_tile * tm + r * sub, prev,
                             x_ref, g_ref, w_ref, cos_ref, slo_ref, shi_ref, pw_ref, pscale_ref,
                             ypool_ref, hg_ref, q_ref, k_ref, v_ref, qi_ref, ki_ref, wi_ref)
    halo_ref[...] = prev


def _in_proj_rows(rows, t_start, prev, x_ref, g_ref, w_ref, cos_ref, slo_ref, shi_ref, pw_ref,
                  pscale_ref, ypool_ref, hg_ref, q_ref, k_ref, v_ref, qi_ref, ki_ref, wi_ref):
    tm = rows.stop - rows.start
    halo = prev.shape[0]
    h = _rms_norm(x_ref[rows, :], g_ref[...])
    proj = _dot(h.astype(BF16), w_ref[...])

    u = proj[:, COL_POOL:COL_POOL + POOL_WIDTH]
    ext = jnp.concatenate([prev, u], axis=0)
    sums = []
    acc = ext
    span = 1
    for win in POOL_WINDOWS:
        while span < win:
            acc = acc + pltpu.roll(acc, span, 0)
            span *= 2
        sums.append(acc[halo:, :])
    lane = lax.broadcasted_iota(jnp.int32, (tm, POOL_WIDTH), 1)
    group = lane // POOL_GROUP_DIM
    wsum = sums[-1]
    win_lane = jnp.full((tm, POOL_WIDTH), POOL_WINDOWS[-1], jnp.int32)
    for gi in range(len(POOL_WINDOWS) - 2, -1, -1):
        wsum = jnp.where(group == gi, sums[gi], wsum)
        win_lane = jnp.where(group == gi, POOL_WINDOWS[gi], win_lane)
    t_pos = t_start + lax.broadcasted_iota(jnp.int32, (tm, POOL_WIDTH), 0)
    count = jnp.minimum(t_pos + 1, win_lane).astype(F32)
    pooled = wsum / count - u
    y_pool = _dot(pooled.astype(BF16), pw_ref[...]) * pscale_ref[...]
    ypool_ref[rows, :] = y_pool.astype(ypool_ref.dtype)

    hg_ref[rows, :] = proj[:, COL_HGRN:COL_HGRN + 4 * HGRN_WIDTH]

    cos, slo, shi = cos_ref[rows, :], slo_ref[rows, :], shi_ref[rows, :]
    q = _rope(proj[:, COL_Q:COL_Q + ATTN_WIDTH], cos, slo, shi) * (ATTN_SCALE * LOG2_E)
    q_ref[rows, :] = q.astype(q_ref.dtype)
    k = _rope(proj[:, COL_K:COL_K + KV_WIDTH], cos, slo, shi)
    k_ref[rows, :] = k.astype(k_ref.dtype)
    v_ref[rows, :] = proj[:, COL_V:COL_V + KV_WIDTH].astype(v_ref.dtype)
    qi = _rope(proj[:, COL_QI:COL_QI + IDX_HEADS * IDX_DIM], cos, slo, shi)
    qi_hi = qi.astype(BF16).astype(F32)
    qi_lo = qi - qi_hi
    zeros = jnp.zeros((tm, IDX_DIM), F32)
    pieces = []
    for hd in range(IDX_HEADS):
        cols = slice(hd * IDX_DIM, (hd + 1) * IDX_DIM)
        pieces += [qi_hi[:, cols], qi_hi[:, cols], qi_lo[:, cols], zeros]
    qi_ref[rows, :] = jnp.concatenate(pieces, axis=1).astype(qi_ref.dtype)
    kiw_raw = proj[:, COL_KI:COL_KI + LANES]
    ki = _rope(kiw_raw, cos, slo, shi)[:, 0:IDX_DIM]
    ki_hi = ki.astype(BF16).astype(F32)
    ki_ref[rows, :] = jnp.concatenate([ki_hi, ki - ki_hi, ki_hi, zeros], axis=1).astype(ki_ref.dtype)
    wi_ref[rows, :] = kiw_raw * IDX_W_SCALE
    return u[tm - halo:, :]


def _in_proj(x2d, norm_g, w_in_pad, cos, slo, shi, pool_w_bd, pool_scale, layer, seq_len):
    n_tok, d_model = x2d.shape
    tm = min(PROJ_TILE, seq_len)
    assert seq_len % tm == 0 and n_tok % tm == 0
    halo = 2 * SUBLANES
    assert halo >= POOL_WINDOWS[-1] and tm >= halo
    tok = lambda w: pl.BlockSpec((tm, w), lambda i: (i, 0))
    lay = lambda *s: pl.BlockSpec((None,) + s, lambda i: (layer,) + (0,) * len(s))
    out_shape = (
        jax.ShapeDtypeStruct((n_tok, POOL_WIDTH), BF16),
        jax.ShapeDtypeStruct((n_tok, 4 * HGRN_WIDTH), F32),
        jax.ShapeDtypeStruct((n_tok, ATTN_WIDTH), BF16),
        jax.ShapeDtypeStruct((n_tok, KV_WIDTH), BF16),
        jax.ShapeDtypeStruct((n_tok, KV_WIDTH), BF16),
        jax.ShapeDtypeStruct((n_tok, IDX_HEADS * IDX_CAT), BF16),
        jax.ShapeDtypeStruct((n_tok, IDX_CAT), BF16),
        jax.ShapeDtypeStruct((n_tok, LANES), F32),
    )
    return pl.pallas_call(
        functools.partial(_in_proj_kernel, tiles_per_seq=seq_len // tm),
        grid=(n_tok // tm,),
        in_specs=[tok(d_model), lay(1, d_model), lay(d_model, D_IN_PAD),
                  tok(LANES), tok(LANES), tok(LANES),
                  lay(POOL_WIDTH, POOL_WIDTH), lay(1, POOL_WIDTH)],
        out_specs=tuple(tok(s.shape[1]) for s in out_shape),
        out_shape=out_shape,
        scratch_shapes=[pltpu.VMEM((halo, POOL_WIDTH), F32)],
        compiler_params=pltpu.CompilerParams(dimension_semantics=("arbitrary",),
                                             vmem_limit_bytes=VMEM_LIMIT_BYTES),
        name="in_proj",
    )(x2d, norm_g, w_in_pad, cos, slo, shi, pool_w_bd, pool_scale)


def _hgrn_kernel(hg_ref, lb_ref, ng_ref, out_ref, state_ref, o_ref):
    tb, width = out_ref.shape
    c = HGRN_CHUNK
    half = c // 2
    n_chunks = tb // c
    heads = width // HGRN_HEAD_DIM

    @pl.when(pl.program_id(1) == 0)
    def _():
        state_ref[...] = jnp.zeros_like(state_ref)

    lb = lb_ref[...]
    q = _silu(hg_ref[:, 0:width])
    sig = jax.nn.sigmoid(hg_ref[:, width:2 * width])
    f = lb + (1.0 - lb) * sig
    log2_f = jnp.log(jnp.maximum(f, LOG_F_FLOOR)) * LOG2_E
    k = (1.0 - lb) * (1.0 - sig)
    v = hg_ref[:, 2 * width:3 * width]

    row_in_chunk = lax.broadcasted_iota(jnp.int32, (tb, width), 0) % c
    b = log2_f
    step = 1
    while step < c:
        b = b + jnp.where(row_in_chunk >= step, pltpu.roll(b, step, 0), 0.0)
        step *= 2

    lane_r = lax.broadcasted_iota(jnp.int32, (width, width), 0) // HGRN_HEAD_DIM
    lane_c = lax.broadcasted_iota(jnp.int32, (width, width), 1) // HGRN_HEAD_DIM
    head_sum = jnp.where(lane_r == lane_c, 1.0, 0.0).astype(BF16)

    tiles = lambda a: a.reshape(tb // half, half, width)
    q3, k3, b3, v3 = tiles(q), tiles(k), tiles(b), tiles(v)
    row_in_tile = lax.broadcasted_iota(jnp.int32, (tb // half, half, width), 1)
    o_near = None
    for s in range(half):
        if s == 0:
            p, vs = q3 * k3, v3
        else:
            ks, bs, vs = (pltpu.roll(a, s, 1) for a in (k3, b3, v3))
            p = jnp.where(row_in_tile >= s, q3 * ks * jnp.exp2(b3 - bs), 0.0)
        score = _dot(p.reshape(tb, width).astype(BF16), head_sum)
        term = score * vs.reshape(tb, width)
        o_near = term if o_near is None else o_near + term
    o_ref[...] = o_near
    first = lambda a: jnp.concatenate([a[ci * c:ci * c + half] for ci in range(n_chunks)], axis=0)
    second = lambda a: jnp.concatenate([a[ci * c + half:(ci + 1) * c] for ci in range(n_chunks)], axis=0)
    hb = tb // 2
    b1, b2 = first(b), second(b)
    b_mid = jnp.concatenate([jnp.broadcast_to(b1[(ci + 1) * half - 1:(ci + 1) * half], (half, width))
                             for ci in range(n_chunks)], axis=0)
    q_mid = second(q) * jnp.exp2(b2 - b_mid)
    k_mid = (first(k) * jnp.exp2(b_mid - b1)).astype(BF16)
    head_of_lane = lax.broadcasted_iota(jnp.int32, (hb, width), 1) // HGRN_HEAD_DIM
    q_heads = jnp.concatenate([jnp.where(head_of_lane == hd, q_mid, 0.0) for hd in range(heads)], axis=0)
    scores = _nt_dot(q_heads.astype(BF16), k_mid)
    t_chunk = (lax.broadcasted_iota(jnp.int32, (heads * hb, hb), 0) % hb) // half
    s_chunk = lax.broadcasted_iota(jnp.int32, (heads * hb, hb), 1) // half
    scores = jnp.where(t_chunk == s_chunk, scores, 0.0).astype(BF16)
    o_heads = _dot(scores, first(v).astype(BF16))
    o_far = jnp.where(head_of_lane == 0, o_heads[0:hb], 0.0)
    for hd in range(1, heads):
        o_far = jnp.where(head_of_lane == hd, o_heads[hd * hb:(hd + 1) * hb], o_far)
    for ci in range(n_chunks):
        o_ref[ci * c + half:(ci + 1) * c, :] += o_far[ci * half:(ci + 1) * half]

    lane_head = lax.broadcasted_iota(jnp.int32, (c, width), 1) // HGRN_HEAD_DIM
    for ci in range(n_chunks):
        rows = slice(ci * c, (ci + 1) * c)
        bc = b[rows]
        b_last = bc[c - 1:c]
        state = state_ref[...]
        q_dec = q[rows] * jnp.exp2(bc)
        q_heads = jnp.concatenate([jnp.where(lane_head == hd, q_dec, 0.0) for hd in range(heads)], axis=0)
        o_heads = _nt_dot(q_heads.astype(BF16), state.astype(BF16))
        o_inter = jnp.where(lane_head == 0, o_heads[0:c], 0.0)
        for hd in range(1, heads):
            o_inter = jnp.where(lane_head == hd, o_heads[hd * c:(hd + 1) * c], o_inter)
        o_ref[rows, :] += o_inter
        k_dec = k[rows] * jnp.exp2(b_last - bc)
        upd = _tn_dot(v[rows].astype(BF16), k_dec.astype(BF16))
        state_ref[...] = state * jnp.exp2(b_last) + upd

    o = o_ref[...]
    sq_hi, sq_lo = _split_bf16(o * o)
    mean_sq = (_dot(sq_hi, head_sum) + _dot(sq_lo, head_sum)) * (1.0 / HGRN_HEAD_DIM)
    gate = _silu(hg_ref[:, 3 * width:4 * width])
    out_ref[...] = (o * lax.rsqrt(mean_sq + RMS_EPS) * ng_ref[...] * gate).astype(out_ref.dtype)


def _hgrn(hg, lower_bound, norm_g, layer, batch, seq_len):
    n_tok = hg.shape[0]
    tb = min(HGRN_TILE, seq_len)
    assert seq_len % tb == 0 and tb % HGRN_CHUNK == 0 and HGRN_CHUNK == 2 * SUBLANES
    steps = seq_len // tb
    lay = pl.BlockSpec((None, 1, HGRN_WIDTH), lambda bi, ti: (layer, 0, 0))
    return pl.pallas_call(
        _hgrn_kernel,
        grid=(batch, steps),
        in_specs=[pl.BlockSpec((tb, 4 * HGRN_WIDTH), lambda bi, ti: (bi * steps + ti, 0)),
                  lay, lay],
        out_specs=pl.BlockSpec((tb, HGRN_WIDTH), lambda bi, ti: (bi * steps + ti, 0)),
        out_shape=jax.ShapeDtypeStruct((n_tok, HGRN_WIDTH), BF16),
        scratch_shapes=[pltpu.VMEM((HGRN_WIDTH, HGRN_WIDTH), F32),
                        pltpu.VMEM((tb, HGRN_WIDTH), F32)],
        compiler_params=pltpu.CompilerParams(dimension_semantics=("arbitrary", "arbitrary"),
                                             vmem_limit_bytes=VMEM_LIMIT_BYTES),
        name="hgrn2",
    )(hg, lower_bound, norm_g)


def _dsa_kernel(q_ref, k_ref, v_ref, qi_ref, ki_ref, wi_ref, out_ref,
                key_ref, hi_ref, lo_ref, thr_ref, need_ref, knorm_ref, m_ref, acc_ref, *, top_k):
    qb = q_ref.shape[0]
    kt = key_ref.shape[1]
    i16 = jnp.int16
    rows16 = 2 * SUBLANES
    span = 4 * rows16
    tall = lambda a: jnp.concatenate([a] * (span // rows16), axis=0)
    t0 = pl.program_id(1) * qb
    n_tiles = (t0 + qb + kt - 1) // kt
    n_whole = (t0 + 1) // kt

    key_pos = lax.broadcasted_iota(jnp.int32, (kt, qb), 0)
    query_pos = t0 + lax.broadcasted_iota(jnp.int32, (kt, qb), 1)

    @pl.when(pl.program_id(1) == 0)
    def _():
        kv_cols = (lax.broadcasted_iota(jnp.int32, (KV_WIDTH, LANES), 0) // HEAD_DIM
                   == lax.broadcasted_iota(jnp.int32, (KV_WIDTH, LANES), 1))
        kv_cols = jnp.where(kv_cols, 1.0, 0.0).astype(BF16)

        def key_norms(ti, best):
            k_f32 = k_ref[pl.ds(pl.multiple_of(ti * kt, kt), kt), :].astype(F32)
            kk_hi, kk_lo = _split_bf16(k_f32 * k_f32)
            norm2 = _dot(kk_hi, kv_cols) + _dot(kk_lo, kv_cols)
            return jnp.maximum(best, jnp.max(norm2, axis=0, keepdims=True))

        best = lax.fori_loop(0, k_ref.shape[0] // kt, key_norms, jnp.zeros((1, LANES), F32))
        knorm_ref[...] = jnp.broadcast_to(best, knorm_ref.shape)

    wi_t = wi_ref[...].T
    w_heads = [wi_t[IDX_DIM + h:IDX_DIM + h + 1, :] for h in range(IDX_HEADS)]

    def index_tile(ti, carry, causal_cut):
        base = pl.multiple_of(ti * kt, kt)
        ki_tile = ki_ref[pl.ds(base, kt), :]
        score = jnp.zeros((kt, qb), F32)
        for h in range(IDX_HEADS):
            s = _nt_dot(ki_tile, qi_ref[:, h * IDX_CAT:(h + 1) * IDX_CAT])
            score = score + jnp.maximum(s, 0.0) * w_heads[h]
        score = jnp.where(score == 0.0, 0.0, score)
        if causal_cut:
            score = jnp.where(base + key_pos <= query_pos, score, MASK_VALUE)
        bits = lax.bitcast_convert_type(score, jnp.int32)
        key = bits ^ ((bits >> 31) & 0x7FFFFFFF)
        key_ref[ti] = key
        hi_ref[ti] = (key >> 16).astype(i16)
        lo_ref[ti] = ((key & 0xFFFF) - 2 ** 15).astype(i16)
        return carry

    lax.fori_loop(0, n_whole, functools.partial(index_tile, causal_cut=False), 0)
    lax.fori_loop(n_whole, n_tiles, functools.partial(index_tile, causal_cut=True), 0)

    def count(ref, pred):
        def body(ti, acc):
            keys = ref[ti]
            for r in range(kt // span):
                acc = acc + jnp.where(pred(keys[r * span:(r + 1) * span, :]), i16(1), i16(0))
            return acc
        partial = lax.fori_loop(0, n_tiles, body, jnp.zeros((span, qb), i16))
        return jnp.sum(partial.astype(F32), axis=0, keepdims=True)

    def kth_largest(ref, target):
        def bit(it, state):
            thr, n_above = state
            cand = thr + lax.shift_left(jnp.int32(1), 15 - it)
            cand16 = tall(cand.astype(i16))
            n_ge = jnp.broadcast_to(count(ref, lambda keys: keys >= cand16), thr.shape)
            ok = n_ge >= target
            return jnp.where(ok, cand, thr), jnp.where(ok, n_above, n_ge)
        return lax.fori_loop(0, 16, bit, (jnp.full((rows16, qb), -2 ** 15, jnp.int32),
                                          jnp.zeros((rows16, qb), F32)))

    thr_ref[...] = jnp.full_like(thr_ref, INT_MIN)
    need_ref[...] = jnp.zeros_like(need_ref)

    @pl.when(t0 + qb > top_k)
    def _():
        upper, n_upper_gt = kth_largest(hi_ref, float(top_k))
        upper_kt = jnp.concatenate([tall(upper.astype(i16))] * (kt // span), axis=0)

        def keep_bucket(ti, carry):
            lo_ref[ti] = jnp.where(hi_ref[ti] == upper_kt, lo_ref[ti], i16(-2 ** 15))
            return carry

        lax.fori_loop(0, n_tiles, keep_bucket, 0)
        lower, n_lower_gt = kth_largest(lo_ref, top_k - n_upper_gt)
        thr_ref[...] = (lax.shift_left(upper, 16) + (lower + 2 ** 15))[0:SUBLANES]
        need_ref[...] = (top_k - (n_upper_gt + n_lower_gt))[0:SUBLANES]

    thr = thr_ref[0:1, :]
    need = need_ref[0:1, :]

    earlier = (lax.broadcasted_iota(jnp.int32, (RANK_BLOCK, RANK_BLOCK), 1)
               < lax.broadcasted_iota(jnp.int32, (RANK_BLOCK, RANK_BLOCK), 0))
    earlier_bf = jnp.where(earlier, 1.0, 0.0).astype(BF16)
    ones_rows = jnp.ones((SUBLANES, RANK_BLOCK), BF16)
    dim_v = lax.broadcasted_iota(jnp.int32, (KV_WIDTH, kt), 0)

    def selection(ti, ties_before, causal_cut):
        base = pl.multiple_of(ti * kt, kt)
        keys = key_ref[ti]
        tie = keys == thr
        tie_bf = jnp.where(tie, 1.0, 0.0).astype(BF16)
        ranks, before = [], ties_before
        for r in range(kt // RANK_BLOCK):
            block = tie_bf[r * RANK_BLOCK:(r + 1) * RANK_BLOCK]
            ranks.append(_dot(earlier_bf, block) + before)
            before = before + _dot(ones_rows, block)[0:1]
        rank = jnp.concatenate(ranks, axis=0)
        chosen = (keys > thr) | (tie & (rank < need))
        if causal_cut:
            chosen = chosen & (base + key_pos <= query_pos)
        return jnp.where(chosen, 0.0, MASK_VALUE), before

    def values(ti):
        v_t = v_ref[pl.ds(pl.multiple_of(ti * kt, kt), kt), :].astype(F32).T
        return [jnp.where(dim_v // HEAD_DIM == g, v_t, 1.0).astype(BF16) for g in range(N_KV_HEADS)]

    head_cols = (lax.broadcasted_iota(jnp.int32, (ATTN_WIDTH, LANES), 0) // HEAD_DIM
                 == lax.broadcasted_iota(jnp.int32, (ATTN_WIDTH, LANES), 1))
    head_cols = jnp.where(head_cols, 1.0, 0.0).astype(BF16)
    q_sq = q_ref[...] * q_ref[...]
    q_norm2 = jnp.max(_dot(q_sq, head_cols), axis=0, keepdims=True)
    k_norm2 = knorm_ref[0:1, :]
    shift = [jnp.sqrt(jnp.max(q_norm2[:, g * KV_GROUP:(g + 1) * KV_GROUP], axis=1, keepdims=True)
                      * k_norm2[:, g:g + 1]) * SHIFT_MARGIN for g in range(N_KV_HEADS)]
    q_stack = [jnp.concatenate([q_ref[:, h * HEAD_DIM:(h + 1) * HEAD_DIM]
                                for h in range(g * KV_GROUP, (g + 1) * KV_GROUP)], axis=0)
               for g in range(N_KV_HEADS)]

    def scores(ti):
        k_tile = k_ref[pl.ds(pl.multiple_of(ti * kt, kt), kt), :]
        return [_nt_dot(k_tile[:, g * HEAD_DIM:(g + 1) * HEAD_DIM], q_stack[g])
                for g in range(N_KV_HEADS)]

    def fast_tile(ti, ties_before, causal_cut):
        s_all = scores(ti)
        bias, ties_after = selection(ti, ties_before, causal_cut)
        v_aug = values(ti)
        for g in range(N_KV_HEADS):
            shifted = bias - shift[g]
            for hh in range(KV_GROUP):
                p = jnp.exp2(s_all[g][:, hh * qb:(hh + 1) * qb] + shifted)
                acc_ref[g * KV_GROUP + hh] += _dot(v_aug[g], p.astype(BF16))
        return ties_after

    def run(tile_fn):
        ties = lax.fori_loop(0, n_whole, functools.partial(tile_fn, causal_cut=False),
                             jnp.zeros((1, qb), F32))
        lax.fori_loop(n_whole, n_tiles, functools.partial(tile_fn, causal_cut=True), ties)

    denominator = lambda h: acc_ref[h][(1 - h // KV_GROUP) * HEAD_DIM:(1 - h // KV_GROUP) * HEAD_DIM + 1, :]
    acc_ref[...] = jnp.zeros_like(acc_ref)
    run(fast_tile)
    smallest = denominator(0)
    for h in range(1, N_HEADS):
        smallest = jnp.minimum(smallest, denominator(h))

    @pl.when(jnp.logical_not(jnp.min(smallest) >= SAFE_DENOMINATOR))
    def _():
        m_ref[...] = jnp.full_like(m_ref, MASK_VALUE)
        acc_ref[...] = jnp.zeros_like(acc_ref)
        def exact_tile(ti, ties_before, causal_cut):
            s_all = scores(ti)
            bias, ties_after = selection(ti, ties_before, causal_cut)
            v_aug = values(ti)
            for h in range(N_HEADS):
                g, hh = divmod(h, KV_GROUP)
                s = s_all[g][:, hh * qb:(hh + 1) * qb] + bias
                m_old = m_ref[h]
                m_new = jnp.maximum(m_old, jnp.max(s, axis=0, keepdims=True))
                p = jnp.exp2(s - m_new)
                acc_ref[h] = jnp.exp2(m_old - m_new) * acc_ref[h] + _dot(v_aug[g], p.astype(BF16))
                m_ref[h] = m_new
            return ties_after

        run(exact_tile)

    for pair in range(N_HEADS // 2):
        outs = []
        for h in (2 * pair, 2 * pair + 1):
            g = h // KV_GROUP
            outs.append(acc_ref[h][g * HEAD_DIM:(g + 1) * HEAD_DIM, :] / denominator(h))
        out_ref[:, pair * LANES:(pair + 1) * LANES] = (
            jnp.concatenate(outs, axis=0).T.astype(out_ref.dtype))


def _dsa(q, k, v, qi, ki, wi, batch, seq_len):
    n_tok = q.shape[0]
    qb = min(Q_TILE, seq_len)
    kt = min(KEY_TILE, seq_len)
    assert seq_len % qb == 0 and seq_len % kt == 0 and qb % LANES == 0
    assert N_KV_HEADS == 2 and KV_WIDTH == LANES and 2 * HEAD_DIM == LANES
    steps = seq_len // qb
    n_kt = seq_len // kt
    top_k = min(TOPK_MAX, seq_len // 4)
    qtile = lambda w: pl.BlockSpec((qb, w), lambda bi, qi_: (bi * steps + qi_, 0))
    seq = lambda w: pl.BlockSpec((seq_len, w), lambda bi, qi_: (bi, 0))
    return pl.pallas_call(
        functools.partial(_dsa_kernel, top_k=top_k),
        grid=(batch, steps),
        in_specs=[qtile(ATTN_WIDTH), seq(KV_WIDTH), seq(KV_WIDTH),
                  qtile(IDX_HEADS * IDX_CAT), seq(IDX_CAT), qtile(LANES)],
        out_specs=qtile(ATTN_WIDTH),
        out_shape=jax.ShapeDtypeStruct((n_tok, ATTN_WIDTH), BF16),
        scratch_shapes=[pltpu.VMEM((n_kt, kt, qb), jnp.int32),
                        pltpu.VMEM((n_kt, kt, qb), jnp.int16),
                        pltpu.VMEM((n_kt, kt, qb), jnp.int16),
                        pltpu.VMEM((SUBLANES, qb), jnp.int32),
                        pltpu.VMEM((SUBLANES, qb), F32),
                        pltpu.VMEM((SUBLANES, LANES), F32),
                        pltpu.VMEM((N_HEADS, 1, qb), F32),
                        pltpu.VMEM((N_HEADS, KV_WIDTH, qb), F32)],
        compiler_params=pltpu.CompilerParams(dimension_semantics=("arbitrary", "arbitrary"),
                                             vmem_limit_bytes=VMEM_LIMIT_BYTES),
        name="dsa",
    )(q, k, v, qi, ki, wi)


def _out_ffn_kernel(x_ref, yp_ref, yh_ref, ya_ref, wo_hbm, g2_ref, wfi_hbm, wfo_hbm, gf_ref,
                    out_ref, wo_ref, wfi_ref, wfo_ref, *, layer, final_norm):
    @pl.when(pl.program_id(0) == 0)
    def _():
        pltpu.sync_copy(wo_hbm.at[layer], wo_ref)
        pltpu.sync_copy(wfi_hbm.at[layer], wfi_ref)
        pltpu.sync_copy(wfo_hbm.at[layer], wfo_ref)

    x = x_ref[...]
    x = x + _dot(yp_ref[...], wo_ref[0:POOL_WIDTH, :])
    x = x + _dot(yh_ref[...], wo_ref[POOL_WIDTH:POOL_WIDTH + HGRN_WIDTH, :])
    x = x + _dot(ya_ref[...], wo_ref[POOL_WIDTH + HGRN_WIDTH:, :])
    h = _rms_norm(x, g2_ref[...]).astype(BF16)
    a = jnp.maximum(_dot(h, wfi_ref[...]), 0.0)
    x = x + _dot((a * a).astype(BF16), wfo_ref[...])
    if final_norm:
        x = _rms_norm(x, gf_ref[...])
    out_ref[...] = x


def _out_ffn(x2d, y_pool, y_hgrn, y_attn, w_out, norm2_g, w_ff_in, w_ff_out, final_g,
             layer, final_norm, seq_len):
    n_tok, d_model = x2d.shape
    d_ff = w_ff_in.shape[2]
    tm = min(MLP_TILE, seq_len)
    assert seq_len % tm == 0
    tok = lambda w: pl.BlockSpec((tm, w), lambda i: (i, 0))
    hbm = pl.BlockSpec(memory_space=pl.ANY)
    return pl.pallas_call(
        functools.partial(_out_ffn_kernel, layer=layer, final_norm=final_norm),
        grid=(n_tok // tm,),
        in_specs=[tok(d_model), tok(POOL_WIDTH), tok(HGRN_WIDTH), tok(ATTN_WIDTH),
                  hbm, pl.BlockSpec((None, 1, d_model), lambda i: (layer, 0, 0)), hbm, hbm,
                  pl.BlockSpec((1, d_model), lambda i: (0, 0))],
        out_specs=tok(d_model),
        out_shape=jax.ShapeDtypeStruct((n_tok, d_model), F32),
        scratch_shapes=[pltpu.VMEM((d_model, d_model), BF16),
                        pltpu.VMEM((d_model, d_ff), BF16),
                        pltpu.VMEM((d_ff, d_model), BF16)],
        compiler_params=pltpu.CompilerParams(dimension_semantics=("arbitrary",),
                                             vmem_limit_bytes=VMEM_LIMIT_BYTES),
        name="out_ffn",
    )(x2d, y_pool, y_hgrn, y_attn, w_out, norm2_g, w_ff_in, w_ff_out, final_g)


def _rope_patterns(positions):
    inv_freq = ROPE_THETA ** (-jnp.arange(0, ROT_DIM, 2, dtype=F32) / ROT_DIM)
    rest = jnp.zeros((HEAD_DIM - ROT_DIM,), F32)
    freq_head = jnp.concatenate([inv_freq, inv_freq, rest])
    lo_head = jnp.concatenate([-jnp.ones((ROT_HALF,), F32), jnp.zeros((ROT_HALF,), F32), rest])
    hi_head = jnp.concatenate([jnp.zeros((ROT_HALF,), F32), jnp.ones((ROT_HALF,), F32), rest])
    two = lambda a: jnp.concatenate([a, a])[None, :]
    ang = positions.astype(F32).reshape(-1, 1) * two(freq_head)
    sin = jnp.sin(ang)
    return jnp.cos(ang), sin * two(lo_head), sin * two(hi_head)


def kernel(x, positions, norm1_g, w_in, pool_w, pool_scale, lb_logits, hgrn_norm_g, w_out,
           norm2_g, w_ff_in, w_ff_out, final_norm_g):
    batch, seq_len, d_model = x.shape
    depth = w_in.shape[0]
    assert w_in.shape[2] == D_IN

    p_lb = jax.nn.softmax(lb_logits.astype(F32), axis=0)
    lower_bounds = (jnp.cumsum(p_lb, axis=0) - p_lb[0])[:, None, :]
    w_in_pad = jnp.pad(w_in, ((0, 0), (0, 0), (0, D_IN_PAD - D_IN))).astype(BF16)
    groups = len(POOL_WINDOWS)
    eye = jnp.eye(groups, dtype=F32)
    pool_w_bd = (pool_w[:, :, :, None, :] * eye[None, :, None, :, None]).reshape(
        depth, POOL_WIDTH, POOL_WIDTH).astype(BF16)
    w_out_b, w_ff_in_b, w_ff_out_b = (w.astype(BF16) for w in (w_out, w_ff_in, w_ff_out))
    row = lambda a: a.reshape(depth, 1, -1).astype(F32)
    norm1, norm2, pscale, hnorm = row(norm1_g), row(norm2_g), row(pool_scale), row(hgrn_norm_g)
    final_g = final_norm_g.reshape(1, d_model).astype(F32)
    cos, slo, shi = _rope_patterns(positions)

    x2d = x.reshape(batch * seq_len, d_model)
    for layer in range(depth):
        y_pool, hg, q, k, v, qi, ki, wi = _in_proj(x2d, norm1, w_in_pad, cos, slo, shi, pool_w_bd,
                                                pscale, layer, seq_len)
        y_hgrn = _hgrn(hg, lower_bounds, hnorm, layer, batch, seq_len)
        y_attn = _dsa(q, k, v, qi, ki, wi, batch, seq_len)
        x2d = _out_ffn(x2d, y_pool, y_hgrn, y_attn, w_out_b, norm2, w_ff_in_b, w_ff_out_b,
                       final_g, layer, layer == depth - 1, seq_len)
    return x2d.reshape(batch, seq_len, d_model)
```
